```python
import numpy as np
import jax
import jax.numpy as jnp
from jax import lax

D_MODEL = 1024
BATCH = 2
SEQ = 8192
DEPTH = 1
DEC_BATCH = 8
DEC_SEQ = 16
PAST_LEN = 4096

CHUNK = 64
HEAD_DIM = 128
N_HEADS = D_MODEL // HEAD_DIM
N_KV_HEADS = max(1, N_HEADS // 4)
ATTN_DIM = N_HEADS * HEAD_DIM
KV_DIM = N_KV_HEADS * HEAD_DIM
ROT_FRACTION = 4
ROPE_THETA = 500000.0
IDX_HEADS = 8
IDX_DIM = 64
TOPK_MAX = 256
D_CONV = D_MODEL
CONV_WIDTH = 3
N_MEM = 256
MEM_HEADS = 4
MEM_HEAD_DIM = D_MODEL // MEM_HEADS
MEM_DIM = MEM_HEADS * MEM_HEAD_DIM
N_BRANCH = 3
D_IN_PROJ = ATTN_DIM + 2 * KV_DIM + IDX_HEADS * IDX_DIM + IDX_DIM + IDX_HEADS + 3 * D_CONV + MEM_DIM + N_BRANCH * D_MODEL
N_EXPERTS = 64
TOP_K = 8
N_GROUPS = 8
TOPK_GROUPS = 4
D_EXPERT = D_MODEL // 4
ROUTED_SCALE = 2.5
LN_EPS = 1e-5
Q_BLOCK = 128
MOE_BLOCK = 128

kernel_name = 'hybrid_stream_dsa_conv_mem_moe_step'


def layer_norm(x, g, b):
    xf = x.astype(jnp.float32)
    mu = jnp.mean(xf, axis=-1, keepdims=True)
    var = jnp.mean(jnp.square(xf - mu), axis=-1, keepdims=True)
    y = (xf - mu) * lax.rsqrt(var + LN_EPS) * g.astype(jnp.float32) + b.astype(jnp.float32)
    return y.astype(x.dtype)


def partial_rotary(x, pos):
    rot = x.shape[-1] // ROT_FRACTION
    half = rot // 2
    inv_freq = ROPE_THETA ** (-jnp.arange(half, dtype=jnp.float32) / half)
    ang = pos.astype(jnp.float32)[:, None] * inv_freq[None, :]
    cos = jnp.cos(ang)[None, :, None, :]
    sin = jnp.sin(ang)[None, :, None, :]
    x1 = x[..., :half].astype(jnp.float32)
    x2 = x[..., half:rot].astype(jnp.float32)
    r = jnp.concatenate([x1 * cos - x2 * sin, x2 * cos + x1 * sin], axis=-1).astype(x.dtype)
    return jnp.concatenate([r, x[..., rot:]], axis=-1)


def dsa_attention(q, qi, wi, k_all, v_all, ki_all, q_pos, k_pos):
    B, T = q.shape[0], q.shape[1]
    L = k_all.shape[1]
    topk = min(TOPK_MAX, L // 4)
    tb = Q_BLOCK if T % Q_BLOCK == 0 else T
    nb = T // tb
    k_chunk = k_pos // CHUNK
    ki32 = ki_all.astype(jnp.float32)
    bidx = jnp.arange(B)[:, None, None]
    group = N_HEADS // N_KV_HEADS

    def to_blocks(a):
        return jnp.moveaxis(a.reshape((B, nb, tb) + a.shape[2:]), 1, 0)

    def block(args):
        qb, qib, wib, posb = args
        q_chunk = posb // CHUNK
        admissible = k_chunk[None, :] <= q_chunk[:, None]
        rel = jax.nn.relu(jnp.einsum('bthd,bsd->bths', qib.astype(jnp.float32), ki32)) * IDX_DIM ** -0.5
        score = jnp.einsum('bths,bth->bts', rel, wib.astype(jnp.float32))
        score = jnp.where(admissible[None], score, -jnp.inf)
        _, idx = lax.top_k(score, topk)
        valid = jnp.take(k_chunk, idx) <= q_chunk[None, :, None]
        kg = k_all[bidx, idx].astype(jnp.float32)
        vg = v_all[bidx, idx].astype(jnp.float32)
        qg = qb.astype(jnp.float32).reshape(B, tb, N_KV_HEADS, group, HEAD_DIM)
        s = jnp.einsum('btngd,btsnd->btngs', qg, kg) * HEAD_DIM ** -0.5
        s = jnp.where(valid[:, :, None, None, :], s, -jnp.inf)
        p = jax.nn.softmax(s, axis=-1)
        o = jnp.einsum('btngs,btsnd->btngd', p, vg)
        return o.reshape(B, tb, ATTN_DIM).astype(qb.dtype)

    out = lax.map(block, (to_blocks(q), to_blocks(qi), to_blocks(wi), q_pos.reshape(nb, tb)))
    return jnp.moveaxis(out, 0, 1).reshape(B, T, ATTN_DIM)


def short_conv(u, prev, conv_w):
    T = u.shape[1]
    padded = jnp.concatenate([prev.astype(u.dtype), u], axis=1)
    y = conv_w[0] * padded[:, 0:T]
    for j in range(1, CONV_WIDTH):
        y = y + conv_w[j] * padded[:, j:j + T]
    return y, padded[:, T:]


def memory_attention(mq, mem_k, mem_v):
    s = jnp.einsum('bthd,bshd->bhts', mq.astype(jnp.float32), mem_k.astype(jnp.float32)) * MEM_HEAD_DIM ** -0.5
    p = jax.nn.softmax(s, axis=-1)
    o = jnp.einsum('bhts,bshd->bthd', p, mem_v.astype(jnp.float32))
    return o.reshape(o.shape[0], o.shape[1], MEM_DIM).astype(mq.dtype)


def route(xb, w_router, b_router):
    n = xb.shape[0]
    s = jax.nn.sigmoid(xb.astype(jnp.float32) @ w_router.astype(jnp.float32))
    sc = s + b_router.astype(jnp.float32)
    gscore = lax.top_k(sc.reshape(n, N_GROUPS, N_EXPERTS // N_GROUPS), 2)[0].sum(-1)
    _, gidx = lax.top_k(gscore, TOPK_GROUPS)
    gmask = jnp.any(gidx[:, :, None] == jnp.arange(N_GROUPS)[None, None, :], axis=1)
    emask = jnp.repeat(gmask, N_EXPERTS // N_GROUPS, axis=1)
    _, eidx = lax.top_k(jnp.where(emask, sc, -jnp.inf), TOP_K)
    wsel = jnp.take_along_axis(s, eidx, axis=1)
    wsel = wsel / jnp.sum(wsel, axis=-1, keepdims=True) * ROUTED_SCALE
    return jnp.sum(jax.nn.one_hot(eidx, N_EXPERTS, dtype=jnp.float32) * wsel[..., None], axis=1)


def moe_ffn(x, w_router, b_router, w_exp_gu, w_exp_down, w_sh_gu, w_sh_down):
    shape = x.shape
    x2 = x.reshape(-1, shape[-1])
    n = x2.shape[0]
    blk = MOE_BLOCK if n % MOE_BLOCK == 0 else n

    def block(xb):
        gate = route(xb, w_router, b_router).astype(xb.dtype)
        g, u = jnp.split(jnp.einsum('nd,edf->nef', xb, w_exp_gu), 2, axis=-1)
        h = jax.nn.silu(g) * u * gate[..., None]
        routed = jnp.einsum('nef,efd->nd', h, w_exp_down)
        sg, su = jnp.split(xb @ w_sh_gu, 2, axis=-1)
        return routed + (jax.nn.silu(sg) * su) @ w_sh_down

    y = lax.map(block, x2.reshape(n // blk, blk, shape[-1]))
    return y.reshape(shape)


def trunk_layer(x, mem_k, mem_v, past_k, past_v, past_ki, conv_prev,
                w_in, conv_w, w_attn_out, w_conv_out, w_mem_out, w_o, ln1_g, ln1_b,
                w_router, b_router, w_exp_gu, w_exp_down, w_sh_gu, w_sh_down, ln2_g, ln2_b):
    B, T, D = x.shape
    P = past_k.shape[1]
    alpha = (2.0 * DEPTH) ** 0.25
    q_pos = P + jnp.arange(T, dtype=jnp.int32)
    k_pos = jnp.arange(P + T, dtype=jnp.int32)
    sizes = [ATTN_DIM, KV_DIM, KV_DIM, IDX_HEADS * IDX_DIM, IDX_DIM, IDX_HEADS,
             D_CONV, D_CONV, D_CONV, MEM_DIM, N_BRANCH * D_MODEL]
    offsets = np.cumsum(sizes)[:-1].tolist()
    q, k, v, qi, ki, wi, gate_b, gate_c, conv_in, mq, gate_logits = jnp.split(x @ w_in, offsets, axis=-1)
    q = partial_rotary(q.reshape(B, T, N_HEADS, HEAD_DIM), q_pos)
    k = partial_rotary(k.reshape(B, T, N_KV_HEADS, HEAD_DIM), q_pos)
    v = v.reshape(B, T, N_KV_HEADS, HEAD_DIM)
    qi = partial_rotary(qi.reshape(B, T, IDX_HEADS, IDX_DIM), q_pos)
    ki = partial_rotary(ki.reshape(B, T, 1, IDX_DIM), q_pos)[:, :, 0]
    wi = wi * IDX_HEADS ** -0.5
    k_all = jnp.concatenate([past_k.astype(k.dtype), k], axis=1)
    v_all = jnp.concatenate([past_v.astype(v.dtype), v], axis=1)
    ki_all = jnp.concatenate([past_ki.astype(ki.dtype), ki], axis=1)
    attn_out = dsa_attention(q, qi, wi, k_all, v_all, ki_all, q_pos, k_pos) @ w_attn_out
    conv_y, new_conv = short_conv(gate_c * conv_in, conv_prev, conv_w)
    conv_out = (gate_b * conv_y) @ w_conv_out
    mem_out = memory_attention(mq.reshape(B, T, MEM_HEADS, MEM_HEAD_DIM), mem_k, mem_v) @ w_mem_out
    g = jax.nn.sigmoid(gate_logits.astype(jnp.float32)).astype(x.dtype).reshape(B, T, N_BRANCH, D)
    mixed = (g[:, :, 0] * attn_out + g[:, :, 1] * conv_out + g[:, :, 2] * mem_out) @ w_o
    x = layer_norm(alpha * x + mixed, ln1_g, ln1_b)
    x = layer_norm(alpha * x + moe_ffn(x, w_router, b_router, w_exp_gu, w_exp_down, w_sh_gu, w_sh_down), ln2_g, ln2_b)
    return x, k, v, ki, new_conv


def setup_inputs(seed: int = 0) -> dict:
    key = jax.random.key(seed)
    ks = jax.random.split(key, 32)
    f32 = jnp.float32
    beta = (8.0 * DEPTH) ** -0.25

    def nrm(k, shape, scale):
        return jax.random.normal(k, shape, f32) * scale

    return {
        'x_prompt': nrm(ks[0], (BATCH, SEQ, D_MODEL), 1.0),
        'x_sample': nrm(ks[1], (DEC_BATCH, DEC_SEQ, D_MODEL), 1.0),
        'mem_prompt': nrm(ks[2], (BATCH, N_MEM, D_MODEL), 1.0),
        'cache_attn_k': nrm(ks[3], (DEPTH, DEC_BATCH, PAST_LEN, N_KV_HEADS, HEAD_DIM), 1.0),
        'cache_attn_v': nrm(ks[4], (DEPTH, DEC_BATCH, PAST_LEN, N_KV_HEADS, HEAD_DIM), 1.0),
        'cache_idx_k': nrm(ks[5], (DEPTH, DEC_BATCH, PAST_LEN, IDX_DIM), 1.0),
        'cache_mem_k': nrm(ks[6], (DEPTH, DEC_BATCH, N_MEM, MEM_HEADS, MEM_HEAD_DIM), 1.0),
        'cache_mem_v': nrm(ks[7], (DEPTH, DEC_BATCH, N_MEM, MEM_HEADS, MEM_HEAD_DIM), 1.0),
        'state_conv': nrm(ks[8], (DEPTH, DEC_BATCH, CONV_WIDTH - 1, D_CONV), 1.0),
        'ln_in_g': 1.0 + nrm(ks[9], (D_MODEL,), 0.02),
        'ln_in_b': nrm(ks[10], (D_MODEL,), 0.02),
        'w_in': nrm(ks[11], (DEPTH, D_MODEL, D_IN_PROJ), D_MODEL ** -0.5),
        'conv_w': nrm(ks[12], (DEPTH, CONV_WIDTH, D_CONV), CONV_WIDTH ** -0.5),
        'w_mem_kv': nrm(ks[13], (DEPTH, D_MODEL, 2 * MEM_DIM), D_MODEL ** -0.5),
        'w_attn_out': nrm(ks[14], (DEPTH, ATTN_DIM, D_MODEL), ATTN_DIM ** -0.5 * beta),
        'w_conv_out': nrm(ks[15], (DEPTH, D_CONV, D_MODEL), D_CONV ** -0.5 * beta),
        'w_mem_out': nrm(ks[16], (DEPTH, MEM_DIM, D_MODEL), MEM_DIM ** -0.5 * beta),
        'w_o': nrm(ks[17], (DEPTH, D_MODEL, D_MODEL), D_MODEL ** -0.5 * beta),
        'ln1_g': 1.0 + nrm(ks[18], (DEPTH, D_MODEL), 0.02),
        'ln1_b': nrm(ks[19], (DEPTH, D_MODEL), 0.02),
        'w_router': nrm(ks[20], (DEPTH, D_MODEL, N_EXPERTS), D_MODEL ** -0.5),
        'b_router': nrm(ks[21], (DEPTH, N_EXPERTS), 0.01),
        'w_exp_gu': nrm(ks[22], (DEPTH, N_EXPERTS, D_MODEL, 2 * D_EXPERT), D_MODEL ** -0.5),
        'w_exp_down': nrm(ks[23], (DEPTH, N_EXPERTS, D_EXPERT, D_MODEL), D_EXPERT ** -0.5 * beta),
        'w_sh_gu': nrm(ks[24], (DEPTH, D_MODEL, 2 * D_EXPERT), D_MODEL ** -0.5),
        'w_sh_down': nrm(ks[25], (DEPTH, D_EXPERT, D_MODEL), D_EXPERT ** -0.5 * beta),
        'ln2_g': 1.0 + nrm(ks[26], (DEPTH, D_MODEL), 0.02),
        'ln2_b': nrm(ks[27], (DEPTH, D_MODEL), 0.02),
    }


def reference(x_prompt, x_sample, mem_prompt, cache_attn_k, cache_attn_v, cache_idx_k, cache_mem_k, cache_mem_v,
              state_conv, ln_in_g, ln_in_b, w_in, conv_w, w_mem_kv, w_attn_out, w_conv_out, w_mem_out, w_o,
              ln1_g, ln1_b, w_router, b_router, w_exp_gu, w_exp_down, w_sh_gu, w_sh_down, ln2_g, ln2_b):
    xp = layer_norm(x_prompt, ln_in_g, ln_in_b)
    xs = layer_norm(x_sample, ln_in_g, ln_in_b)
    Bp = xp.shape[0]
    kp, vp, kip, mkp, mvp, cp = [], [], [], [], [], []
    ks, vs, kis, cs = [], [], [], []
    for l in range(DEPTH):
        lw = (w_in[l], conv_w[l], w_attn_out[l], w_conv_out[l], w_mem_out[l], w_o[l], ln1_g[l], ln1_b[l],
              w_router[l], b_router[l], w_exp_gu[l], w_exp_down[l], w_sh_gu[l], w_sh_down[l], ln2_g[l], ln2_b[l])
        mem_kv = (mem_prompt @ w_mem_kv[l]).reshape(Bp, N_MEM, 2, MEM_HEADS, MEM_HEAD_DIM)
        mem_k, mem_v = mem_kv[:, :, 0], mem_kv[:, :, 1]
        xp, k1, v1, ki1, c1 = trunk_layer(
            xp, mem_k, mem_v,
            jnp.zeros((Bp, 0, N_KV_HEADS, HEAD_DIM), xp.dtype),
            jnp.zeros((Bp, 0, N_KV_HEADS, HEAD_DIM), xp.dtype),
            jnp.zeros((Bp, 0, IDX_DIM), xp.dtype),
            jnp.zeros((Bp, CONV_WIDTH - 1, D_CONV), xp.dtype),
            *lw)
        xs, k2, v2, ki2, c2 = trunk_layer(
            xs, cache_mem_k[l], cache_mem_v[l], cache_attn_k[l], cache_attn_v[l], cache_idx_k[l], state_conv[l],
            *lw)
        kp.append(k1); vp.append(v1); kip.append(ki1); mkp.append(mem_k); mvp.append(mem_v); cp.append(c1)
        ks.append(k2); vs.append(v2); kis.append(ki2); cs.append(c2)
    return (xp, xs, jnp.stack(kp), jnp.stack(vp), jnp.stack(kip), jnp.stack(mkp), jnp.stack(mvp), jnp.stack(cp),
            jnp.stack(ks), jnp.stack(vs), jnp.stack(kis), jnp.stack(cs))
```

```python
import functools

import jax
import jax.numpy as jnp
import numpy as np
from jax import lax
from jax.experimental import pallas as pl
from jax.experimental.pallas import tpu as pltpu

D_MODEL = 1024
DEPTH = 1
CHUNK = 64
HEAD_DIM = 128
N_HEADS = D_MODEL // HEAD_DIM
N_KV_HEADS = max(1, N_HEADS // 4)
KV_GROUP = N_HEADS // N_KV_HEADS
ATTN_DIM = N_HEADS * HEAD_DIM
KV_DIM = N_KV_HEADS * HEAD_DIM
ROT_FRACTION = 4
ROPE_THETA = 500000.0
IDX_HEADS = 8
IDX_DIM = 64
TOPK_MAX = 256
D_CONV = D_MODEL
CONV_WIDTH = 3
N_MEM = 256
MEM_HEADS = 4
MEM_HEAD_DIM = D_MODEL // MEM_HEADS
MEM_DIM = MEM_HEADS * MEM_HEAD_DIM
N_BRANCH = 3
N_EXPERTS = 64
TOP_K = 8
N_GROUPS = 8
GROUP_SIZE = N_EXPERTS // N_GROUPS
TOPK_GROUPS = 4
D_EXPERT = D_MODEL // 4
ROUTED_SCALE = 2.5
LN_EPS = 1e-5
ALPHA = (2.0 * DEPTH) ** 0.25

LANES = 128
SUBLANES = 8
VMEM_LIMIT_BYTES = 56 * 1024 * 1024

Q_TILE = 128
KEY_TILE = 512
PROJ_ROWS = 256
MOE_ROWS = 1024
MOE_EXPERTS_PER_STEP = 4

F32 = jnp.float32
BF16 = jnp.bfloat16
I32 = jnp.int32
INT_MIN = -(2 ** 31)
INT_MAX = 2 ** 31 - 1
MASKED_LOGIT = -1e30

_NT = (((1,), (1,)), ((), ()))


def _dot(a, b):
    return jnp.dot(a, b, preferred_element_type=F32)


def _dot_nt(a, b):
    return lax.dot_general(a, b, _NT, preferred_element_type=F32)


def _layer_norm(x, g, b):
    mu = jnp.mean(x, axis=-1, keepdims=True)
    xc = x - mu
    var = jnp.mean(xc * xc, axis=-1, keepdims=True)
    return xc * lax.rsqrt(var + LN_EPS) * g + b


def _sigmoid(x):
    return 1.0 / (1.0 + jnp.exp(-x))


def _resident(shape):
    nd = len(shape)
    return pl.BlockSpec(shape, lambda *_: (0,) * nd, pipeline_mode=pl.Buffered(1))


def _params(*sem):
    return pltpu.CompilerParams(dimension_semantics=sem, vmem_limit_bytes=VMEM_LIMIT_BYTES)


def _rope_tables(pos, half, extra_scale_lanes=None, extra_scale=1.0):
    inv_freq = ROPE_THETA ** (-jnp.arange(half, dtype=F32) / half)
    ang = pos.astype(F32)[:, None] * inv_freq[None, :]
    cos, sin = jnp.cos(ang), jnp.sin(ang)
    n = pos.shape[0]
    c = jnp.concatenate([cos, cos, jnp.ones((n, LANES - 2 * half), F32)], axis=1)
    s = jnp.concatenate([-sin, sin, jnp.zeros((n, LANES - 2 * half), F32)], axis=1)
    if extra_scale_lanes is not None:
        lo, hi = extra_scale_lanes
        lane = jnp.arange(LANES)[None, :]
        c = jnp.where((lane >= lo) & (lane < hi), jnp.float32(extra_scale), c)
    return c, s


def _rope(x, c, s, half):
    rows, width = x.shape
    lane = lax.broadcasted_iota(I32, (rows, LANES), 1)
    first = lane < half
    outs = []
    for j in range(width // LANES):
        xj = x[:, j * LANES:(j + 1) * LANES]
        swap = jnp.where(first, pltpu.roll(xj, LANES - half, 1), pltpu.roll(xj, half, 1))
        outs.append(xj * c + swap * s)
    return outs[0] if len(outs) == 1 else jnp.concatenate(outs, axis=1)


def _proj_kernel(x_ref, w_ref, o_ref):
    o_ref[...] = _dot(x_ref[...].astype(BF16), w_ref[...])


def _project(x, w_bf, col_tile=512):
    n, d = x.shape
    m = w_bf.shape[1]
    return pl.pallas_call(
        _proj_kernel,
        grid=(m // col_tile,),
        in_specs=[pl.BlockSpec((n, d), lambda j: (0, 0)),
                  pl.BlockSpec((d, col_tile), lambda j: (0, j))],
        out_specs=pl.BlockSpec((n, col_tile), lambda j: (0, j)),
        out_shape=jax.ShapeDtypeStruct((n, m), F32),
        compiler_params=_params("arbitrary"),
        name="mem_kv_proj",
    )(x, w_bf)


def _qkv_kernel(x_ref, g_ref, b_ref, wq_ref, wqi_ref, wkv_ref, wkw_ref, ch_ref, sh_ref, ci_ref, si_ref,
                q_ref, qi_ref, k_ref, v_ref, kiwi_ref, kbf_ref, kibf_ref):
    xb = _layer_norm(x_ref[...], g_ref[...], b_ref[...]).astype(BF16)
    ch, sh, ci, si = ch_ref[...], sh_ref[...], ci_ref[...], si_ref[...]
    q_ref[...] = _rope(_dot(xb, wq_ref[...]), ch, sh, HEAD_DIM // ROT_FRACTION // 2).astype(BF16)
    qi_ref[...] = _rope(_dot(xb, wqi_ref[...]), ci, si, IDX_DIM // ROT_FRACTION // 2).astype(BF16)
    kv = _dot(xb, wkv_ref[...])
    k = _rope(kv[:, :KV_DIM], ch, sh, HEAD_DIM // ROT_FRACTION // 2)
    k_ref[...] = k
    kbf_ref[...] = k.astype(BF16)
    v_ref[...] = kv[:, KV_DIM:]
    kw = _rope(_dot(xb, wkw_ref[...]), ci, si, IDX_DIM // ROT_FRACTION // 2)
    kiwi_ref[...] = kw
    kibf_ref[...] = kw.astype(BF16)


def _qkv_project(x, ln_g, ln_b, w, tabs, rows, seq):
    n = x.shape[0]
    tm = min(rows, n)
    tab_blocks = max(seq, tm) // tm
    row = lambda width: pl.BlockSpec((tm, width), lambda i: (i, 0))
    tab = pl.BlockSpec((tm, LANES), lambda i: (i % tab_blocks, 0))
    out_shapes = (
        jax.ShapeDtypeStruct((n, ATTN_DIM), BF16),
        jax.ShapeDtypeStruct((n, IDX_HEADS * LANES), BF16),
        jax.ShapeDtypeStruct((n, KV_DIM), F32),
        jax.ShapeDtypeStruct((n, KV_DIM), F32),
        jax.ShapeDtypeStruct((n, LANES), F32),
        jax.ShapeDtypeStruct((n, KV_DIM), BF16),
        jax.ShapeDtypeStruct((n, LANES), BF16),
    )
    return pl.pallas_call(
        _qkv_kernel,
        grid=(n // tm,),
        in_specs=[row(D_MODEL), _resident((1, D_MODEL)), _resident((1, D_MODEL)),
                  _resident(w["q"].shape), _resident(w["qi"].shape), _resident(w["kv"].shape),
                  _resident(w["kw"].shape), tab, tab, tab, tab],
        out_specs=(row(ATTN_DIM), row(IDX_HEADS * LANES), row(KV_DIM), row(KV_DIM), row(LANES),
                   row(KV_DIM), row(LANES)),
        out_shape=out_shapes,
        compiler_params=_params("arbitrary"),
        name="qkv_proj",
    )(x, ln_g, ln_b, w["q"], w["qi"], w["kv"], w["kw"], *tabs)


def _sortable_key(score):
    bits = lax.bitcast_convert_type(score + 0.0, I32)
    return bits ^ ((bits >> 31) & INT_MAX)


def _dsa_kernel(q_ref, qi_ref, kiwi_ref, k_ref, v_ref, ki_ref, o_ref,
                vt_ref, key_ref, acc_ref, m_ref, l_ref, j_ref,
                *, past, seq, n_keys, topk, n_key_tiles):
    i = pl.program_id(1)
    tq, kt = Q_TILE, KEY_TILE

    @pl.when(i == 0)
    def _():
        for c in range(n_key_tiles):
            vt_ref[c] = v_ref[0, c * kt:(c + 1) * kt, :].T.astype(BF16)

    lane = lax.broadcasted_iota(I32, (1, tq), 1)
    t = i * tq + lane
    q_real = t < seq
    lim = jnp.minimum(((past + t) // CHUNK + 1) * CHUNK, n_keys)
    lim_max = jnp.minimum(((past + i * tq + tq - 1) // CHUNK + 1) * CHUNK, n_keys)
    n_tiles = (lim_max + kt - 1) // kt
    row = lax.broadcasted_iota(I32, (kt, tq), 0)

    wi = kiwi_ref[0].T[IDX_DIM:IDX_DIM + IDX_HEADS, :] * (IDX_DIM ** -0.5)

    def score_tile(j, carry):
        kib = ki_ref[0, pl.ds(pl.multiple_of(j * kt, kt), kt), :]
        acc = jnp.zeros((kt, tq), F32)
        for h in range(IDX_HEADS):
            rel = _dot_nt(kib, qi_ref[0, :, h * LANES:(h + 1) * LANES])
            acc = acc + jnp.maximum(rel, 0.0) * wi[h:h + 1, :]
        score = jnp.where(j * kt + row < lim, acc, -jnp.inf)
        key_ref[j] = _sortable_key(score)
        return carry

    lax.fori_loop(0, n_tiles, score_tile, 0)

    def count(pred):
        def body(j, acc):
            hit = pred(key_ref[j], j).astype(I32)
            return acc + jnp.sum(hit.reshape(kt // SUBLANES, SUBLANES, tq), axis=0)
        acc = lax.fori_loop(0, n_tiles, body, jnp.zeros((SUBLANES, tq), I32))
        return jnp.sum(acc, axis=0, keepdims=True)

    def count_ge(thr):
        return count(lambda blk, j: blk >= thr)

    thr0 = jnp.where(count_ge(jnp.zeros((1, tq), I32)) >= topk, 0, INT_MIN).astype(I32)

    def bisect(it, thr):
        cand = thr | jnp.left_shift(jnp.int32(1), 30 - it)
        return jnp.where(count_ge(cand) >= topk, cand, thr)

    thr = lax.fori_loop(0, 31, bisect, thr0)
    n_above = count_ge(thr + 1)
    n_ties_wanted = topk - n_above
    excess = jnp.where(q_real, count_ge(thr) - topk, 0)

    j_ref[...] = jnp.full((1, tq), INT_MAX, I32)

    @pl.when(jnp.max(excess) > 0)
    def _():
        n_bits = int(n_key_tiles * kt).bit_length()

        def bisect_idx(it, cut):
            cand = cut | jnp.left_shift(jnp.int32(1), n_bits - 1 - it)
            c = count(lambda blk, j: (blk == thr) & (j * kt + row < cand))
            return jnp.where(c <= n_ties_wanted, cand, cut)

        j_ref[...] = lax.fori_loop(0, n_bits, bisect_idx, jnp.zeros((1, tq), I32))

    j_cut = j_ref[...]

    m_ref[...] = jnp.full((N_HEADS, tq), MASKED_LOGIT, F32)
    l_ref[...] = jnp.zeros((N_HEADS, tq), F32)
    acc_ref[...] = jnp.zeros((N_HEADS, HEAD_DIM, tq), F32)
    scale = HEAD_DIM ** -0.5

    def attend_tile(j, carry):
        keys = key_ref[j]
        kidx = j * kt + row
        sel = ((keys > thr) | ((keys == thr) & (kidx < j_cut))) & (kidx < lim)
        start = pl.multiple_of(j * kt, kt)
        for n in range(N_KV_HEADS):
            kb = k_ref[0, pl.ds(start, kt), n * HEAD_DIM:(n + 1) * HEAD_DIM]
            vt = vt_ref[j, n * HEAD_DIM:(n + 1) * HEAD_DIM, :]
            for g in range(KV_GROUP):
                h = n * KV_GROUP + g
                s = _dot_nt(kb, q_ref[0, :, h * HEAD_DIM:(h + 1) * HEAD_DIM])
                s = jnp.where(sel, s, MASKED_LOGIT)
                m_old = m_ref[h:h + 1, :]
                m_new = jnp.maximum(m_old, jnp.max(s, axis=0, keepdims=True))
                a = jnp.exp((m_old - m_new) * scale)
                p = jnp.exp((s - m_new) * scale)
                l_ref[h:h + 1, :] = a * l_ref[h:h + 1, :] + jnp.sum(p, axis=0, keepdims=True)
                acc_ref[h] = a * acc_ref[h] + _dot(vt, p.astype(BF16))
                m_ref[h:h + 1, :] = m_new
        return carry

    lax.fori_loop(0, n_tiles, attend_tile, 0)

    for h in range(N_HEADS):
        o_ref[0, :, h * HEAD_DIM:(h + 1) * HEAD_DIM] = (acc_ref[h] / l_ref[h:h + 1, :]).T.astype(BF16)


def _dsa_attention(q, qi, kiwi, k_bf, v, ki_bf, *, past, seq, n_keys):
    b, tq_pad, _ = q.shape
    l_pad = k_bf.shape[1]
    n_key_tiles = l_pad // KEY_TILE
    topk = min(TOPK_MAX, n_keys // 4)
    qblk = lambda width: pl.BlockSpec((1, Q_TILE, width), lambda bi, i: (bi, i, 0))
    kblk = lambda width: pl.BlockSpec((1, l_pad, width), lambda bi, i: (bi, 0, 0),
                                      pipeline_mode=pl.Buffered(1))
    kern = functools.partial(_dsa_kernel, past=past, seq=seq, n_keys=n_keys, topk=topk,
                             n_key_tiles=n_key_tiles)
    return pl.pallas_call(
        kern,
        grid=(b, tq_pad // Q_TILE),
        in_specs=[qblk(ATTN_DIM), qblk(IDX_HEADS * LANES), qblk(LANES),
                  kblk(KV_DIM), kblk(KV_DIM), kblk(LANES)],
        out_specs=qblk(ATTN_DIM),
        out_shape=jax.ShapeDtypeStruct((b, tq_pad, ATTN_DIM), BF16),
        scratch_shapes=[
            pltpu.VMEM((n_key_tiles, KV_DIM, KEY_TILE), BF16),
            pltpu.VMEM((n_key_tiles, KEY_TILE, Q_TILE), I32),
            pltpu.VMEM((N_HEADS, HEAD_DIM, Q_TILE), F32),
            pltpu.VMEM((N_HEADS, Q_TILE), F32),
            pltpu.VMEM((N_HEADS, Q_TILE), F32),
            pltpu.VMEM((1, Q_TILE), I32),
        ],
        compiler_params=_params("arbitrary", "arbitrary"),
        name="dsa_attention",
    )(q, qi, kiwi, k_bf, v, ki_bf)


def _mixer_kernel(x_ref, attn_ref, memk_ref, memv_ref, prev_a_ref, prev_b_ref,
                  lng_ref, lnb_ref, cw_ref, ln1g_ref, ln1b_ref,
                  wgb_ref, wgc_ref, wci_ref, wmq_ref, wgl_ref, wao_ref, wco_ref, wmo_ref, wo_ref,
                  x1_ref, state_ref, carry_ref, *, seq, rows):
    i = pl.program_id(0)
    multi_seq = seq < rows
    xn = _layer_norm(x_ref[...], lng_ref[...], lnb_ref[...])
    xb = xn.astype(BF16)

    u = _dot(xb, wgc_ref[...]) * _dot(xb, wci_ref[...])
    ridx = lax.broadcasted_iota(I32, u.shape, 0)
    if multi_seq:
        tpos = ridx & (seq - 1)
        s1 = jnp.where(tpos == 0, prev_a_ref[...], pltpu.roll(u, 1, 0))
        s2 = jnp.where(tpos < 2, prev_b_ref[...], pltpu.roll(u, 2, 0))
        state_ref[...] = u
    else:
        blocks_per_seq = seq // rows
        @pl.when(i % blocks_per_seq == 0)
        def _():
            carry_ref[...] = prev_a_ref[0]

        prev = carry_ref[...]
        p1, p2 = prev[SUBLANES - 1:SUBLANES, :], prev[SUBLANES - 2:SUBLANES - 1, :]
        s1 = jnp.where(ridx == 0, p1, pltpu.roll(u, 1, 0))
        s2 = jnp.where(ridx == 0, p2, jnp.where(ridx == 1, p1, pltpu.roll(u, 2, 0)))
        carry_ref[...] = u[rows - SUBLANES:, :]
        state_ref[0] = u[rows - SUBLANES:, :]
    cw = cw_ref[...]
    conv = cw[0:1, :] * s2 + cw[1:2, :] * s1 + cw[2:3, :] * u
    conv_in = (_dot(xb, wgb_ref[...]) * conv).astype(BF16)

    mq = _dot(xb, wmq_ref[...]).astype(BF16)
    n_batches = max(1, rows // seq)
    rb = rows // n_batches
    mem_rows = []
    for bi in range(n_batches):
        heads = []
        for h in range(MEM_HEADS):
            cols = slice(h * MEM_HEAD_DIM, (h + 1) * MEM_HEAD_DIM)
            s = _dot_nt(mq[bi * rb:(bi + 1) * rb, cols], memk_ref[bi, :, cols]) * (MEM_HEAD_DIM ** -0.5)
            e = jnp.exp(s - jnp.max(s, axis=-1, keepdims=True))
            p = e / jnp.sum(e, axis=-1, keepdims=True)
            heads.append(_dot(p.astype(BF16), memv_ref[bi, :, cols]))
        mem_rows.append(jnp.concatenate(heads, axis=1))
    mem = (mem_rows[0] if n_batches == 1 else jnp.concatenate(mem_rows, axis=0)).astype(BF16)

    gate = _sigmoid(_dot(xb, wgl_ref[...]))
    mixed = (gate[:, :D_MODEL] * _dot(attn_ref[...], wao_ref[...])
             + gate[:, D_MODEL:2 * D_MODEL] * _dot(conv_in, wco_ref[...])
             + gate[:, 2 * D_MODEL:] * _dot(mem, wmo_ref[...]))
    y = ALPHA * xn + _dot(mixed.astype(BF16), wo_ref[...])
    x1_ref[...] = _layer_norm(y, ln1g_ref[...], ln1b_ref[...])


def _mixer(x, attn, mem_k, mem_v, prev_a, prev_b, ln_g, ln_b, conv_w8, ln1_g, ln1_b, w, *, seq, rows):
    n = x.shape[0]
    tm = min(rows, n)
    multi_seq = seq < tm
    n_batches = max(1, tm // seq)
    row = lambda width: pl.BlockSpec((tm, width), lambda i: (i, 0))
    mem = pl.BlockSpec((n_batches, N_MEM, MEM_DIM), lambda i: ((i * tm) // (seq * n_batches), 0, 0))
    vec = _resident((1, D_MODEL))
    if multi_seq:
        prev_spec = row(D_CONV)
        state_spec = row(D_CONV)
        state_shape = jax.ShapeDtypeStruct((n, D_CONV), F32)
    else:
        prev_spec = pl.BlockSpec((1, SUBLANES, D_CONV), lambda i: ((i * tm) // seq, 0, 0))
        state_spec = prev_spec
        state_shape = jax.ShapeDtypeStruct((n // seq, SUBLANES, D_CONV), F32)
    weights = [w[k] for k in ("gb", "gc", "ci", "mq", "gl", "ao", "co", "mo", "o")]
    kern = functools.partial(_mixer_kernel, seq=seq, rows=tm)
    return pl.pallas_call(
        kern,
        grid=(n // tm,),
        in_specs=[row(D_MODEL), row(ATTN_DIM), mem, mem, prev_spec, prev_spec,
                  vec, vec, _resident((SUBLANES, D_CONV)), vec, vec] + [_resident(m.shape) for m in weights],
        out_specs=(row(D_MODEL), state_spec),
        out_shape=(jax.ShapeDtypeStruct((n, D_MODEL), F32), state_shape),
        scratch_shapes=[pltpu.VMEM((SUBLANES, D_CONV), F32)],
        compiler_params=_params("arbitrary"),
        name="mixer",
    )(x, attn, mem_k, mem_v, prev_a, prev_b, ln_g, ln_b, conv_w8, ln1_g, ln1_b, *weights)


def _route(x, wr_hi, wr_lo, bias):
    tm = x.shape[0]
    x_hi = x.astype(BF16)
    x_lo = (x - x_hi.astype(F32)).astype(BF16)
    logits = _dot_nt(wr_hi, x_hi) + (_dot_nt(wr_hi, x_lo) + _dot_nt(wr_lo, x_hi))
    s = _sigmoid(logits)
    sc = s + bias[:, 0:1]
    sub = lax.broadcasted_iota(I32, (GROUP_SIZE, tm), 0)
    neg = -jnp.inf
    groups, gscore = [], []
    for g in range(N_GROUPS):
        xg = sc[g * GROUP_SIZE:(g + 1) * GROUP_SIZE, :]
        m1 = jnp.max(xg, axis=0, keepdims=True)
        i1 = jnp.min(jnp.where(xg == m1, sub, GROUP_SIZE), axis=0, keepdims=True)
        m2 = jnp.max(jnp.where(sub == i1, neg, xg), axis=0, keepdims=True)
        groups.append(xg)
        gscore.append(m1 + m2)
    masked = []
    for g in range(N_GROUPS):
        rank = jnp.zeros((1, tm), I32)
        for o in range(N_GROUPS):
            if o == g:
                continue
            beats = (gscore[o] > gscore[g]) | ((gscore[o] == gscore[g]) & (o < g))
            rank = rank + beats.astype(I32)
        masked.append(jnp.where(rank < TOPK_GROUPS, groups[g], neg))
    v = jnp.concatenate(masked, axis=0)
    eidx = lax.broadcasted_iota(I32, (N_EXPERTS, tm), 0)
    chosen = jnp.zeros((N_EXPERTS, tm), jnp.bool_)
    for _ in range(TOP_K):
        m = jnp.max(v, axis=0, keepdims=True)
        first = jnp.min(jnp.where(v == m, eidx, N_EXPERTS), axis=0, keepdims=True)
        hit = eidx == first
        chosen = chosen | hit
        v = jnp.where(hit, neg, v)
    wsel = jnp.where(chosen, s, 0.0)
    return wsel / jnp.sum(wsel, axis=0, keepdims=True) * ROUTED_SCALE


def _expert_ffn(xb, w_gu, w_down, gate_col=None):
    gu = _dot(xb, w_gu)
    g, u = gu[:, :D_EXPERT], gu[:, D_EXPERT:]
    h = g * _sigmoid(g) * u
    if gate_col is not None:
        h = h * gate_col
    return _dot(h.astype(BF16), w_down)


def _moe_kernel(x_ref, wrh_ref, wrl_ref, br_ref, wgu_ref, wd_ref, wsgu_ref, wsd_ref, g_ref, b_ref,
                o_ref, xb_ref, gate_ref, acc_ref, *, experts_per_step):
    j = pl.program_id(1)

    @pl.when(j == 0)
    def _():
        x = x_ref[...]
        gate_t = _route(x, wrh_ref[...], wrl_ref[...], br_ref[...])
        pad = jnp.zeros((LANES - N_EXPERTS, x.shape[0]), F32)
        gate_ref[...] = jnp.concatenate([gate_t, pad], axis=0).T
        xb = x.astype(BF16)
        xb_ref[...] = xb
        acc_ref[...] = _expert_ffn(xb, wsgu_ref[...], wsd_ref[...])

    xb = xb_ref[...]
    gate = gate_ref[...]
    lane = lax.broadcasted_iota(I32, gate.shape, 1)
    acc = acc_ref[...]
    for e in range(experts_per_step):
        col = jnp.sum(jnp.where(lane == j * experts_per_step + e, gate, 0.0), axis=1, keepdims=True)
        acc = acc + _expert_ffn(xb, wgu_ref[e], wd_ref[e], col)
    acc_ref[...] = acc

    @pl.when(j == pl.num_programs(1) - 1)
    def _():
        o_ref[...] = _layer_norm(ALPHA * x_ref[...] + acc_ref[...], g_ref[...], b_ref[...])


def _moe(x, w, ln_g, ln_b, rows):
    n = x.shape[0]
    tm = min(rows, n)
    eps = MOE_EXPERTS_PER_STEP
    row = pl.BlockSpec((tm, D_MODEL), lambda i, j: (i, 0))
    kern = functools.partial(_moe_kernel, experts_per_step=eps)
    return pl.pallas_call(
        kern,
        grid=(n // tm, N_EXPERTS // eps),
        in_specs=[row, _resident(w["r_hi"].shape), _resident(w["r_lo"].shape), _resident(w["r_b"].shape),
                  pl.BlockSpec((eps, D_MODEL, 2 * D_EXPERT), lambda i, j: (j, 0, 0)),
                  pl.BlockSpec((eps, D_EXPERT, D_MODEL), lambda i, j: (j, 0, 0)),
                  _resident(w["s_gu"].shape), _resident(w["s_d"].shape),
                  _resident((1, D_MODEL)), _resident((1, D_MODEL))],
        out_specs=row,
        out_shape=jax.ShapeDtypeStruct((n, D_MODEL), F32),
        scratch_shapes=[pltpu.VMEM((tm, D_MODEL), BF16), pltpu.VMEM((tm, LANES), F32),
                        pltpu.VMEM((tm, D_MODEL), F32)],
        compiler_params=_params("arbitrary", "arbitrary"),
        name="moe",
    )(x, w["r_hi"], w["r_lo"], w["r_b"], w["e_gu"], w["e_d"], w["s_gu"], w["s_d"], ln_g, ln_b)


def _pad_rows(a, rows):
    return a if a.shape[1] == rows else jnp.pad(a, ((0, 0), (0, rows - a.shape[1]), (0, 0)))


def _layer(x, mem_k_bf, mem_v_bf, past_k, past_v, past_ki, conv_prev, w, ln, layer):
    b, seq, _ = x.shape
    past = past_k.shape[1]
    n = b * seq
    n_keys = past + seq
    x2 = x.reshape(n, D_MODEL)

    pos = past + jnp.arange(seq, dtype=I32)
    reps = max(1, min(PROJ_ROWS, n) // seq)
    pos_t = jnp.tile(pos, reps)
    tabs = (_rope_tables(pos_t, HEAD_DIM // ROT_FRACTION // 2)
            + _rope_tables(pos_t, IDX_DIM // ROT_FRACTION // 2,
                           (IDX_DIM, IDX_DIM + IDX_HEADS), IDX_HEADS ** -0.5))
    q, qi, k, v, kiwi, k_bf, ki_bf = _qkv_project(x2, ln["in_g"], ln["in_b"], w, tabs, PROJ_ROWS, seq)

    l_pad = -(-n_keys // KEY_TILE) * KEY_TILE
    tq_pad = -(-seq // Q_TILE) * Q_TILE
    k3 = jnp.concatenate([past_k.reshape(b, past, KV_DIM).astype(BF16), k_bf.reshape(b, seq, KV_DIM)], axis=1)
    v3 = jnp.concatenate([past_v.reshape(b, past, KV_DIM), v.reshape(b, seq, KV_DIM)], axis=1)
    ki_past = jnp.pad(past_ki, ((0, 0), (0, 0), (0, LANES - IDX_DIM))).astype(BF16)
    ki3 = jnp.concatenate([ki_past, ki_bf.reshape(b, seq, LANES)], axis=1)
    attn = _dsa_attention(
        _pad_rows(q.reshape(b, seq, ATTN_DIM), tq_pad), _pad_rows(qi.reshape(b, seq, IDX_HEADS * LANES), tq_pad),
        _pad_rows(kiwi.reshape(b, seq, LANES), tq_pad),
        _pad_rows(k3, l_pad), _pad_rows(v3, l_pad), _pad_rows(ki3, l_pad),
        past=past, seq=seq, n_keys=n_keys)
    attn = attn[:, :seq].reshape(n, ATTN_DIM)

    tm = min(PROJ_ROWS, n)
    if seq < tm:
        tpos = jnp.arange(seq)[None, :, None]
        prev_a = jnp.where(tpos == 0, conv_prev[:, 1:2], 0.0).reshape(n, D_CONV)
        prev_b = jnp.where(tpos == 0, conv_prev[:, 0:1], jnp.where(tpos == 1, conv_prev[:, 1:2], 0.0)).reshape(n, D_CONV)
    else:
        prev_a = jnp.pad(conv_prev, ((0, 0), (SUBLANES - (CONV_WIDTH - 1), 0), (0, 0)))
        prev_b = prev_a
    x1, state = _mixer(x2, attn, mem_k_bf, mem_v_bf, prev_a, prev_b, ln["in_g"], ln["in_b"], w["conv"],
                       ln["g1"], ln["b1"], w, seq=seq, rows=PROJ_ROWS)
    if seq < tm:
        new_conv = state.reshape(b, seq, D_CONV)[:, seq - (CONV_WIDTH - 1):]
    else:
        new_conv = state[:, SUBLANES - (CONV_WIDTH - 1):]

    y = _moe(x1, w, ln["g2"], ln["b2"], MOE_ROWS)
    return (y.reshape(b, seq, D_MODEL), k.reshape(b, seq, N_KV_HEADS, HEAD_DIM),
            v.reshape(b, seq, N_KV_HEADS, HEAD_DIM), kiwi[:, :IDX_DIM].reshape(b, seq, IDX_DIM), new_conv)


def _prep_weights(l, w_in, conv_w, w_attn_out, w_conv_out, w_mem_out, w_o, w_router, b_router,
                  w_exp_gu, w_exp_down, w_sh_gu, w_sh_down):
    sizes = [ATTN_DIM, KV_DIM, KV_DIM, IDX_HEADS * IDX_DIM, IDX_DIM, IDX_HEADS,
             D_CONV, D_CONV, D_CONV, MEM_DIM, N_BRANCH * D_MODEL]
    offs = np.concatenate([[0], np.cumsum(sizes)]).tolist()
    wi = w_in[l]
    part = lambda j: wi[:, offs[j]:offs[j + 1]]
    w_qi = part(3).reshape(D_MODEL, IDX_HEADS, IDX_DIM)
    w_qi = jnp.pad(w_qi, ((0, 0), (0, 0), (0, LANES - IDX_DIM))).reshape(D_MODEL, IDX_HEADS * LANES)
    w_kw = jnp.pad(jnp.concatenate([part(4), part(5)], axis=1), ((0, 0), (0, LANES - IDX_DIM - IDX_HEADS)))
    wr = w_router[l].T
    wr_hi = wr.astype(BF16)
    return {
        "q": part(0).astype(BF16), "kv": jnp.concatenate([part(1), part(2)], axis=1).astype(BF16),
        "qi": w_qi.astype(BF16), "kw": w_kw.astype(BF16),
        "gb": part(6).astype(BF16), "gc": part(7).astype(BF16), "ci": part(8).astype(BF16),
        "mq": part(9).astype(BF16), "gl": part(10).astype(BF16),
        "ao": w_attn_out[l].astype(BF16), "co": w_conv_out[l].astype(BF16), "mo": w_mem_out[l].astype(BF16),
        "o": w_o[l].astype(BF16),
        "conv": jnp.pad(conv_w[l], ((0, SUBLANES - CONV_WIDTH), (0, 0))),
        "r_hi": wr_hi, "r_lo": (wr - wr_hi.astype(F32)).astype(BF16),
        "r_b": jnp.broadcast_to(b_router[l][:, None], (N_EXPERTS, LANES)),
        "e_gu": w_exp_gu[l].astype(BF16), "e_d": w_exp_down[l].astype(BF16),
        "s_gu": w_sh_gu[l].astype(BF16), "s_d": w_sh_down[l].astype(BF16),
    }


def kernel(x_prompt, x_sample, mem_prompt, cache_attn_k, cache_attn_v, cache_idx_k, cache_mem_k, cache_mem_v,
           state_conv, ln_in_g, ln_in_b, w_in, conv_w, w_mem_kv, w_attn_out, w_conv_out, w_mem_out, w_o,
           ln1_g, ln1_b, w_router, b_router, w_exp_gu, w_exp_down, w_sh_gu, w_sh_down, ln2_g, ln2_b):
    assert DEPTH == 1 and w_in.shape[0] == 1
    l = 0
    bp, n_mem = mem_prompt.shape[0], mem_prompt.shape[1]
    w = _prep_weights(l, w_in, conv_w, w_attn_out, w_conv_out, w_mem_out, w_o, w_router, b_router,
                      w_exp_gu, w_exp_down, w_sh_gu, w_sh_down)
    ln = {"in_g": ln_in_g[None, :], "in_b": ln_in_b[None, :], "g1": ln1_g[l][None, :], "b1": ln1_b[l][None, :],
          "g2": ln2_g[l][None, :], "b2": ln2_b[l][None, :]}

    mem_kv = _project(mem_prompt.reshape(bp * n_mem, D_MODEL), w_mem_kv[l].astype(BF16))
    mem_k = mem_kv[:, :MEM_DIM].reshape(bp, n_mem, MEM_DIM)
    mem_v = mem_kv[:, MEM_DIM:].reshape(bp, n_mem, MEM_DIM)

    zeros = lambda *s: jnp.zeros(s, F32)
    yp, kp, vp, kip, cp = _layer(
        x_prompt, mem_k.astype(BF16), mem_v.astype(BF16),
        zeros(bp, 0, N_KV_HEADS, HEAD_DIM), zeros(bp, 0, N_KV_HEADS, HEAD_DIM), zeros(bp, 0, IDX_DIM),
        zeros(bp, CONV_WIDTH - 1, D_CONV), w, ln, l)
    bs = x_sample.shape[0]
    ys, ks, vs, kis, cs = _layer(
        x_sample, cache_mem_k[l].reshape(bs, n_mem, MEM_DIM).astype(BF16),
        cache_mem_v[l].reshape(bs, n_mem, MEM_DIM).astype(BF16),
        cache_attn_k[l], cache_attn_v[l], cache_idx_k[l], state_conv[l], w, ln, l)

    mem_shape = (1, bp, n_mem, MEM_HEADS, MEM_HEAD_DIM)
    return (yp, ys, kp[None], vp[None], kip[None], mem_k.reshape(mem_shape), mem_v.reshape(mem_shape), cp[None],
            ks[None], vs[None], kis[None], cs[None])
```

```python
import functools

import jax
import jax.numpy as jnp
import numpy as np
from jax import lax
from jax.experimental import pallas as pl
from jax.experimental.pallas import tpu as pltpu

D_MODEL = 1024
DEPTH = 1
CHUNK = 64
HEAD_DIM = 128
N_HEADS = D_MODEL // HEAD_DIM
N_KV_HEADS = max(1, N_HEADS // 4)
KV_GROUP = N_HEADS // N_KV_HEADS
ATTN_DIM = N_HEADS * HEAD_DIM
KV_DIM = N_KV_HEADS * HEAD_DIM
ROT_FRACTION = 4
ROPE_THETA = 500000.0
IDX_HEADS = 8
IDX_DIM = 64
TOPK_MAX = 256
D_CONV = D_MODEL
CONV_WIDTH = 3
N_MEM = 256
MEM_HEADS = 4
MEM_HEAD_DIM = D_MODEL // MEM_HEADS
MEM_DIM = MEM_HEADS * MEM_HEAD_DIM
N_BRANCH = 3
N_EXPERTS = 64
TOP_K = 8
N_GROUPS = 8
GROUP_SIZE = N_EXPERTS // N_GROUPS
TOPK_GROUPS = 4
D_EXPERT = D_MODEL // 4
ROUTED_SCALE = 2.5
LN_EPS = 1e-5
ALPHA = (2.0 * DEPTH) ** 0.25

LANES = 128
SUBLANES = 8
VMEM_LIMIT_BYTES = 56 * 1024 * 1024

Q_TILE = 128
KEY_TILE = 512
DENOM_ROWS = 16
BITS_PER_CHECK = 4
PROJ_ROWS = 256
MOE_ROWS = 1024
MOE_EXPERTS_PER_STEP = 4

F32 = jnp.float32
BF16 = jnp.bfloat16
I32 = jnp.int32
INT_MIN = -(2 ** 31)
INT_MAX = 2 ** 31 - 1
MASKED_LOGIT = -1e30
QK_SCALE_LOG2E = HEAD_DIM ** -0.5 * float(np.log2(np.e))

_NT = (((1,), (1,)), ((), ()))


def _dot(a, b):
    return jnp.dot(a, b, preferred_element_type=F32)


def _dot_nt(a, b):
    return lax.dot_general(a, b, _NT, preferred_element_type=F32)


def _layer_norm(x, g, b):
    mu = jnp.mean(x, axis=-1, keepdims=True)
    xc = x - mu
    var = jnp.mean(xc * xc, axis=-1, keepdims=True)
    return xc * lax.rsqrt(var + LN_EPS) * g + b


def _sigmoid(x):
    return 1.0 / (1.0 + jnp.exp(-x))


def _resident(shape):
    nd = len(shape)
    return pl.BlockSpec(shape, lambda *_: (0,) * nd, pipeline_mode=pl.Buffered(1))


def _params(*sem):
    return pltpu.CompilerParams(dimension_semantics=sem, vmem_limit_bytes=VMEM_LIMIT_BYTES)


def _rope_tables(pos, half, extra_scale_lanes=None, extra_scale=1.0):
    inv_freq = ROPE_THETA ** (-jnp.arange(half, dtype=F32) / half)
    ang = pos.astype(F32)[:, None] * inv_freq[None, :]
    cos, sin = jnp.cos(ang), jnp.sin(ang)
    n = pos.shape[0]
    c = jnp.concatenate([cos, cos, jnp.ones((n, LANES - 2 * half), F32)], axis=1)
    s = jnp.concatenate([-sin, sin, jnp.zeros((n, LANES - 2 * half), F32)], axis=1)
    if extra_scale_lanes is not None:
        lo, hi = extra_scale_lanes
        lane = jnp.arange(LANES)[None, :]
        c = jnp.where((lane >= lo) & (lane < hi), jnp.float32(extra_scale), c)
    return c, s


def _rope(x, c, s, half):
    rows, width = x.shape
    lane = lax.broadcasted_iota(I32, (rows, LANES), 1)
    first = lane < half
    outs = []
    for j in range(width // LANES):
        xj = x[:, j * LANES:(j + 1) * LANES]
        swap = jnp.where(first, pltpu.roll(xj, LANES - half, 1), pltpu.roll(xj, half, 1))
        outs.append(xj * c + swap * s)
    return outs[0] if len(outs) == 1 else jnp.concatenate(outs, axis=1)


def _proj_kernel(x_ref, w_ref, o_ref):
    o_ref[...] = _dot(x_ref[...].astype(BF16), w_ref[...])


def _project(x, w_bf, col_tile=512):
    n, d = x.shape
    m = w_bf.shape[1]
    return pl.pallas_call(
        _proj_kernel,
        grid=(m // col_tile,),
        in_specs=[pl.BlockSpec((n, d), lambda j: (0, 0)),
                  pl.BlockSpec((d, col_tile), lambda j: (0, j))],
        out_specs=pl.BlockSpec((n, col_tile), lambda j: (0, j)),
        out_shape=jax.ShapeDtypeStruct((n, m), F32),
        compiler_params=_params("arbitrary"),
        name="mem_kv_proj",
    )(x, w_bf)


def _store_head_major(ref, x):
    for r in range(ref.shape[0]):
        for h in range(ref.shape[1]):
            ref[r, h] = x[r * Q_TILE:(r + 1) * Q_TILE, h * LANES:(h + 1) * LANES]


def _qkv_kernel(x_ref, g_ref, b_ref, wq_ref, wqi_ref, wkv_ref, wkw_ref, ch_ref, sh_ref, ci_ref, si_ref,
                q_ref, qi_ref, k_ref, v_ref, kiwi_ref, kbf_ref, kibf_ref):
    xb = _layer_norm(x_ref[...], g_ref[...], b_ref[...]).astype(BF16)
    ch, sh, ci, si = ch_ref[...], sh_ref[...], ci_ref[...], si_ref[...]
    q = _rope(_dot(xb, wq_ref[...]), ch, sh, HEAD_DIM // ROT_FRACTION // 2)
    _store_head_major(q_ref, (q * QK_SCALE_LOG2E).astype(BF16))
    qi = _rope(_dot(xb, wqi_ref[...]), ci, si, IDX_DIM // ROT_FRACTION // 2)
    _store_head_major(qi_ref, qi.astype(BF16))
    kv = _dot(xb, wkv_ref[...])
    k = _rope(kv[:, :KV_DIM], ch, sh, HEAD_DIM // ROT_FRACTION // 2)
    k_ref[...] = k
    kbf_ref[...] = k.astype(BF16)
    v_ref[...] = kv[:, KV_DIM:]
    kw = _rope(_dot(xb, wkw_ref[...]), ci, si, IDX_DIM // ROT_FRACTION // 2)
    kiwi_ref[...] = kw
    kibf_ref[...] = kw.astype(BF16)


def _qkv_project(x, ln_g, ln_b, w, tabs, rows, seq):
    n = x.shape[0]
    tm = min(rows, n)
    tab_blocks = max(seq, tm) // tm
    row = lambda width: pl.BlockSpec((tm, width), lambda i: (i, 0))
    tab = pl.BlockSpec((tm, LANES), lambda i: (i % tab_blocks, 0))
    hm_shape = lambda heads: (n // Q_TILE, heads, Q_TILE, LANES)
    hm = lambda heads: pl.BlockSpec((tm // Q_TILE, heads, Q_TILE, LANES), lambda i: (i, 0, 0, 0))
    out_shapes = (
        jax.ShapeDtypeStruct(hm_shape(N_HEADS), BF16),
        jax.ShapeDtypeStruct(hm_shape(IDX_HEADS), BF16),
        jax.ShapeDtypeStruct((n, KV_DIM), F32),
        jax.ShapeDtypeStruct((n, KV_DIM), F32),
        jax.ShapeDtypeStruct((n, LANES), F32),
        jax.ShapeDtypeStruct((n, KV_DIM), BF16),
        jax.ShapeDtypeStruct((n, LANES), BF16),
    )
    return pl.pallas_call(
        _qkv_kernel,
        grid=(n // tm,),
        in_specs=[row(D_MODEL), _resident((1, D_MODEL)), _resident((1, D_MODEL)),
                  _resident(w["q"].shape), _resident(w["qi"].shape), _resident(w["kv"].shape),
                  _resident(w["kw"].shape), tab, tab, tab, tab],
        out_specs=(hm(N_HEADS), hm(IDX_HEADS), row(KV_DIM), row(KV_DIM), row(LANES),
                   row(KV_DIM), row(LANES)),
        out_shape=out_shapes,
        compiler_params=_params("arbitrary"),
        name="qkv_proj",
    )(x, ln_g, ln_b, w["q"], w["qi"], w["kv"], w["kw"], *tabs)


def _sortable_key(score):
    bits = lax.bitcast_convert_type(score + 0.0, I32)
    return bits ^ ((bits >> 31) & INT_MAX)


def _dsa_kernel(q_ref, qi_ref, kiwi_ref, k_ref, v_ref, ki_ref, o_ref,
                vt_ref, key_ref, acc_ref, m_ref, j_ref,
                *, past, seq, n_keys, topk, n_key_tiles):
    i = pl.program_id(1)
    tq, kt = Q_TILE, KEY_TILE
    gq = KV_GROUP * tq

    @pl.when(i == 0)
    def _():
        ones = jnp.ones((DENOM_ROWS, kt), BF16)
        for c in range(n_key_tiles):
            for n in range(N_KV_HEADS):
                vt_ref[c, n, :HEAD_DIM, :] = (
                    v_ref[0, c * kt:(c + 1) * kt, n * HEAD_DIM:(n + 1) * HEAD_DIM].T.astype(BF16))
                vt_ref[c, n, HEAD_DIM:, :] = ones

    lane = lax.broadcasted_iota(I32, (1, tq), 1)
    t = i * tq + lane
    q_real = t < seq
    lim = jnp.minimum(((past + t) // CHUNK + 1) * CHUNK, n_keys)
    lim_max = jnp.minimum(((past + i * tq + tq - 1) // CHUNK + 1) * CHUNK, n_keys)
    n_tiles = (lim_max + kt - 1) // kt
    row = lax.broadcasted_iota(I32, (kt, tq), 0)

    wi = kiwi_ref[0].T[IDX_DIM:IDX_DIM + IDX_HEADS, :] * (IDX_DIM ** -0.5)

    def score_tile(j, carry):
        kib = ki_ref[0, pl.ds(pl.multiple_of(j * kt, kt), kt), :]
        acc = jnp.zeros((kt, tq), F32)
        for c in range(IDX_HEADS // KV_GROUP):
            rel = _dot_nt(kib, qi_ref[0, 0, c * gq:(c + 1) * gq, :])
            for g in range(KV_GROUP):
                h = c * KV_GROUP + g
                acc = acc + jnp.maximum(rel[:, g * tq:(g + 1) * tq], 0.0) * wi[h:h + 1, :]
        score = jnp.where(j * kt + row < lim, acc, -jnp.inf)
        key_ref[j] = _sortable_key(score)
        return carry

    lax.fori_loop(0, n_tiles, score_tile, 0)

    def count(pred):
        def body(j, acc):
            hit = pred(key_ref[j], j).astype(I32)
            return acc + jnp.sum(hit.reshape(kt // SUBLANES, SUBLANES, tq), axis=0)
        acc = lax.fori_loop(0, n_tiles, body, jnp.zeros((SUBLANES, tq), I32))
        return jnp.sum(acc, axis=0, keepdims=True)

    def count_ge(thr):
        return count(lambda blk, j: blk >= thr)

    def unresolved(cnt):
        return jnp.max(jnp.where(q_real, cnt, topk)) > topk

    c0 = count_ge(jnp.zeros((1, tq), I32))
    thr0 = jnp.where(c0 >= topk, 0, INT_MIN).astype(I32)
    cnt0 = jnp.where(c0 >= topk, c0, n_tiles * kt).astype(I32)

    def bisect(state):
        it, thr, cnt, _ = state
        for b in range(BITS_PER_CHECK):
            bit = 30 - (it + b)
            cand = thr | jnp.where(bit >= 0, jnp.left_shift(jnp.int32(1), jnp.maximum(bit, 0)), 0)
            c = count_ge(cand)
            take = c >= topk
            thr, cnt = jnp.where(take, cand, thr), jnp.where(take, c, cnt)
        return it + BITS_PER_CHECK, thr, cnt, unresolved(cnt)

    _, thr, cnt, tied = lax.while_loop(lambda st: (st[0] < 31) & st[3], bisect,
                                       (jnp.int32(0), thr0, cnt0, unresolved(cnt0)))

    j_ref[...] = jnp.full((1, tq), INT_MAX, I32)

    @pl.when(tied)
    def _():
        n_ties_wanted = topk - count_ge(thr + 1)
        n_bits = int(n_key_tiles * kt).bit_length()

        def bisect_idx(it, cut):
            cand = cut | jnp.left_shift(jnp.int32(1), n_bits - 1 - it)
            c = count(lambda blk, j: (blk == thr) & (j * kt + row < cand))
            return jnp.where(c <= n_ties_wanted, cand, cut)

        j_ref[...] = lax.fori_loop(0, n_bits, bisect_idx, jnp.zeros((1, tq), I32))

    j_cut = j_ref[...]

    m_ref[...] = jnp.full(m_ref.shape, MASKED_LOGIT, F32)
    acc_ref[...] = jnp.zeros(acc_ref.shape, F32)

    def attend_tile(j, carry):
        keys = key_ref[j]
        kidx = j * kt + row
        sel = ((keys > thr) | ((keys == thr) & (kidx < j_cut))) & (kidx < lim)
        start = pl.multiple_of(j * kt, kt)
        for n in range(N_KV_HEADS):
            kb = k_ref[0, pl.ds(start, kt), n * HEAD_DIM:(n + 1) * HEAD_DIM]
            s = _dot_nt(kb, q_ref[0, 0, n * gq:(n + 1) * gq, :])
            s = jnp.concatenate([jnp.where(sel, s[:, g * tq:(g + 1) * tq], MASKED_LOGIT)
                                 for g in range(KV_GROUP)], axis=1)
            m_old = m_ref[n]
            m_new = jnp.maximum(m_old, jnp.max(s, axis=0, keepdims=True))
            p = jnp.exp2(s - m_new).astype(BF16)
            acc_ref[n] = jnp.exp2(m_old - m_new) * acc_ref[n] + _dot(vt_ref[j, n], p)
            m_ref[n] = m_new
        return carry

    lax.fori_loop(0, n_tiles, attend_tile, 0)

    for h in range(N_HEADS):
        n, cols = h // KV_GROUP, slice((h % KV_GROUP) * tq, (h % KV_GROUP + 1) * tq)
        o = acc_ref[n, :HEAD_DIM, cols] / acc_ref[n, HEAD_DIM:HEAD_DIM + 1, cols]
        o_ref[0, :, h * HEAD_DIM:(h + 1) * HEAD_DIM] = o.T.astype(BF16)


def _dsa_attention(q, qi, kiwi, k_bf, v, ki_bf, *, past, seq, n_keys):
    b, tq_pad, _ = kiwi.shape
    l_pad = k_bf.shape[1]
    n_key_tiles = l_pad // KEY_TILE
    topk = min(TOPK_MAX, n_keys // 4)
    qblk = lambda width: pl.BlockSpec((1, Q_TILE, width), lambda bi, i: (bi, i, 0))
    hmblk = lambda heads: pl.BlockSpec((1, 1, heads * Q_TILE, LANES), lambda bi, i: (bi, i, 0, 0))
    kblk = lambda width: pl.BlockSpec((1, l_pad, width), lambda bi, i: (bi, 0, 0),
                                      pipeline_mode=pl.Buffered(1))
    kern = functools.partial(_dsa_kernel, past=past, seq=seq, n_keys=n_keys, topk=topk,
                             n_key_tiles=n_key_tiles)
    return pl.pallas_call(
        kern,
        grid=(b, tq_pad // Q_TILE),
        in_specs=[hmblk(N_HEADS), hmblk(IDX_HEADS), qblk(LANES),
                  kblk(KV_DIM), kblk(KV_DIM), kblk(LANES)],
        out_specs=qblk(ATTN_DIM),
        out_shape=jax.ShapeDtypeStruct((b, tq_pad, ATTN_DIM), BF16),
        scratch_shapes=[
            pltpu.VMEM((n_key_tiles, N_KV_HEADS, HEAD_DIM + DENOM_ROWS, KEY_TILE), BF16),
            pltpu.VMEM((n_key_tiles, KEY_TILE, Q_TILE), I32),
            pltpu.VMEM((N_KV_HEADS, HEAD_DIM + DENOM_ROWS, KV_GROUP * Q_TILE), F32),
            pltpu.VMEM((N_KV_HEADS, 1, KV_GROUP * Q_TILE), F32),
            pltpu.VMEM((1, Q_TILE), I32),
        ],
        compiler_params=_params("arbitrary", "arbitrary"),
        name="dsa_attention",
    )(q, qi, kiwi, k_bf, v, ki_bf)


def _mixer_kernel(x_ref, attn_ref, memk_ref, memv_ref, prev_a_ref, prev_b_ref,
                  lng_ref, lnb_ref, cw_ref, ln1g_ref, ln1b_ref,
                  wgb_ref, wgc_ref, wci_ref, wmq_ref, wgl_ref, wao_ref, wco_ref, wmo_ref, wo_ref,
                  x1_ref, state_ref, carry_ref, *, seq, rows):
    i = pl.program_id(0)
    multi_seq = seq < rows
    xn = _layer_norm(x_ref[...], lng_ref[...], lnb_ref[...])
    xb = xn.astype(BF16)

    u = _dot(xb, wgc_ref[...]) * _dot(xb, wci_ref[...])
    ridx = lax.broadcasted_iota(I32, u.shape, 0)
    if multi_seq:
        tpos = ridx & (seq - 1)
        s1 = jnp.where(tpos == 0, prev_a_ref[...], pltpu.roll(u, 1, 0))
        s2 = jnp.where(tpos < 2, prev_b_ref[...], pltpu.roll(u, 2, 0))
        state_ref[...] = u
    else:
        blocks_per_seq = seq // rows
        @pl.when(i % blocks_per_seq == 0)
        def _():
            carry_ref[...] = prev_a_ref[0]

        prev = carry_ref[...]
        p1, p2 = prev[SUBLANES - 1:SUBLANES, :], prev[SUBLANES - 2:SUBLANES - 1, :]
        s1 = jnp.where(ridx == 0, p1, pltpu.roll(u, 1, 0))
        s2 = jnp.where(ridx == 0, p2, jnp.where(ridx == 1, p1, pltpu.roll(u, 2, 0)))
        carry_ref[...] = u[rows - SUBLANES:, :]
        state_ref[0] = u[rows - SUBLANES:, :]
    cw = cw_ref[...]
    conv = cw[0:1, :] * s2 + cw[1:2, :] * s1 + cw[2:3, :] * u
    conv_in = (_dot(xb, wgb_ref[...]) * conv).astype(BF16)

    mq = _dot(xb, wmq_ref[...]).astype(BF16)
    n_batches = max(1, rows // seq)
    rb = rows // n_batches
    mem_rows = []
    for bi in range(n_batches):
        heads = []
        for h in range(MEM_HEADS):
            cols = slice(h * MEM_HEAD_DIM, (h + 1) * MEM_HEAD_DIM)
            s = _dot_nt(mq[bi * rb:(bi + 1) * rb, cols], memk_ref[bi, :, cols]) * (MEM_HEAD_DIM ** -0.5)
            e = jnp.exp(s - jnp.max(s, axis=-1, keepdims=True))
            p = e / jnp.sum(e, axis=-1, keepdims=True)
            heads.append(_dot(p.astype(BF16), memv_ref[bi, :, cols]))
        mem_rows.append(jnp.concatenate(heads, axis=1))
    mem = (mem_rows[0] if n_batches == 1 else jnp.concatenate(mem_rows, axis=0)).astype(BF16)

    gate = _sigmoid(_dot(xb, wgl_ref[...]))
    mixed = (gate[:, :D_MODEL] * _dot(attn_ref[...], wao_ref[...])
             + gate[:, D_MODEL:2 * D_MODEL] * _dot(conv_in, wco_ref[...])
             + gate[:, 2 * D_MODEL:] * _dot(mem, wmo_ref[...]))
    y = ALPHA * xn + _dot(mixed.astype(BF16), wo_ref[...])
    x1_ref[...] = _layer_norm(y, ln1g_ref[...], ln1b_ref[...])


def _mixer(x, attn, mem_k, mem_v, prev_a, prev_b, ln_g, ln_b, conv_w8, ln1_g, ln1_b, w, *, seq, rows):
    n = x.shape[0]
    tm = min(rows, n)
    multi_seq = seq < tm
    n_batches = max(1, tm // seq)
    row = lambda width: pl.BlockSpec((tm, width), lambda i: (i, 0))
    mem = pl.BlockSpec((n_batches, N_MEM, MEM_DIM), lambda i: ((i * tm) // (seq * n_batches), 0, 0))
    vec = _resident((1, D_MODEL))
    if multi_seq:
        prev_spec = row(D_CONV)
        state_spec = row(D_CONV)
        state_shape = jax.ShapeDtypeStruct((n, D_CONV), F32)
    else:
        prev_spec = pl.BlockSpec((1, SUBLANES, D_CONV), lambda i: ((i * tm) // seq, 0, 0))
        state_spec = prev_spec
        state_shape = jax.ShapeDtypeStruct((n // seq, SUBLANES, D_CONV), F32)
    weights = [w[k] for k in ("gb", "gc", "ci", "mq", "gl", "ao", "co", "mo", "o")]
    kern = functools.partial(_mixer_kernel, seq=seq, rows=tm)
    return pl.pallas_call(
        kern,
        grid=(n // tm,),
        in_specs=[row(D_MODEL), row(ATTN_DIM), mem, mem, prev_spec, prev_spec,
                  vec, vec, _resident((SUBLANES, D_CONV)), vec, vec] + [_resident(m.shape) for m in weights],
        out_specs=(row(D_MODEL), state_spec),
        out_shape=(jax.ShapeDtypeStruct((n, D_MODEL), F32), state_shape),
        scratch_shapes=[pltpu.VMEM((SUBLANES, D_CONV), F32)],
        compiler_params=_params("arbitrary"),
        name="mixer",
    )(x, attn, mem_k, mem_v, prev_a, prev_b, ln_g, ln_b, conv_w8, ln1_g, ln1_b, *weights)


def _route(x, wr_hi, wr_lo, bias):
    tm = x.shape[0]
    x_hi = x.astype(BF16)
    x_lo = (x - x_hi.astype(F32)).astype(BF16)
    logits = _dot_nt(wr_hi, x_hi) + (_dot_nt(wr_hi, x_lo) + _dot_nt(wr_lo, x_hi))
    s = _sigmoid(logits)
    sc = s + bias[:, 0:1]
    sub = lax.broadcasted_iota(I32, (GROUP_SIZE, tm), 0)
    neg = -jnp.inf
    groups, gscore = [], []
    for g in range(N_GROUPS):
        xg = sc[g * GROUP_SIZE:(g + 1) * GROUP_SIZE, :]
        m1 = jnp.max(xg, axis=0, keepdims=True)
        i1 = jnp.min(jnp.where(xg == m1, sub, GROUP_SIZE), axis=0, keepdims=True)
        m2 = jnp.max(jnp.where(sub == i1, neg, xg), axis=0, keepdims=True)
        groups.append(xg)
        gscore.append(m1 + m2)
    masked = []
    for g in range(N_GROUPS):
        rank = jnp.zeros((1, tm), I32)
        for o in range(N_GROUPS):
            if o == g:
                continue
            beats = (gscore[o] > gscore[g]) | ((gscore[o] == gscore[g]) & (o < g))
            rank = rank + beats.astype(I32)
        masked.append(jnp.where(rank < TOPK_GROUPS, groups[g], neg))
    v = jnp.concatenate(masked, axis=0)
    eidx = lax.broadcasted_iota(I32, (N_EXPERTS, tm), 0)
    chosen = jnp.zeros((N_EXPERTS, tm), jnp.bool_)
    for _ in range(TOP_K):
        m = jnp.max(v, axis=0, keepdims=True)
        first = jnp.min(jnp.where(v == m, eidx, N_EXPERTS), axis=0, keepdims=True)
        hit = eidx == first
        chosen = chosen | hit
        v = jnp.where(hit, neg, v)
    wsel = jnp.where(chosen, s, 0.0)
    return wsel / jnp.sum(wsel, axis=0, keepdims=True) * ROUTED_SCALE


def _expert_ffn(xb, w_gu, w_down, gate_col=None):
    gu = _dot(xb, w_gu)
    g, u = gu[:, :D_EXPERT], gu[:, D_EXPERT:]
    h = g * _sigmoid(g) * u
    if gate_col is not None:
        h = h * gate_col
    return _dot(h.astype(BF16), w_down)


def _moe_kernel(x_ref, wrh_ref, wrl_ref, br_ref, wgu_ref, wd_ref, wsgu_ref, wsd_ref, g_ref, b_ref,
                o_ref, xb_ref, gate_ref, acc_ref, *, experts_per_step):
    j = pl.program_id(1)

    @pl.when(j == 0)
    def _():
        x = x_ref[...]
        gate_t = _route(x, wrh_ref[...], wrl_ref[...], br_ref[...])
        pad = jnp.zeros((LANES - N_EXPERTS, x.shape[0]), F32)
        gate_ref[...] = jnp.concatenate([gate_t, pad], axis=0).T
        xb = x.astype(BF16)
        xb_ref[...] = xb
        acc_ref[...] = _expert_ffn(xb, wsgu_ref[...], wsd_ref[...])

    xb = xb_ref[...]
    gate = gate_ref[...]
    lane = lax.broadcasted_iota(I32, gate.shape, 1)
    acc = acc_ref[...]
    for e in range(experts_per_step):
        col = jnp.sum(jnp.where(lane == j * experts_per_step + e, gate, 0.0), axis=1, keepdims=True)
        acc = acc + _expert_ffn(xb, wgu_ref[e], wd_ref[e], col)
    acc_ref[...] = acc

    @pl.when(j == pl.num_programs(1) - 1)
    def _():
        o_ref[...] = _layer_norm(ALPHA * x_ref[...] + acc_ref[...], g_ref[...], b_ref[...])


def _moe(x, w, ln_g, ln_b, rows):
    n = x.shape[0]
    tm = min(rows, n)
    eps = MOE_EXPERTS_PER_STEP
    row = pl.BlockSpec((tm, D_MODEL), lambda i, j: (i, 0))
    kern = functools.partial(_moe_kernel, experts_per_step=eps)
    return pl.pallas_call(
        kern,
        grid=(n // tm, N_EXPERTS // eps),
        in_specs=[row, _resident(w["r_hi"].shape), _resident(w["r_lo"].shape), _resident(w["r_b"].shape),
                  pl.BlockSpec((eps, D_MODEL, 2 * D_EXPERT), lambda i, j: (j, 0, 0)),
                  pl.BlockSpec((eps, D_EXPERT, D_MODEL), lambda i, j: (j, 0, 0)),
                  _resident(w["s_gu"].shape), _resident(w["s_d"].shape),
                  _resident((1, D_MODEL)), _resident((1, D_MODEL))],
        out_specs=row,
        out_shape=jax.ShapeDtypeStruct((n, D_MODEL), F32),
        scratch_shapes=[pltpu.VMEM((tm, D_MODEL), BF16), pltpu.VMEM((tm, LANES), F32),
                        pltpu.VMEM((tm, D_MODEL), F32)],
        compiler_params=_params("arbitrary", "arbitrary"),
        name="moe",
    )(x, w["r_hi"], w["r_lo"], w["r_b"], w["e_gu"], w["e_d"], w["s_gu"], w["s_d"], ln_g, ln_b)


def _pad_rows(a, rows):
    return a if a.shape[1] == rows else jnp.pad(a, ((0, 0), (0, rows - a.shape[1]), (0, 0)))


def _head_major_per_batch(x, b, seq, tq_pad):
    heads = x.shape[1]
    if seq % Q_TILE == 0:
        return x.reshape(b, seq // Q_TILE, heads * Q_TILE, LANES)
    per_tile = Q_TILE // seq
    x = x.reshape(-1, heads, per_tile, seq, LANES).transpose(0, 2, 1, 3, 4).reshape(b, heads, seq, LANES)
    x = jnp.pad(x, ((0, 0), (0, 0), (0, tq_pad - seq), (0, 0)))
    return x.reshape(b, 1, heads * tq_pad, LANES)


def _layer(x, mem_k_bf, mem_v_bf, past_k, past_v, past_ki, conv_prev, w, ln, layer):
    b, seq, _ = x.shape
    past = past_k.shape[1]
    n = b * seq
    n_keys = past + seq
    x2 = x.reshape(n, D_MODEL)

    pos = past + jnp.arange(seq, dtype=I32)
    reps = max(1, min(PROJ_ROWS, n) // seq)
    pos_t = jnp.tile(pos, reps)
    tabs = (_rope_tables(pos_t, HEAD_DIM // ROT_FRACTION // 2)
            + _rope_tables(pos_t, IDX_DIM // ROT_FRACTION // 2,
                           (IDX_DIM, IDX_DIM + IDX_HEADS), IDX_HEADS ** -0.5))
    q, qi, k, v, kiwi, k_bf, ki_bf = _qkv_project(x2, ln["in_g"], ln["in_b"], w, tabs, PROJ_ROWS, seq)

    l_pad = -(-n_keys // KEY_TILE) * KEY_TILE
    tq_pad = -(-seq // Q_TILE) * Q_TILE
    k3 = jnp.concatenate([past_k.reshape(b, past, KV_DIM).astype(BF16), k_bf.reshape(b, seq, KV_DIM)], axis=1)
    v3 = jnp.concatenate([past_v.reshape(b, past, KV_DIM), v.reshape(b, seq, KV_DIM)], axis=1)
    ki_past = jnp.pad(past_ki, ((0, 0), (0, 0), (0, LANES - IDX_DIM))).astype(BF16)
    ki3 = jnp.concatenate([ki_past, ki_bf.reshape(b, seq, LANES)], axis=1)
    attn = _dsa_attention(
        _head_major_per_batch(q, b, seq, tq_pad), _head_major_per_batch(qi, b, seq, tq_pad),
        _pad_rows(kiwi.reshape(b, seq, LANES), tq_pad),
        _pad_rows(k3, l_pad), _pad_rows(v3, l_pad), _pad_rows(ki3, l_pad),
        past=past, seq=seq, n_keys=n_keys)
    attn = attn[:, :seq].reshape(n, ATTN_DIM)

    tm = min(PROJ_ROWS, n)
    if seq < tm:
        tpos = jnp.arange(seq)[None, :, None]
        prev_a = jnp.where(tpos == 0, conv_prev[:, 1:2], 0.0).reshape(n, D_CONV)
        prev_b = jnp.where(tpos == 0, conv_prev[:, 0:1], jnp.where(tpos == 1, conv_prev[:, 1:2], 0.0)).reshape(n, D_CONV)
    else:
        prev_a = jnp.pad(conv_prev, ((0, 0), (SUBLANES - (CONV_WIDTH - 1), 0), (0, 0)))
        prev_b = prev_a
    x1, state = _mixer(x2, attn, mem_k_bf, mem_v_bf, prev_a, prev_b, ln["in_g"], ln["in_b"], w["conv"],
                       ln["g1"], ln["b1"], w, seq=seq, rows=PROJ_ROWS)
    if seq < tm:
        new_conv = state.reshape(b, seq, D_CONV)[:, seq - (CONV_WIDTH - 1):]
    else:
        new_conv = state[:, SUBLANES - (CONV_WIDTH - 1):]

    y = _moe(x1, w, ln["g2"], ln["b2"], MOE_ROWS)
    return (y.reshape(b, seq, D_MODEL), k.reshape(b, seq, N_KV_HEADS, HEAD_DIM),
            v.reshape(b, seq, N_KV_HEADS, HEAD_DIM), kiwi[:, :IDX_DIM].reshape(b, seq, IDX_DIM), new_conv)


def _prep_weights(l, w_in, conv_w, w_attn_out, w_conv_out, w_mem_out, w_o, w_router, b_router,
                  w_exp_gu, w_exp_down, w_sh_gu, w_sh_down):
    sizes = [ATTN_DIM, KV_DIM, KV_DIM, IDX_HEADS * IDX_DIM, IDX_DIM, IDX_HEADS,
             D_CONV, D_CONV, D_CONV, MEM_DIM, N_BRANCH * D_MODEL]
    offs = np.concatenate([[0], np.cumsum(sizes)]).tolist()
    wi = w_in[l]
    part = lambda j: wi[:, offs[j]:offs[j + 1]]
    w_qi = part(3).reshape(D_MODEL, IDX_HEADS, IDX_DIM)
    w_qi = jnp.pad(w_qi, ((0, 0), (0, 0), (0, LANES - IDX_DIM))).reshape(D_MODEL, IDX_HEADS * LANES)
    w_kw = jnp.pad(jnp.concatenate([part(4), part(5)], axis=1), ((0, 0), (0, LANES - IDX_DIM - IDX_HEADS)))
    wr = w_router[l].T
    wr_hi = wr.astype(BF16)
    return {
        "q": part(0).astype(BF16), "kv": jnp.concatenate([part(1), part(2)], axis=1).astype(BF16),
        "qi": w_qi.astype(BF16), "kw": w_kw.astype(BF16),
        "gb": part(6).astype(BF16), "gc": part(7).astype(BF16), "ci": part(8).astype(BF16),
        "mq": part(9).astype(BF16), "gl": part(10).astype(BF16),
        "ao": w_attn_out[l].astype(BF16), "co": w_conv_out[l].astype(BF16), "mo": w_mem_out[l].astype(BF16),
        "o": w_o[l].astype(BF16),
        "conv": jnp.pad(conv_w[l], ((0, SUBLANES - CONV_WIDTH), (0, 0))),
        "r_hi": wr_hi, "r_lo": (wr - wr_hi.astype(F32)).astype(BF16),
        "r_b": jnp.broadcast_to(b_router[l][:, None], (N_EXPERTS, LANES)),
        "e_gu": w_exp_gu[l].astype(BF16), "e_d": w_exp_down[l].astype(BF16),
        "s_gu": w_sh_gu[l].astype(BF16), "s_d": w_sh_down[l].astype(BF16),
    }


def kernel(x_prompt, x_sample, mem_prompt, cache_attn_k, cache_attn_v, cache_idx_k, cache_mem_k, cache_mem_v,
           state_conv, ln_in_g, ln_in_b, w_in, conv_w, w_mem_kv, w_attn_out, w_conv_out, w_mem_out, w_o,
           ln1_g, ln1_b, w_router, b_router, w_exp_gu, w_exp_down, w_sh_gu, w_sh_down, ln2_g, ln2_b):
    assert DEPTH == 1 and w_in.shape[0] == 1
    l = 0
    bp, n_mem = mem_prompt.shape[0], mem_prompt.shape[1]
    w = _prep_weights(l, w_in, conv_w, w_attn_out, w_conv_out, w_mem_out, w_o, w_router, b_router,
                      w_exp_gu, w_exp_down, w_sh_gu, w_sh_down)
    ln = {"in_g": ln_in_g[None, :], "in_b": ln_in_b[None, :], "g1": ln1_g[l][None, :], "b1": ln1_b[l][None, :],
          "g2": ln2_g[l][None, :], "b2": ln2_b[l][None, :]}

    mem_kv = _project(mem_prompt.reshape(bp * n_mem, D_MODEL), w_mem_kv[l].astype(BF16))
    mem_k = mem_kv[:, :MEM_DIM].reshape(bp, n_mem, MEM_DIM)
    mem_v = mem_kv[:, MEM_DIM:].reshape(bp, n_mem, MEM_DIM)

    zeros = lambda *s: jnp.zeros(s, F32)
    yp, kp, vp, kip, cp = _layer(
        x_prompt, mem_k.astype(BF16), mem_v.astype(BF16),
        zeros(bp, 0, N_KV_HEADS, HEAD_DIM), zeros(bp, 0, N_KV_HEADS, HEAD_DIM), zeros(bp, 0, IDX_DIM),
        zeros(bp, CONV_WIDTH - 1, D_CONV), w, ln, l)
    bs = x_sample.shape[0]
    ys, ks, vs, kis, cs = _layer(
        x_sample, cache_mem_k[l].reshape(bs, n_mem, MEM_DIM).astype(BF16),
        cache_mem_v[l].reshape(bs, n_mem, MEM_DIM).astype(BF16),
        cache_attn_k[l], cache_attn_v[l], cache_idx_k[l], state_conv[l], w, ln, l)

    mem_shape = (1, bp, n_mem, MEM_HEADS, MEM_HEAD_DIM)
    return (yp, ys, kp[None], vp[None], kip[None], mem_k.reshape(mem_shape), mem_v.reshape(mem_shape), cp[None],
            ks[None], vs[None], kis[None], cs[None])
```

```python
import functools

import jax
import jax.numpy as jnp
import numpy as np
from jax import lax
from jax.experimental import pallas as pl
from jax.experimental.pallas import tpu as pltpu

D_MODEL = 1024
DEPTH = 1
CHUNK = 64
HEAD_DIM = 128
N_HEADS = D_MODEL // HEAD_DIM
N_KV_HEADS = max(1, N_HEADS // 4)
KV_GROUP = N_HEADS // N_KV_HEADS
ATTN_DIM = N_HEADS * HEAD_DIM
KV_DIM = N_KV_HEADS * HEAD_DIM
ROT_FRACTION = 4
ROPE_THETA = 500000.0
IDX_HEADS = 8
IDX_DIM = 64
TOPK_MAX = 256
D_CONV = D_MODEL
CONV_WIDTH = 3
N_MEM = 256
MEM_HEADS = 4
MEM_HEAD_DIM = D_MODEL // MEM_HEADS
MEM_DIM = MEM_HEADS * MEM_HEAD_DIM
N_BRANCH = 3
N_EXPERTS = 64
TOP_K = 8
N_GROUPS = 8
GROUP_SIZE = N_EXPERTS // N_GROUPS
TOPK_GROUPS = 4
D_EXPERT = D_MODEL // 4
ROUTED_SCALE = 2.5
LN_EPS = 1e-5
ALPHA = (2.0 * DEPTH) ** 0.25

LANES = 128
SUBLANES = 8
PACKED_ROWS = 16
VMEM_LIMIT_BYTES = 56 * 1024 * 1024

Q_TILE = 128
KEY_TILE = 512
HEADS_PER_UNIT = 2
DENOM_ROWS = 16
PROJ_ROWS = 256
MOE_ROWS = 1024
MOE_EXPERTS_PER_STEP = 4

F32 = jnp.float32
BF16 = jnp.bfloat16
I32 = jnp.int32
I16 = jnp.int16
INT_MIN = -(2 ** 31)
INT_MAX = 2 ** 31 - 1
MASKED_LOGIT = -1e30
QK_SCALE_LOG2E = HEAD_DIM ** -0.5 * float(np.log2(np.e))

_NT = (((1,), (1,)), ((), ()))


def _dot(a, b):
    return jnp.dot(a, b, preferred_element_type=F32)


def _dot_nt(a, b):
    return lax.dot_general(a, b, _NT, preferred_element_type=F32)


def _layer_norm(x, g, b):
    mu = jnp.mean(x, axis=-1, keepdims=True)
    xc = x - mu
    var = jnp.mean(xc * xc, axis=-1, keepdims=True)
    return xc * lax.rsqrt(var + LN_EPS) * g + b


def _sigmoid(x):
    return 1.0 / (1.0 + jnp.exp(-x))


def _resident(shape):
    nd = len(shape)
    return pl.BlockSpec(shape, lambda *_: (0,) * nd, pipeline_mode=pl.Buffered(1))


def _params(*sem):
    return pltpu.CompilerParams(dimension_semantics=sem, vmem_limit_bytes=VMEM_LIMIT_BYTES)


def _rope_tables(pos, half, extra_scale_lanes=None, extra_scale=1.0):
    inv_freq = ROPE_THETA ** (-jnp.arange(half, dtype=F32) / half)
    ang = pos.astype(F32)[:, None] * inv_freq[None, :]
    cos, sin = jnp.cos(ang), jnp.sin(ang)
    n = pos.shape[0]
    c = jnp.concatenate([cos, cos, jnp.ones((n, LANES - 2 * half), F32)], axis=1)
    s = jnp.concatenate([-sin, sin, jnp.zeros((n, LANES - 2 * half), F32)], axis=1)
    if extra_scale_lanes is not None:
        lo, hi = extra_scale_lanes
        lane = jnp.arange(LANES)[None, :]
        c = jnp.where((lane >= lo) & (lane < hi), jnp.float32(extra_scale), c)
    return c, s


def _rope(x, c, s, half):
    rows, width = x.shape
    lane = lax.broadcasted_iota(I32, (rows, LANES), 1)
    first = lane < half
    outs = []
    for j in range(width // LANES):
        xj = x[:, j * LANES:(j + 1) * LANES]
        swap = jnp.where(first, pltpu.roll(xj, LANES - half, 1), pltpu.roll(xj, half, 1))
        outs.append(xj * c + swap * s)
    return outs[0] if len(outs) == 1 else jnp.concatenate(outs, axis=1)


def _proj_kernel(x_ref, w_ref, o_ref):
    o_ref[...] = _dot(x_ref[...].astype(BF16), w_ref[...])


def _project(x, w_bf, col_tile=512):
    n, d = x.shape
    m = w_bf.shape[1]
    return pl.pallas_call(
        _proj_kernel,
        grid=(m // col_tile,),
        in_specs=[pl.BlockSpec((n, d), lambda j: (0, 0)),
                  pl.BlockSpec((d, col_tile), lambda j: (0, j))],
        out_specs=pl.BlockSpec((n, col_tile), lambda j: (0, j)),
        out_shape=jax.ShapeDtypeStruct((n, m), F32),
        compiler_params=_params("arbitrary"),
        name="mem_kv_proj",
    )(x, w_bf)


def _store_head_major(ref, x):
    for r in range(ref.shape[0]):
        for h in range(ref.shape[1]):
            ref[r, h] = x[r * Q_TILE:(r + 1) * Q_TILE, h * LANES:(h + 1) * LANES]


def _qkv_kernel(x_ref, g_ref, b_ref, wq_ref, wqi_ref, wkv_ref, wkw_ref, ch_ref, sh_ref, ci_ref, si_ref,
                q_ref, qi_ref, k_ref, v_ref, kiwi_ref, kbf_ref, kibf_ref):
    xb = _layer_norm(x_ref[...], g_ref[...], b_ref[...]).astype(BF16)
    ch, sh, ci, si = ch_ref[...], sh_ref[...], ci_ref[...], si_ref[...]
    q = _rope(_dot(xb, wq_ref[...]), ch, sh, HEAD_DIM // ROT_FRACTION // 2)
    _store_head_major(q_ref, (q * QK_SCALE_LOG2E).astype(BF16))
    qi = _rope(_dot(xb, wqi_ref[...]), ci, si, IDX_DIM // ROT_FRACTION // 2)
    _store_head_major(qi_ref, qi.astype(BF16))
    kv = _dot(xb, wkv_ref[...])
    k = _rope(kv[:, :KV_DIM], ch, sh, HEAD_DIM // ROT_FRACTION // 2)
    k_ref[...] = k
    kbf_ref[...] = k.astype(BF16)
    v_ref[...] = kv[:, KV_DIM:]
    kw = _rope(_dot(xb, wkw_ref[...]), ci, si, IDX_DIM // ROT_FRACTION // 2)
    kiwi_ref[...] = kw
    kibf_ref[...] = kw.astype(BF16)


def _qkv_project(x, ln_g, ln_b, w, tabs, rows, seq):
    n = x.shape[0]
    tm = min(rows, n)
    tab_blocks = max(seq, tm) // tm
    row = lambda width: pl.BlockSpec((tm, width), lambda i: (i, 0))
    tab = pl.BlockSpec((tm, LANES), lambda i: (i % tab_blocks, 0))
    hm_shape = lambda heads: (n // Q_TILE, heads, Q_TILE, LANES)
    hm = lambda heads: pl.BlockSpec((tm // Q_TILE, heads, Q_TILE, LANES), lambda i: (i, 0, 0, 0))
    out_shapes = (
        jax.ShapeDtypeStruct(hm_shape(N_HEADS), BF16),
        jax.ShapeDtypeStruct(hm_shape(IDX_HEADS), BF16),
        jax.ShapeDtypeStruct((n, KV_DIM), F32),
        jax.ShapeDtypeStruct((n, KV_DIM), F32),
        jax.ShapeDtypeStruct((n, LANES), F32),
        jax.ShapeDtypeStruct((n, KV_DIM), BF16),
        jax.ShapeDtypeStruct((n, LANES), BF16),
    )
    return pl.pallas_call(
        _qkv_kernel,
        grid=(n // tm,),
        in_specs=[row(D_MODEL), _resident((1, D_MODEL)), _resident((1, D_MODEL)),
                  _resident(w["q"].shape), _resident(w["qi"].shape), _resident(w["kv"].shape),
                  _resident(w["kw"].shape), tab, tab, tab, tab],
        out_specs=(hm(N_HEADS), hm(IDX_HEADS), row(KV_DIM), row(KV_DIM), row(LANES),
                   row(KV_DIM), row(LANES)),
        out_shape=out_shapes,
        compiler_params=_params("arbitrary"),
        name="qkv_proj",
    )(x, ln_g, ln_b, w["q"], w["qi"], w["kv"], w["kw"], *tabs)


def _sortable_key(score):
    bits = lax.bitcast_convert_type(score + 0.0, I32)
    return bits ^ ((bits >> 31) & INT_MAX)


def _dsa_kernel(q_ref, qi_ref, kiwi_ref, k_ref, v_ref, ki_ref, o_ref,
                vt_ref, key_ref, khi_ref, klo_ref, acc_ref, m_ref, j_ref,
                *, past, seq, n_keys, topk, n_key_tiles):
    i = pl.program_id(1)
    tq, kt = Q_TILE, KEY_TILE
    uq = HEADS_PER_UNIT * tq

    @pl.when(i == 0)
    def _():
        ones = jnp.ones((DENOM_ROWS, kt), BF16)
        for c in range(n_key_tiles):
            for n in range(N_KV_HEADS):
                vt_ref[c, n, :HEAD_DIM, :] = (
                    v_ref[0, c * kt:(c + 1) * kt, n * HEAD_DIM:(n + 1) * HEAD_DIM].T.astype(BF16))
                vt_ref[c, n, HEAD_DIM:, :] = ones

    lane = lax.broadcasted_iota(I32, (1, tq), 1)
    t = i * tq + lane
    q_real = t < seq
    lim = jnp.minimum(((past + t) // CHUNK + 1) * CHUNK, n_keys)
    lim_max = jnp.minimum(((past + i * tq + tq - 1) // CHUNK + 1) * CHUNK, n_keys)
    n_tiles = (lim_max + kt - 1) // kt
    row = lax.broadcasted_iota(I32, (kt, tq), 0)

    wi = kiwi_ref[0].T[IDX_DIM:IDX_DIM + IDX_HEADS, :] * (IDX_DIM ** -0.5)

    def score_tile(j, carry):
        kib = ki_ref[0, pl.ds(pl.multiple_of(j * kt, kt), kt), :]
        rel_of = lambda u: _dot_nt(kib, qi_ref[0, 0, u * uq:(u + 1) * uq, :])
        n_units = IDX_HEADS // HEADS_PER_UNIT
        acc = jnp.zeros((kt, tq), F32)
        rel_next = rel_of(0)
        for u in range(n_units):
            rel, rel_next = rel_next, (rel_of(u + 1) if u + 1 < n_units else None)
            for g in range(HEADS_PER_UNIT):
                h = u * HEADS_PER_UNIT + g
                acc = acc + jnp.maximum(rel[:, g * tq:(g + 1) * tq], 0.0) * wi[h:h + 1, :]
        score = jnp.where(j * kt + row < lim, acc, -jnp.inf)
        key = _sortable_key(score)
        key_ref[j] = key
        khi_ref[j] = (key >> 16).astype(I16)
        klo_ref[j] = ((key & 0xFFFF) - 0x8000).astype(I16)
        return carry

    lax.fori_loop(0, n_tiles, score_tile, 0)

    def count16(ref, cand):
        c16 = cand.astype(I16)

        def body(j, acc):
            hit = jnp.where(ref[j] >= c16, jnp.int16(1), jnp.int16(0))
            parts = [hit[r * PACKED_ROWS:(r + 1) * PACKED_ROWS, :] for r in range(kt // PACKED_ROWS)]
            while len(parts) > 1:
                parts = [a + b for a, b in zip(parts[0::2], parts[1::2])]
            return acc + parts[0]
        acc = lax.fori_loop(0, n_tiles, body, jnp.zeros((PACKED_ROWS, tq), I16))
        return jnp.sum(acc.astype(I32), axis=0, keepdims=True)

    def bisect16(ref, n_bits, bias, start, start_cnt, want):
        def step(b, state):
            t, cnt = state
            cand = t | jnp.left_shift(jnp.int32(1), n_bits - 1 - b)
            c = count16(ref, cand - bias)
            take = c >= want
            return jnp.where(take, cand, t), jnp.where(take, c, cnt)
        return lax.fori_loop(0, n_bits, step, (start, start_cnt))

    c0 = count16(khi_ref, jnp.zeros((1, tq), I32))
    nonneg = c0 >= topk
    t_hi, c_hi = bisect16(khi_ref, 15, 0, jnp.where(nonneg, 0, -0x8000).astype(I32),
                          jnp.where(nonneg, c0, n_tiles * kt).astype(I32), topk)
    n_above_hi = jnp.where(t_hi == 0x7FFF, 0, count16(khi_ref, t_hi + 1))
    t_hi16 = t_hi.astype(I16)

    def keep_bucket(j, carry):
        klo_ref[j] = jnp.where(khi_ref[j] == t_hi16, klo_ref[j], jnp.int16(-0x8000))
        return carry

    lax.fori_loop(0, n_tiles, keep_bucket, 0)
    t_lo, c_lo = bisect16(klo_ref, 16, 0x8000, jnp.zeros((1, tq), I32), c_hi - n_above_hi, topk - n_above_hi)
    thr = jnp.left_shift(t_hi, 16) | t_lo
    tied = jnp.max(jnp.where(q_real, n_above_hi + c_lo, topk)) > topk

    j_ref[...] = jnp.full((1, tq), INT_MAX, I32)

    @pl.when(tied)
    def _():
        def count(pred):
            def body(j, acc):
                hit = pred(key_ref[j], j).astype(I32)
                return acc + jnp.sum(hit.reshape(kt // SUBLANES, SUBLANES, tq), axis=0)
            acc = lax.fori_loop(0, n_tiles, body, jnp.zeros((SUBLANES, tq), I32))
            return jnp.sum(acc, axis=0, keepdims=True)

        n_ties_wanted = topk - count(lambda blk, j: blk > thr)
        n_bits = int(n_key_tiles * kt).bit_length()

        def bisect_idx(it, cut):
            cand = cut | jnp.left_shift(jnp.int32(1), n_bits - 1 - it)
            c = count(lambda blk, j: (blk == thr) & (j * kt + row < cand))
            return jnp.where(c <= n_ties_wanted, cand, cut)

        j_ref[...] = lax.fori_loop(0, n_bits, bisect_idx, jnp.zeros((1, tq), I32))

    j_cut = j_ref[...]

    m_ref[...] = jnp.full(m_ref.shape, MASKED_LOGIT, F32)
    acc_ref[...] = jnp.zeros(acc_ref.shape, F32)

    def attend_tile(j, carry):
        keys = key_ref[j]
        kidx = j * kt + row
        sel = ((keys > thr) | ((keys == thr) & (kidx < j_cut))) & (kidx < lim)
        start = pl.multiple_of(j * kt, kt)
        n_units = N_HEADS // HEADS_PER_UNIT
        kv_of = lambda u: u * HEADS_PER_UNIT // KV_GROUP

        def logits(u):
            kb = k_ref[0, pl.ds(start, kt), kv_of(u) * HEAD_DIM:(kv_of(u) + 1) * HEAD_DIM]
            return _dot_nt(kb, q_ref[0, 0, u * uq:(u + 1) * uq, :])

        s_next = logits(0)
        for u in range(n_units):
            s, s_next = s_next, (logits(u + 1) if u + 1 < n_units else None)
            s = jnp.concatenate([jnp.where(sel, s[:, g * tq:(g + 1) * tq], MASKED_LOGIT)
                                 for g in range(HEADS_PER_UNIT)], axis=1)
            m_old = m_ref[u]
            m_new = jnp.maximum(m_old, jnp.max(s, axis=0, keepdims=True))
            p = jnp.exp2(s - m_new).astype(BF16)
            acc_ref[u] = jnp.exp2(m_old - m_new) * acc_ref[u] + _dot(vt_ref[j, kv_of(u)], p)
            m_ref[u] = m_new
        return carry

    lax.fori_loop(0, n_tiles, attend_tile, 0)

    for h in range(N_HEADS):
        u, cols = h // HEADS_PER_UNIT, slice((h % HEADS_PER_UNIT) * tq, (h % HEADS_PER_UNIT + 1) * tq)
        o = acc_ref[u, :HEAD_DIM, cols] / acc_ref[u, HEAD_DIM:HEAD_DIM + 1, cols]
        o_ref[0, :, h * HEAD_DIM:(h + 1) * HEAD_DIM] = o.T.astype(BF16)


def _dsa_attention(q, qi, kiwi, k_bf, v, ki_bf, *, past, seq, n_keys):
    b, tq_pad, _ = kiwi.shape
    l_pad = k_bf.shape[1]
    n_key_tiles = l_pad // KEY_TILE
    topk = min(TOPK_MAX, n_keys // 4)
    qblk = lambda width: pl.BlockSpec((1, Q_TILE, width), lambda bi, i: (bi, i, 0))
    hmblk = lambda heads: pl.BlockSpec((1, 1, heads * Q_TILE, LANES), lambda bi, i: (bi, i, 0, 0))
    kblk = lambda width: pl.BlockSpec((1, l_pad, width), lambda bi, i: (bi, 0, 0),
                                      pipeline_mode=pl.Buffered(1))
    kern = functools.partial(_dsa_kernel, past=past, seq=seq, n_keys=n_keys, topk=topk,
                             n_key_tiles=n_key_tiles)
    return pl.pallas_call(
        kern,
        grid=(b, tq_pad // Q_TILE),
        in_specs=[hmblk(N_HEADS), hmblk(IDX_HEADS), qblk(LANES),
                  kblk(KV_DIM), kblk(KV_DIM), kblk(LANES)],
        out_specs=qblk(ATTN_DIM),
        out_shape=jax.ShapeDtypeStruct((b, tq_pad, ATTN_DIM), BF16),
        scratch_shapes=[
            pltpu.VMEM((n_key_tiles, N_KV_HEADS, HEAD_DIM + DENOM_ROWS, KEY_TILE), BF16),
            pltpu.VMEM((n_key_tiles, KEY_TILE, Q_TILE), I32),
            pltpu.VMEM((n_key_tiles, KEY_TILE, Q_TILE), I16),
            pltpu.VMEM((n_key_tiles, KEY_TILE, Q_TILE), I16),
            pltpu.VMEM((N_HEADS // HEADS_PER_UNIT, HEAD_DIM + DENOM_ROWS, HEADS_PER_UNIT * Q_TILE), F32),
            pltpu.VMEM((N_HEADS // HEADS_PER_UNIT, 1, HEADS_PER_UNIT * Q_TILE), F32),
            pltpu.VMEM((1, Q_TILE), I32),
        ],
        compiler_params=_params("arbitrary", "arbitrary"),
        name="dsa_attention",
    )(q, qi, kiwi, k_bf, v, ki_bf)


def _mixer_kernel(x_ref, attn_ref, memk_ref, memv_ref, prev_a_ref, prev_b_ref,
                  lng_ref, lnb_ref, cw_ref, ln1g_ref, ln1b_ref,
                  wgb_ref, wgc_ref, wci_ref, wmq_ref, wgl_ref, wao_ref, wco_ref, wmo_ref, wo_ref,
                  x1_ref, state_ref, carry_ref, *, seq, rows):
    i = pl.program_id(0)
    multi_seq = seq < rows
    xn = _layer_norm(x_ref[...], lng_ref[...], lnb_ref[...])
    xb = xn.astype(BF16)

    u = _dot(xb, wgc_ref[...]) * _dot(xb, wci_ref[...])
    ridx = lax.broadcasted_iota(I32, u.shape, 0)
    if multi_seq:
        tpos = ridx & (seq - 1)
        s1 = jnp.where(tpos == 0, prev_a_ref[...], pltpu.roll(u, 1, 0))
        s2 = jnp.where(tpos < 2, prev_b_ref[...], pltpu.roll(u, 2, 0))
        state_ref[...] = u
    else:
        blocks_per_seq = seq // rows
        @pl.when(i % blocks_per_seq == 0)
        def _():
            carry_ref[...] = prev_a_ref[0]

        prev = carry_ref[...]
        p1, p2 = prev[SUBLANES - 1:SUBLANES, :], prev[SUBLANES - 2:SUBLANES - 1, :]
        s1 = jnp.where(ridx == 0, p1, pltpu.roll(u, 1, 0))
        s2 = jnp.where(ridx == 0, p2, jnp.where(ridx == 1, p1, pltpu.roll(u, 2, 0)))
        carry_ref[...] = u[rows - SUBLANES:, :]
        state_ref[0] = u[rows - SUBLANES:, :]
    cw = cw_ref[...]
    conv = cw[0:1, :] * s2 + cw[1:2, :] * s1 + cw[2:3, :] * u
    conv_in = (_dot(xb, wgb_ref[...]) * conv).astype(BF16)

    mq = _dot(xb, wmq_ref[...]).astype(BF16)
    n_batches = max(1, rows // seq)
    rb = rows // n_batches
    mem_rows = []
    for bi in range(n_batches):
        heads = []
        for h in range(MEM_HEADS):
            cols = slice(h * MEM_HEAD_DIM, (h + 1) * MEM_HEAD_DIM)
            s = _dot_nt(mq[bi * rb:(bi + 1) * rb, cols], memk_ref[bi, :, cols]) * (MEM_HEAD_DIM ** -0.5)
            e = jnp.exp(s - jnp.max(s, axis=-1, keepdims=True))
            p = e / jnp.sum(e, axis=-1, keepdims=True)
            heads.append(_dot(p.astype(BF16), memv_ref[bi, :, cols]))
        mem_rows.append(jnp.concatenate(heads, axis=1))
    mem = (mem_rows[0] if n_batches == 1 else jnp.concatenate(mem_rows, axis=0)).astype(BF16)

    gate = _sigmoid(_dot(xb, wgl_ref[...]))
    mixed = (gate[:, :D_MODEL] * _dot(attn_ref[...], wao_ref[...])
             + gate[:, D_MODEL:2 * D_MODEL] * _dot(conv_in, wco_ref[...])
             + gate[:, 2 * D_MODEL:] * _dot(mem, wmo_ref[...]))
    y = ALPHA * xn + _dot(mixed.astype(BF16), wo_ref[...])
    x1_ref[...] = _layer_norm(y, ln1g_ref[...], ln1b_ref[...])


def _mixer(x, attn, mem_k, mem_v, prev_a, prev_b, ln_g, ln_b, conv_w8, ln1_g, ln1_b, w, *, seq, rows):
    n = x.shape[0]
    tm = min(rows, n)
    multi_seq = seq < tm
    n_batches = max(1, tm // seq)
    row = lambda width: pl.BlockSpec((tm, width), lambda i: (i, 0))
    mem = pl.BlockSpec((n_batches, N_MEM, MEM_DIM), lambda i: ((i * tm) // (seq * n_batches), 0, 0))
    vec = _resident((1, D_MODEL))
    if multi_seq:
        prev_spec = row(D_CONV)
        state_spec = row(D_CONV)
        state_shape = jax.ShapeDtypeStruct((n, D_CONV), F32)
    else:
        prev_spec = pl.BlockSpec((1, SUBLANES, D_CONV), lambda i: ((i * tm) // seq, 0, 0))
        state_spec = prev_spec
        state_shape = jax.ShapeDtypeStruct((n // seq, SUBLANES, D_CONV), F32)
    weights = [w[k] for k in ("gb", "gc", "ci", "mq", "gl", "ao", "co", "mo", "o")]
    kern = functools.partial(_mixer_kernel, seq=seq, rows=tm)
    return pl.pallas_call(
        kern,
        grid=(n // tm,),
        in_specs=[row(D_MODEL), row(ATTN_DIM), mem, mem, prev_spec, prev_spec,
                  vec, vec, _resident((SUBLANES, D_CONV)), vec, vec] + [_resident(m.shape) for m in weights],
        out_specs=(row(D_MODEL), state_spec),
        out_shape=(jax.ShapeDtypeStruct((n, D_MODEL), F32), state_shape),
        scratch_shapes=[pltpu.VMEM((SUBLANES, D_CONV), F32)],
        compiler_params=_params("arbitrary"),
        name="mixer",
    )(x, attn, mem_k, mem_v, prev_a, prev_b, ln_g, ln_b, conv_w8, ln1_g, ln1_b, *weights)


def _route(x, wr_hi, wr_lo, bias):
    tm = x.shape[0]
    x_hi = x.astype(BF16)
    x_lo = (x - x_hi.astype(F32)).astype(BF16)
    logits = _dot_nt(wr_hi, x_hi) + (_dot_nt(wr_hi, x_lo) + _dot_nt(wr_lo, x_hi))
    s = _sigmoid(logits)
    sc = s + bias[:, 0:1]
    sub = lax.broadcasted_iota(I32, (GROUP_SIZE, tm), 0)
    neg = -jnp.inf
    groups, gscore = [], []
    for g in range(N_GROUPS):
        xg = sc[g * GROUP_SIZE:(g + 1) * GROUP_SIZE, :]
        m1 = jnp.max(xg, axis=0, keepdims=True)
        i1 = jnp.min(jnp.where(xg == m1, sub, GROUP_SIZE), axis=0, keepdims=True)
        m2 = jnp.max(jnp.where(sub == i1, neg, xg), axis=0, keepdims=True)
        groups.append(xg)
        gscore.append(m1 + m2)
    masked = []
    for g in range(N_GROUPS):
        rank = jnp.zeros((1, tm), I32)
        for o in range(N_GROUPS):
            if o == g:
                continue
            beats = (gscore[o] > gscore[g]) | ((gscore[o] == gscore[g]) & (o < g))
            rank = rank + beats.astype(I32)
        masked.append(jnp.where(rank < TOPK_GROUPS, groups[g], neg))
    v = jnp.concatenate(masked, axis=0)
    eidx = lax.broadcasted_iota(I32, (N_EXPERTS, tm), 0)
    chosen = jnp.zeros((N_EXPERTS, tm), jnp.bool_)
    for _ in range(TOP_K):
        m = jnp.max(v, axis=0, keepdims=True)
        first = jnp.min(jnp.where(v == m, eidx, N_EXPERTS), axis=0, keepdims=True)
        hit = eidx == first
        chosen = chosen | hit
        v = jnp.where(hit, neg, v)
    wsel = jnp.where(chosen, s, 0.0)
    return wsel / jnp.sum(wsel, axis=0, keepdims=True) * ROUTED_SCALE


def _expert_ffn(xb, w_gu, w_down, gate_col=None):
    gu = _dot(xb, w_gu)
    g, u = gu[:, :D_EXPERT], gu[:, D_EXPERT:]
    h = g * _sigmoid(g) * u
    if gate_col is not None:
        h = h * gate_col
    return _dot(h.astype(BF16), w_down)


def _moe_kernel(x_ref, wrh_ref, wrl_ref, br_ref, wgu_ref, wd_ref, wsgu_ref, wsd_ref, g_ref, b_ref,
                o_ref, xb_ref, gate_ref, acc_ref, *, experts_per_step):
    j = pl.program_id(1)

    @pl.when(j == 0)
    def _():
        x = x_ref[...]
        gate_t = _route(x, wrh_ref[...], wrl_ref[...], br_ref[...])
        pad = jnp.zeros((LANES - N_EXPERTS, x.shape[0]), F32)
        gate_ref[...] = jnp.concatenate([gate_t, pad], axis=0).T
        xb = x.astype(BF16)
        xb_ref[...] = xb
        acc_ref[...] = _expert_ffn(xb, wsgu_ref[...], wsd_ref[...])

    xb = xb_ref[...]
    gate = gate_ref[...]
    lane = lax.broadcasted_iota(I32, gate.shape, 1)
    acc = acc_ref[...]
    for e in range(experts_per_step):
        col = jnp.sum(jnp.where(lane == j * experts_per_step + e, gate, 0.0), axis=1, keepdims=True)
        acc = acc + _expert_ffn(xb, wgu_ref[e], wd_ref[e], col)
    acc_ref[...] = acc

    @pl.when(j == pl.num_programs(1) - 1)
    def _():
        o_ref[...] = _layer_norm(ALPHA * x_ref[...] + acc_ref[...], g_ref[...], b_ref[...])


def _moe(x, w, ln_g, ln_b, rows):
    n = x.shape[0]
    tm = min(rows, n)
    eps = MOE_EXPERTS_PER_STEP
    row = pl.BlockSpec((tm, D_MODEL), lambda i, j: (i, 0))
    kern = functools.partial(_moe_kernel, experts_per_step=eps)
    return pl.pallas_call(
        kern,
        grid=(n // tm, N_EXPERTS // eps),
        in_specs=[row, _resident(w["r_hi"].shape), _resident(w["r_lo"].shape), _resident(w["r_b"].shape),
                  pl.BlockSpec((eps, D_MODEL, 2 * D_EXPERT), lambda i, j: (j, 0, 0)),
                  pl.BlockSpec((eps, D_EXPERT, D_MODEL), lambda i, j: (j, 0, 0)),
                  _resident(w["s_gu"].shape), _resident(w["s_d"].shape),
                  _resident((1, D_MODEL)), _resident((1, D_MODEL))],
        out_specs=row,
        out_shape=jax.ShapeDtypeStruct((n, D_MODEL), F32),
        scratch_shapes=[pltpu.VMEM((tm, D_MODEL), BF16), pltpu.VMEM((tm, LANES), F32),
                        pltpu.VMEM((tm, D_MODEL), F32)],
        compiler_params=_params("arbitrary", "arbitrary"),
        name="moe",
    )(x, w["r_hi"], w["r_lo"], w["r_b"], w["e_gu"], w["e_d"], w["s_gu"], w["s_d"], ln_g, ln_b)


def _pad_rows(a, rows):
    return a if a.shape[1] == rows else jnp.pad(a, ((0, 0), (0, rows - a.shape[1]), (0, 0)))


def _head_major_per_batch(x, b, seq, tq_pad):
    heads = x.shape[1]
    if seq % Q_TILE == 0:
        return x.reshape(b, seq // Q_TILE, heads * Q_TILE, LANES)
    per_tile = Q_TILE // seq
    x = x.reshape(-1, heads, per_tile, seq, LANES).transpose(0, 2, 1, 3, 4).reshape(b, heads, seq, LANES)
    x = jnp.pad(x, ((0, 0), (0, 0), (0, tq_pad - seq), (0, 0)))
    return x.reshape(b, 1, heads * tq_pad, LANES)


def _layer(x, mem_k_bf, mem_v_bf, past_k, past_v, past_ki, conv_prev, w, ln, layer):
    b, seq, _ = x.shape
    past = past_k.shape[1]
    n = b * seq
    n_keys = past + seq
    x2 = x.reshape(n, D_MODEL)

    pos = past + jnp.arange(seq, dtype=I32)
    reps = max(1, min(PROJ_ROWS, n) // seq)
    pos_t = jnp.tile(pos, reps)
    tabs = (_rope_tables(pos_t, HEAD_DIM // ROT_FRACTION // 2)
            + _rope_tables(pos_t, IDX_DIM // ROT_FRACTION // 2,
                           (IDX_DIM, IDX_DIM + IDX_HEADS), IDX_HEADS ** -0.5))
    q, qi, k, v, kiwi, k_bf, ki_bf = _qkv_project(x2, ln["in_g"], ln["in_b"], w, tabs, PROJ_ROWS, seq)

    l_pad = -(-n_keys // KEY_TILE) * KEY_TILE
    tq_pad = -(-seq // Q_TILE) * Q_TILE
    k3 = jnp.concatenate([past_k.reshape(b, past, KV_DIM).astype(BF16), k_bf.reshape(b, seq, KV_DIM)], axis=1)
    v3 = jnp.concatenate([past_v.reshape(b, past, KV_DIM), v.reshape(b, seq, KV_DIM)], axis=1)
    ki_past = jnp.pad(past_ki, ((0, 0), (0, 0), (0, LANES - IDX_DIM))).astype(BF16)
    ki3 = jnp.concatenate([ki_past, ki_bf.reshape(b, seq, LANES)], axis=1)
    attn = _dsa_attention(
        _head_major_per_batch(q, b, seq, tq_pad), _head_major_per_batch(qi, b, seq, tq_pad),
        _pad_rows(kiwi.reshape(b, seq, LANES), tq_pad),
        _pad_rows(k3, l_pad), _pad_rows(v3, l_pad), _pad_rows(ki3, l_pad),
        past=past, seq=seq, n_keys=n_keys)
    attn = attn[:, :seq].reshape(n, ATTN_DIM)

    tm = min(PROJ_ROWS, n)
    if seq < tm:
        tpos = jnp.arange(seq)[None, :, None]
        prev_a = jnp.where(tpos == 0, conv_prev[:, 1:2], 0.0).reshape(n, D_CONV)
        prev_b = jnp.where(tpos == 0, conv_prev[:, 0:1], jnp.where(tpos == 1, conv_prev[:, 1:2], 0.0)).reshape(n, D_CONV)
    else:
        prev_a = jnp.pad(conv_prev, ((0, 0), (SUBLANES - (CONV_WIDTH - 1), 0), (0, 0)))
        prev_b = prev_a
    x1, state = _mixer(x2, attn, mem_k_bf, mem_v_bf, prev_a, prev_b, ln["in_g"], ln["in_b"], w["conv"],
                       ln["g1"], ln["b1"], w, seq=seq, rows=PROJ_ROWS)
    if seq < tm:
        new_conv = state.reshape(b, seq, D_CONV)[:, seq - (CONV_WIDTH - 1):]
    else:
        new_conv = state[:, SUBLANES - (CONV_WIDTH - 1):]

    y = _moe(x1, w, ln["g2"], ln["b2"], MOE_ROWS)
    return (y.reshape(b, seq, D_MODEL), k.reshape(b, seq, N_KV_HEADS, HEAD_DIM),
            v.reshape(b, seq, N_KV_HEADS, HEAD_DIM), kiwi[:, :IDX_DIM].reshape(b, seq, IDX_DIM), new_conv)


def _prep_weights(l, w_in, conv_w, w_attn_out, w_conv_out, w_mem_out, w_o, w_router, b_router,
                  w_exp_gu, w_exp_down, w_sh_gu, w_sh_down):
    sizes = [ATTN_DIM, KV_DIM, KV_DIM, IDX_HEADS * IDX_DIM, IDX_DIM, IDX_HEADS,
             D_CONV, D_CONV, D_CONV, MEM_DIM, N_BRANCH * D_MODEL]
    offs = np.concatenate([[0], np.cumsum(sizes)]).tolist()
    wi = w_in[l]
    part = lambda j: wi[:, offs[j]:offs[j + 1]]
    w_qi = part(3).reshape(D_MODEL, IDX_HEADS, IDX_DIM)
    w_qi = jnp.pad(w_qi, ((0, 0), (0, 0), (0, LANES - IDX_DIM))).reshape(D_MODEL, IDX_HEADS * LANES)
    w_kw = jnp.pad(jnp.concatenate([part(4), part(5)], axis=1), ((0, 0), (0, LANES - IDX_DIM - IDX_HEADS)))
    wr = w_router[l].T
    wr_hi = wr.astype(BF16)
    return {
        "q": part(0).astype(BF16), "kv": jnp.concatenate([part(1), part(2)], axis=1).astype(BF16),
        "qi": w_qi.astype(BF16), "kw": w_kw.astype(BF16),
        "gb": part(6).astype(BF16), "gc": part(7).astype(BF16), "ci": part(8).astype(BF16),
        "mq": part(9).astype(BF16), "gl": part(10).astype(BF16),
        "ao": w_attn_out[l].astype(BF16), "co": w_conv_out[l].astype(BF16), "mo": w_mem_out[l].astype(BF16),
        "o": w_o[l].astype(BF16),
        "conv": jnp.pad(conv_w[l], ((0, SUBLANES - CONV_WIDTH), (0, 0))),
        "r_hi": wr_hi, "r_lo": (wr - wr_hi.astype(F32)).astype(BF16),
        "r_b": jnp.broadcast_to(b_router[l][:, None], (N_EXPERTS, LANES)),
        "e_gu": w_exp_gu[l].astype(BF16), "e_d": w_exp_down[l].astype(BF16),
        "s_gu": w_sh_gu[l].astype(BF16), "s_d": w_sh_down[l].astype(BF16),
    }


def kernel(x_prompt, x_sample, mem_prompt, cache_attn_k, cache_attn_v, cache_idx_k, cache_mem_k, cache_mem_v,
           state_conv, ln_in_g, ln_in_b, w_in, conv_w, w_mem_kv, w_attn_out, w_conv_out, w_mem_out, w_o,
           ln1_g, ln1_b, w_router, b_router, w_exp_gu, w_exp_down, w_sh_gu, w_sh_down, ln2_g, ln2_b):
    assert DEPTH == 1 and w_in.shape[0] == 1
    l = 0
    bp, n_mem = mem_prompt.shape[0], mem_prompt.shape[1]
    w = _prep_weights(l, w_in, conv_w, w_attn_out, w_conv_out, w_mem_out, w_o, w_router, b_router,
                      w_exp_gu, w_exp_down, w_sh_gu, w_sh_down)
    ln = {"in_g": ln_in_g[None, :], "in_b": ln_in_b[None, :], "g1": ln1_g[l][None, :], "b1": ln1_b[l][None, :],
          "g2": ln2_g[l][None, :], "b2": ln2_b[l][None, :]}

    mem_kv = _project(mem_prompt.reshape(bp * n_mem, D_MODEL), w_mem_kv[l].astype(BF16))
    mem_k = mem_kv[:, :MEM_DIM].reshape(bp, n_mem, MEM_DIM)
    mem_v = mem_kv[:, MEM_DIM:].reshape(bp, n_mem, MEM_DIM)

    zeros = lambda *s: jnp.zeros(s, F32)
    yp, kp, vp, kip, cp = _layer(
        x_prompt, mem_k.astype(BF16), mem_v.astype(BF16),
        zeros(bp, 0, N_KV_HEADS, HEAD_DIM), zeros(bp, 0, N_KV_HEADS, HEAD_DIM), zeros(bp, 0, IDX_DIM),
        zeros(bp, CONV_WIDTH - 1, D_CONV), w, ln, l)
    bs = x_sample.shape[0]
    ys, ks, vs, kis, cs = _layer(
        x_sample, cache_mem_k[l].reshape(bs, n_mem, MEM_DIM).astype(BF16),
        cache_mem_v[l].reshape(bs, n_mem, MEM_DIM).astype(BF16),
        cache_attn_k[l], cache_attn_v[l], cache_idx_k[l], state_conv[l], w, ln, l)

    mem_shape = (1, bp, n_mem, MEM_HEADS, MEM_HEAD_DIM)
    return (yp, ys, kp[None], vp[None], kip[None], mem_k.reshape(mem_shape), mem_v.reshape(mem_shape), cp[None],
            ks[None], vs[None], kis[None], cs[None])
```

```python
import functools

import jax
import jax.numpy as jnp
import numpy as np
from jax import lax
from jax.experimental import pallas as pl
from jax.experimental.pallas import tpu as pltpu

D_MODEL = 1024
DEPTH = 1
CHUNK = 64
HEAD_DIM = 128
N_HEADS = D_MODEL // HEAD_DIM
N_KV_HEADS = max(1, N_HEADS // 4)
KV_GROUP = N_HEADS // N_KV_HEADS
ATTN_DIM = N_HEADS * HEAD_DIM
KV_DIM = N_KV_HEADS * HEAD_DIM
ROT_FRACTION = 4
ROPE_THETA = 500000.0
IDX_HEADS = 8
IDX_DIM = 64
TOPK_MAX = 256
D_CONV = D_MODEL
CONV_WIDTH = 3
N_MEM = 256
MEM_HEADS = 4
MEM_HEAD_DIM = D_MODEL // MEM_HEADS
MEM_DIM = MEM_HEADS * MEM_HEAD_DIM
N_BRANCH = 3
N_EXPERTS = 64
TOP_K = 8
N_GROUPS = 8
GROUP_SIZE = N_EXPERTS // N_GROUPS
TOPK_GROUPS = 4
D_EXPERT = D_MODEL // 4
ROUTED_SCALE = 2.5
LN_EPS = 1e-5
ALPHA = (2.0 * DEPTH) ** 0.25

LANES = 128
SUBLANES = 8
VMEM_LIMIT_BYTES = 56 * 1024 * 1024

Q_TILE = 128
KEY_TILE = 512
HEADS_PER_UNIT = 2
COUNT_ROWS = 64
DENOM_ROWS = 16
PROJ_ROWS = 256
MOE_ROWS = 1024
MOE_EXPERTS_PER_STEP = 4

F32 = jnp.float32
BF16 = jnp.bfloat16
I32 = jnp.int32
INT_MIN = -(2 ** 31)
INT_MAX = 2 ** 31 - 1
MASKED_LOGIT = -1e30
QK_SCALE_LOG2E = HEAD_DIM ** -0.5 * float(np.log2(np.e))

_NT = (((1,), (1,)), ((), ()))


def _dot(a, b):
    return jnp.dot(a, b, preferred_element_type=F32)


def _dot_nt(a, b):
    return lax.dot_general(a, b, _NT, preferred_element_type=F32)


def _layer_norm(x, g, b):
    mu = jnp.mean(x, axis=-1, keepdims=True)
    xc = x - mu
    var = jnp.mean(xc * xc, axis=-1, keepdims=True)
    return xc * lax.rsqrt(var + LN_EPS) * g + b


def _sigmoid(x):
    return 1.0 / (1.0 + jnp.exp(-x))


def _resident(shape):
    nd = len(shape)
    return pl.BlockSpec(shape, lambda *_: (0,) * nd, pipeline_mode=pl.Buffered(1))


def _params(*sem):
    return pltpu.CompilerParams(dimension_semantics=sem, vmem_limit_bytes=VMEM_LIMIT_BYTES)


def _rope_tables(pos, half, extra_scale_lanes=None, extra_scale=1.0):
    inv_freq = ROPE_THETA ** (-jnp.arange(half, dtype=F32) / half)
    ang = pos.astype(F32)[:, None] * inv_freq[None, :]
    cos, sin = jnp.cos(ang), jnp.sin(ang)
    n = pos.shape[0]
    c = jnp.concatenate([cos, cos, jnp.ones((n, LANES - 2 * half), F32)], axis=1)
    s = jnp.concatenate([-sin, sin, jnp.zeros((n, LANES - 2 * half), F32)], axis=1)
    if extra_scale_lanes is not None:
        lo, hi = extra_scale_lanes
        lane = jnp.arange(LANES)[None, :]
        c = jnp.where((lane >= lo) & (lane < hi), jnp.float32(extra_scale), c)
    return c, s


def _rope(x, c, s, half):
    rows, width = x.shape
    lane = lax.broadcasted_iota(I32, (rows, LANES), 1)
    first = lane < half
    outs = []
    for j in range(width // LANES):
        xj = x[:, j * LANES:(j + 1) * LANES]
        swap = jnp.where(first, pltpu.roll(xj, LANES - half, 1), pltpu.roll(xj, half, 1))
        outs.append(xj * c + swap * s)
    return outs[0] if len(outs) == 1 else jnp.concatenate(outs, axis=1)


def _proj_kernel(x_ref, w_ref, o_ref):
    o_ref[...] = _dot(x_ref[...].astype(BF16), w_ref[...])


def _project(x, w_bf, col_tile=512):
    n, d = x.shape
    m = w_bf.shape[1]
    return pl.pallas_call(
        _proj_kernel,
        grid=(m // col_tile,),
        in_specs=[pl.BlockSpec((n, d), lambda j: (0, 0)),
                  pl.BlockSpec((d, col_tile), lambda j: (0, j))],
        out_specs=pl.BlockSpec((n, col_tile), lambda j: (0, j)),
        out_shape=jax.ShapeDtypeStruct((n, m), F32),
        compiler_params=_params("arbitrary"),
        name="mem_kv_proj",
    )(x, w_bf)


def _store_head_major(ref, x):
    for r in range(ref.shape[0]):
        for h in range(ref.shape[1]):
            ref[r, h] = x[r * Q_TILE:(r + 1) * Q_TILE, h * LANES:(h + 1) * LANES]


def _qkv_kernel(x_ref, g_ref, b_ref, wq_ref, wqi_ref, wkv_ref, wkw_ref, ch_ref, sh_ref, ci_ref, si_ref,
                q_ref, qi_ref, k_ref, v_ref, kiwi_ref, kbf_ref, kibf_ref):
    xb = _layer_norm(x_ref[...], g_ref[...], b_ref[...]).astype(BF16)
    ch, sh, ci, si = ch_ref[...], sh_ref[...], ci_ref[...], si_ref[...]
    q = _rope(_dot(xb, wq_ref[...]), ch, sh, HEAD_DIM // ROT_FRACTION // 2)
    _store_head_major(q_ref, (q * QK_SCALE_LOG2E).astype(BF16))
    qi = _rope(_dot(xb, wqi_ref[...]), ci, si, IDX_DIM // ROT_FRACTION // 2)
    _store_head_major(qi_ref, qi.astype(BF16))
    kv = _dot(xb, wkv_ref[...])
    k = _rope(kv[:, :KV_DIM], ch, sh, HEAD_DIM // ROT_FRACTION // 2)
    k_ref[...] = k
    kbf_ref[...] = k.astype(BF16)
    v_ref[...] = kv[:, KV_DIM:]
    kw = _rope(_dot(xb, wkw_ref[...]), ci, si, IDX_DIM // ROT_FRACTION // 2)
    kiwi_ref[...] = kw
    kibf_ref[...] = kw.astype(BF16)


def _qkv_project(x, ln_g, ln_b, w, tabs, rows, seq):
    n = x.shape[0]
    tm = min(rows, n)
    tab_blocks = max(seq, tm) // tm
    row = lambda width: pl.BlockSpec((tm, width), lambda i: (i, 0))
    tab = pl.BlockSpec((tm, LANES), lambda i: (i % tab_blocks, 0))
    hm_shape = lambda heads: (n // Q_TILE, heads, Q_TILE, LANES)
    hm = lambda heads: pl.BlockSpec((tm // Q_TILE, heads, Q_TILE, LANES), lambda i: (i, 0, 0, 0))
    out_shapes = (
        jax.ShapeDtypeStruct(hm_shape(N_HEADS), BF16),
        jax.ShapeDtypeStruct(hm_shape(IDX_HEADS), BF16),
        jax.ShapeDtypeStruct((n, KV_DIM), F32),
        jax.ShapeDtypeStruct((n, KV_DIM), F32),
        jax.ShapeDtypeStruct((n, LANES), F32),
        jax.ShapeDtypeStruct((n, KV_DIM), BF16),
        jax.ShapeDtypeStruct((n, LANES), BF16),
    )
    return pl.pallas_call(
        _qkv_kernel,
        grid=(n // tm,),
        in_specs=[row(D_MODEL), _resident((1, D_MODEL)), _resident((1, D_MODEL)),
                  _resident(w["q"].shape), _resident(w["qi"].shape), _resident(w["kv"].shape),
                  _resident(w["kw"].shape), tab, tab, tab, tab],
        out_specs=(hm(N_HEADS), hm(IDX_HEADS), row(KV_DIM), row(KV_DIM), row(LANES),
                   row(KV_DIM), row(LANES)),
        out_shape=out_shapes,
        compiler_params=_params("arbitrary"),
        name="qkv_proj",
    )(x, ln_g, ln_b, w["q"], w["qi"], w["kv"], w["kw"], *tabs)


def _sortable_key(score):
    bits = lax.bitcast_convert_type(score + 0.0, I32)
    return bits ^ ((bits >> 31) & INT_MAX)


def _dsa_kernel(q_ref, qi_ref, kiwi_ref, k_ref, v_ref, ki_ref, o_ref,
                vt_ref, key_ref, acc_ref, m_ref, j_ref,
                *, past, seq, n_keys, topk, n_key_tiles):
    i = pl.program_id(1)
    tq, kt = Q_TILE, KEY_TILE
    uq = HEADS_PER_UNIT * tq

    @pl.when(i == 0)
    def _():
        ones = jnp.ones((DENOM_ROWS, kt), BF16)
        for c in range(n_key_tiles):
            for n in range(N_KV_HEADS):
                vt_ref[c, n, :HEAD_DIM, :] = (
                    v_ref[0, c * kt:(c + 1) * kt, n * HEAD_DIM:(n + 1) * HEAD_DIM].T.astype(BF16))
                vt_ref[c, n, HEAD_DIM:, :] = ones

    lane = lax.broadcasted_iota(I32, (1, tq), 1)
    t = i * tq + lane
    q_real = t < seq
    lim = jnp.minimum(((past + t) // CHUNK + 1) * CHUNK, n_keys)
    lim_max = jnp.minimum(((past + i * tq + tq - 1) // CHUNK + 1) * CHUNK, n_keys)
    n_tiles = (lim_max + kt - 1) // kt
    row = lax.broadcasted_iota(I32, (kt, tq), 0)

    wi = kiwi_ref[0].T[IDX_DIM:IDX_DIM + IDX_HEADS, :] * (IDX_DIM ** -0.5)

    def score_tile(j, carry):
        kib = ki_ref[0, pl.ds(pl.multiple_of(j * kt, kt), kt), :]
        rel_of = lambda u: _dot_nt(kib, qi_ref[0, 0, u * uq:(u + 1) * uq, :])
        n_units = IDX_HEADS // HEADS_PER_UNIT
        acc = jnp.zeros((kt, tq), F32)
        rel_next = rel_of(0)
        for u in range(n_units):
            rel, rel_next = rel_next, (rel_of(u + 1) if u + 1 < n_units else None)
            for g in range(HEADS_PER_UNIT):
                h = u * HEADS_PER_UNIT + g
                acc = acc + jnp.maximum(rel[:, g * tq:(g + 1) * tq], 0.0) * wi[h:h + 1, :]
        score = jnp.where(j * kt + row < lim, acc, -jnp.inf)
        key_ref[j] = _sortable_key(score)
        return carry

    lax.fori_loop(0, n_tiles, score_tile, 0)

    def count(pred):
        def body(j, acc):
            blk = key_ref[j]
            for r in range(kt // COUNT_ROWS):
                rows = slice(r * COUNT_ROWS, (r + 1) * COUNT_ROWS)
                acc = jnp.where(pred(blk[rows], j * kt + row[rows]), acc + 1, acc)
            return acc
        acc = lax.fori_loop(0, n_tiles, body, jnp.zeros((COUNT_ROWS, tq), I32))
        return jnp.sum(acc, axis=0, keepdims=True)

    def count_ge(cand):
        return count(lambda keys, kidx: keys >= cand)

    def bisect(b, state):
        thr, cnt = state
        cand = thr | jnp.left_shift(jnp.int32(1), 30 - b)
        c = count_ge(cand)
        take = c >= topk
        return jnp.where(take, cand, thr), jnp.where(take, c, cnt)

    c0 = count_ge(jnp.zeros((1, tq), I32))
    nonneg = c0 >= topk
    thr, cnt = lax.fori_loop(0, 31, bisect, (jnp.where(nonneg, 0, INT_MIN).astype(I32),
                                             jnp.where(nonneg, c0, n_tiles * kt).astype(I32)))
    tied = jnp.max(jnp.where(q_real, cnt, topk)) > topk

    j_ref[...] = jnp.full((1, tq), INT_MAX, I32)

    @pl.when(tied)
    def _():
        n_ties_wanted = topk - count(lambda keys, kidx: keys > thr)
        n_bits = int(n_key_tiles * kt).bit_length()

        def bisect_idx(it, cut):
            cand = cut | jnp.left_shift(jnp.int32(1), n_bits - 1 - it)
            c = count(lambda keys, kidx: (keys == thr) & (kidx < cand))
            return jnp.where(c <= n_ties_wanted, cand, cut)

        j_ref[...] = lax.fori_loop(0, n_bits, bisect_idx, jnp.zeros((1, tq), I32))

    j_cut = j_ref[...]

    m_ref[...] = jnp.full(m_ref.shape, MASKED_LOGIT, F32)
    acc_ref[...] = jnp.zeros(acc_ref.shape, F32)

    def attend_tile(j, carry):
        keys = key_ref[j]
        kidx = j * kt + row
        sel = ((keys > thr) | ((keys == thr) & (kidx < j_cut))) & (kidx < lim)
        start = pl.multiple_of(j * kt, kt)
        n_units = N_HEADS // HEADS_PER_UNIT
        kv_of = lambda u: u * HEADS_PER_UNIT // KV_GROUP

        def logits(u):
            kb = k_ref[0, pl.ds(start, kt), kv_of(u) * HEAD_DIM:(kv_of(u) + 1) * HEAD_DIM]
            return _dot_nt(kb, q_ref[0, 0, u * uq:(u + 1) * uq, :])

        s_next = logits(0)
        for u in range(n_units):
            s, s_next = s_next, (logits(u + 1) if u + 1 < n_units else None)
            s = jnp.concatenate([jnp.where(sel, s[:, g * tq:(g + 1) * tq], MASKED_LOGIT)
                                 for g in range(HEADS_PER_UNIT)], axis=1)
            m_old = m_ref[u]
            m_new = jnp.maximum(m_old, jnp.max(s, axis=0, keepdims=True))
            p = jnp.exp2(s - m_new).astype(BF16)
            acc_ref[u] = jnp.exp2(m_old - m_new) * acc_ref[u] + _dot(vt_ref[j, kv_of(u)], p)
            m_ref[u] = m_new
        return carry

    lax.fori_loop(0, n_tiles, attend_tile, 0)

    for h in range(N_HEADS):
        u, cols = h // HEADS_PER_UNIT, slice((h % HEADS_PER_UNIT) * tq, (h % HEADS_PER_UNIT + 1) * tq)
        o = acc_ref[u, :HEAD_DIM, cols] / acc_ref[u, HEAD_DIM:HEAD_DIM + 1, cols]
        o_ref[0, :, h * HEAD_DIM:(h + 1) * HEAD_DIM] = o.T.astype(BF16)


def _dsa_attention(q, qi, kiwi, k_bf, v, ki_bf, *, past, seq, n_keys):
    b, tq_pad, _ = kiwi.shape
    l_pad = k_bf.shape[1]
    n_key_tiles = l_pad // KEY_TILE
    topk = min(TOPK_MAX, n_keys // 4)
    qblk = lambda width: pl.BlockSpec((1, Q_TILE, width), lambda bi, i: (bi, i, 0))
    hmblk = lambda heads: pl.BlockSpec((1, 1, heads * Q_TILE, LANES), lambda bi, i: (bi, i, 0, 0))
    kblk = lambda width: pl.BlockSpec((1, l_pad, width), lambda bi, i: (bi, 0, 0),
                                      pipeline_mode=pl.Buffered(1))
    kern = functools.partial(_dsa_kernel, past=past, seq=seq, n_keys=n_keys, topk=topk,
                             n_key_tiles=n_key_tiles)
    return pl.pallas_call(
        kern,
        grid=(b, tq_pad // Q_TILE),
        in_specs=[hmblk(N_HEADS), hmblk(IDX_HEADS), qblk(LANES),
                  kblk(KV_DIM), kblk(KV_DIM), kblk(LANES)],
        out_specs=qblk(ATTN_DIM),
        out_shape=jax.ShapeDtypeStruct((b, tq_pad, ATTN_DIM), BF16),
        scratch_shapes=[
            pltpu.VMEM((n_key_tiles, N_KV_HEADS, HEAD_DIM + DENOM_ROWS, KEY_TILE), BF16),
            pltpu.VMEM((n_key_tiles, KEY_TILE, Q_TILE), I32),
            pltpu.VMEM((N_HEADS // HEADS_PER_UNIT, HEAD_DIM + DENOM_ROWS, HEADS_PER_UNIT * Q_TILE), F32),
            pltpu.VMEM((N_HEADS // HEADS_PER_UNIT, 1, HEADS_PER_UNIT * Q_TILE), F32),
            pltpu.VMEM((1, Q_TILE), I32),
        ],
        compiler_params=_params("arbitrary", "arbitrary"),
        name="dsa_attention",
    )(q, qi, kiwi, k_bf, v, ki_bf)


def _mixer_kernel(x_ref, attn_ref, memk_ref, memv_ref, prev_a_ref, prev_b_ref,
                  lng_ref, lnb_ref, cw_ref, ln1g_ref, ln1b_ref,
                  wgb_ref, wgc_ref, wci_ref, wmq_ref, wgl_ref, wao_ref, wco_ref, wmo_ref, wo_ref,
                  x1_ref, state_ref, carry_ref, *, seq, rows):
    i = pl.program_id(0)
    multi_seq = seq < rows
    xn = _layer_norm(x_ref[...], lng_ref[...], lnb_ref[...])
    xb = xn.astype(BF16)

    u = _dot(xb, wgc_ref[...]) * _dot(xb, wci_ref[...])
    ridx = lax.broadcasted_iota(I32, u.shape, 0)
    if multi_seq:
        tpos = ridx & (seq - 1)
        s1 = jnp.where(tpos == 0, prev_a_ref[...], pltpu.roll(u, 1, 0))
        s2 = jnp.where(tpos < 2, prev_b_ref[...], pltpu.roll(u, 2, 0))
        state_ref[...] = u
    else:
        blocks_per_seq = seq // rows
        @pl.when(i % blocks_per_seq == 0)
        def _():
            carry_ref[...] = prev_a_ref[0]

        prev = carry_ref[...]
        p1, p2 = prev[SUBLANES - 1:SUBLANES, :], prev[SUBLANES - 2:SUBLANES - 1, :]
        s1 = jnp.where(ridx == 0, p1, pltpu.roll(u, 1, 0))
        s2 = jnp.where(ridx == 0, p2, jnp.where(ridx == 1, p1, pltpu.roll(u, 2, 0)))
        carry_ref[...] = u[rows - SUBLANES:, :]
        state_ref[0] = u[rows - SUBLANES:, :]
    cw = cw_ref[...]
    conv = cw[0:1, :] * s2 + cw[1:2, :] * s1 + cw[2:3, :] * u
    conv_in = (_dot(xb, wgb_ref[...]) * conv).astype(BF16)

    mq = _dot(xb, wmq_ref[...]).astype(BF16)
    n_batches = max(1, rows // seq)
    rb = rows // n_batches
    mem_rows = []
    for bi in range(n_batches):
        heads = []
        for h in range(MEM_HEADS):
            cols = slice(h * MEM_HEAD_DIM, (h + 1) * MEM_HEAD_DIM)
            s = _dot_nt(mq[bi * rb:(bi + 1) * rb, cols], memk_ref[bi, :, cols]) * (MEM_HEAD_DIM ** -0.5)
            e = jnp.exp(s - jnp.max(s, axis=-1, keepdims=True))
            p = e / jnp.sum(e, axis=-1, keepdims=True)
            heads.append(_dot(p.astype(BF16), memv_ref[bi, :, cols]))
        mem_rows.append(jnp.concatenate(heads, axis=1))
    mem = (mem_rows[0] if n_batches == 1 else jnp.concatenate(mem_rows, axis=0)).astype(BF16)

    gate = _sigmoid(_dot(xb, wgl_ref[...]))
    mixed = (gate[:, :D_MODEL] * _dot(attn_ref[...], wao_ref[...])
             + gate[:, D_MODEL:2 * D_MODEL] * _dot(conv_in, wco_ref[...])
             + gate[:, 2 * D_MODEL:] * _dot(mem, wmo_ref[...]))
    y = ALPHA * xn + _dot(mixed.astype(BF16), wo_ref[...])
    x1_ref[...] = _layer_norm(y, ln1g_ref[...], ln1b_ref[...])


def _mixer(x, attn, mem_k, mem_v, prev_a, prev_b, ln_g, ln_b, conv_w8, ln1_g, ln1_b, w, *, seq, rows):
    n = x.shape[0]
    tm = min(rows, n)
    multi_seq = seq < tm
    n_batches = max(1, tm // seq)
    row = lambda width: pl.BlockSpec((tm, width), lambda i: (i, 0))
    mem = pl.BlockSpec((n_batches, N_MEM, MEM_DIM), lambda i: ((i * tm) // (seq * n_batches), 0, 0))
    vec = _resident((1, D_MODEL))
    if multi_seq:
        prev_spec = row(D_CONV)
        state_spec = row(D_CONV)
        state_shape = jax.ShapeDtypeStruct((n, D_CONV), F32)
    else:
        prev_spec = pl.BlockSpec((1, SUBLANES, D_CONV), lambda i: ((i * tm) // seq, 0, 0))
        state_spec = prev_spec
        state_shape = jax.ShapeDtypeStruct((n // seq, SUBLANES, D_CONV), F32)
    weights = [w[k] for k in ("gb", "gc", "ci", "mq", "gl", "ao", "co", "mo", "o")]
    kern = functools.partial(_mixer_kernel, seq=seq, rows=tm)
    return pl.pallas_call(
        kern,
        grid=(n // tm,),
        in_specs=[row(D_MODEL), row(ATTN_DIM), mem, mem, prev_spec, prev_spec,
                  vec, vec, _resident((SUBLANES, D_CONV)), vec, vec] + [_resident(m.shape) for m in weights],
        out_specs=(row(D_MODEL), state_spec),
        out_shape=(jax.ShapeDtypeStruct((n, D_MODEL), F32), state_shape),
        scratch_shapes=[pltpu.VMEM((SUBLANES, D_CONV), F32)],
        compiler_params=_params("arbitrary"),
        name="mixer",
    )(x, attn, mem_k, mem_v, prev_a, prev_b, ln_g, ln_b, conv_w8, ln1_g, ln1_b, *weights)


def _route(x, wr_hi, wr_lo, bias):
    tm = x.shape[0]
    x_hi = x.astype(BF16)
    x_lo = (x - x_hi.astype(F32)).astype(BF16)
    logits = _dot_nt(wr_hi, x_hi) + (_dot_nt(wr_hi, x_lo) + _dot_nt(wr_lo, x_hi))
    s = _sigmoid(logits)
    sc = s + bias[:, 0:1]
    sub = lax.broadcasted_iota(I32, (GROUP_SIZE, tm), 0)
    neg = -jnp.inf
    groups, gscore = [], []
    for g in range(N_GROUPS):
        xg = sc[g * GROUP_SIZE:(g + 1) * GROUP_SIZE, :]
        m1 = jnp.max(xg, axis=0, keepdims=True)
        i1 = jnp.min(jnp.where(xg == m1, sub, GROUP_SIZE), axis=0, keepdims=True)
        m2 = jnp.max(jnp.where(sub == i1, neg, xg), axis=0, keepdims=True)
        groups.append(xg)
        gscore.append(m1 + m2)
    masked = []
    for g in range(N_GROUPS):
        rank = jnp.zeros((1, tm), I32)
        for o in range(N_GROUPS):
            if o == g:
                continue
            beats = (gscore[o] > gscore[g]) | ((gscore[o] == gscore[g]) & (o < g))
            rank = rank + beats.astype(I32)
        masked.append(jnp.where(rank < TOPK_GROUPS, groups[g], neg))
    v = jnp.concatenate(masked, axis=0)
    eidx = lax.broadcasted_iota(I32, (N_EXPERTS, tm), 0)
    chosen = jnp.zeros((N_EXPERTS, tm), jnp.bool_)
    for _ in range(TOP_K):
        m = jnp.max(v, axis=0, keepdims=True)
        first = jnp.min(jnp.where(v == m, eidx, N_EXPERTS), axis=0, keepdims=True)
        hit = eidx == first
        chosen = chosen | hit
        v = jnp.where(hit, neg, v)
    wsel = jnp.where(chosen, s, 0.0)
    return wsel / jnp.sum(wsel, axis=0, keepdims=True) * ROUTED_SCALE


def _expert_ffn(xb, w_gu, w_down, gate_col=None):
    gu = _dot(xb, w_gu)
    g, u = gu[:, :D_EXPERT], gu[:, D_EXPERT:]
    h = g * _sigmoid(g) * u
    if gate_col is not None:
        h = h * gate_col
    return _dot(h.astype(BF16), w_down)


def _moe_kernel(x_ref, wrh_ref, wrl_ref, br_ref, wgu_ref, wd_ref, wsgu_ref, wsd_ref, g_ref, b_ref,
                o_ref, xb_ref, gate_ref, acc_ref, *, experts_per_step):
    j = pl.program_id(1)

    @pl.when(j == 0)
    def _():
        x = x_ref[...]
        gate_t = _route(x, wrh_ref[...], wrl_ref[...], br_ref[...])
        pad = jnp.zeros((LANES - N_EXPERTS, x.shape[0]), F32)
        gate_ref[...] = jnp.concatenate([gate_t, pad], axis=0).T
        xb = x.astype(BF16)
        xb_ref[...] = xb
        acc_ref[...] = _expert_ffn(xb, wsgu_ref[...], wsd_ref[...])

    xb = xb_ref[...]
    gate = gate_ref[...]
    lane = lax.broadcasted_iota(I32, gate.shape, 1)
    acc = acc_ref[...]
    for e in range(experts_per_step):
        col = jnp.sum(jnp.where(lane == j * experts_per_step + e, gate, 0.0), axis=1, keepdims=True)
        acc = acc + _expert_ffn(xb, wgu_ref[e], wd_ref[e], col)
    acc_ref[...] = acc

    @pl.when(j == pl.num_programs(1) - 1)
    def _():
        o_ref[...] = _layer_norm(ALPHA * x_ref[...] + acc_ref[...], g_ref[...], b_ref[...])


def _moe(x, w, ln_g, ln_b, rows):
    n = x.shape[0]
    tm = min(rows, n)
    eps = MOE_EXPERTS_PER_STEP
    row = pl.BlockSpec((tm, D_MODEL), lambda i, j: (i, 0))
    kern = functools.partial(_moe_kernel, experts_per_step=eps)
    return pl.pallas_call(
        kern,
        grid=(n // tm, N_EXPERTS // eps),
        in_specs=[row, _resident(w["r_hi"].shape), _resident(w["r_lo"].shape), _resident(w["r_b"].shape),
                  pl.BlockSpec((eps, D_MODEL, 2 * D_EXPERT), lambda i, j: (j, 0, 0)),
                  pl.BlockSpec((eps, D_EXPERT, D_MODEL), lambda i, j: (j, 0, 0)),
                  _resident(w["s_gu"].shape), _resident(w["s_d"].shape),
                  _resident((1, D_MODEL)), _resident((1, D_MODEL))],
        out_specs=row,
        out_shape=jax.ShapeDtypeStruct((n, D_MODEL), F32),
        scratch_shapes=[pltpu.VMEM((tm, D_MODEL), BF16), pltpu.VMEM((tm, LANES), F32),
                        pltpu.VMEM((tm, D_MODEL), F32)],
        compiler_params=_params("arbitrary", "arbitrary"),
        name="moe",
    )(x, w["r_hi"], w["r_lo"], w["r_b"], w["e_gu"], w["e_d"], w["s_gu"], w["s_d"], ln_g, ln_b)


def _pad_rows(a, rows):
    return a if a.shape[1] == rows else jnp.pad(a, ((0, 0), (0, rows - a.shape[1]), (0, 0)))


def _head_major_per_batch(x, b, seq, tq_pad):
    heads = x.shape[1]
    if seq % Q_TILE == 0:
        return x.reshape(b, seq // Q_TILE, heads * Q_TILE, LANES)
    per_tile = Q_TILE // seq
    x = x.reshape(-1, heads, per_tile, seq, LANES).transpose(0, 2, 1, 3, 4).reshape(b, heads, seq, LANES)
    x = jnp.pad(x, ((0, 0), (0, 0), (0, tq_pad - seq), (0, 0)))
    return x.reshape(b, 1, heads * tq_pad, LANES)


def _layer(x, mem_k_bf, mem_v_bf, past_k, past_v, past_ki, conv_prev, w, ln, layer):
    b, seq, _ = x.shape
    past = past_k.shape[1]
    n = b * seq
    n_keys = past + seq
    x2 = x.reshape(n, D_MODEL)

    pos = past + jnp.arange(seq, dtype=I32)
    reps = max(1, min(PROJ_ROWS, n) // seq)
    pos_t = jnp.tile(pos, reps)
    tabs = (_rope_tables(pos_t, HEAD_DIM // ROT_FRACTION // 2)
            + _rope_tables(pos_t, IDX_DIM // ROT_FRACTION // 2,
                           (IDX_DIM, IDX_DIM + IDX_HEADS), IDX_HEADS ** -0.5))
    q, qi, k, v, kiwi, k_bf, ki_bf = _qkv_project(x2, ln["in_g"], ln["in_b"], w, tabs, PROJ_ROWS, seq)

    l_pad = -(-n_keys // KEY_TILE) * KEY_TILE
    tq_pad = -(-seq // Q_TILE) * Q_TILE
    k3 = jnp.concatenate([past_k.reshape(b, past, KV_DIM).astype(BF16), k_bf.reshape(b, seq, KV_DIM)], axis=1)
    v3 = jnp.concatenate([past_v.reshape(b, past, KV_DIM), v.reshape(b, seq, KV_DIM)], axis=1)
    ki_past = jnp.pad(past_ki, ((0, 0), (0, 0), (0, LANES - IDX_DIM))).astype(BF16)
    ki3 = jnp.concatenate([ki_past, ki_bf.reshape(b, seq, LANES)], axis=1)
    attn = _dsa_attention(
        _head_major_per_batch(q, b, seq, tq_pad), _head_major_per_batch(qi, b, seq, tq_pad),
        _pad_rows(kiwi.reshape(b, seq, LANES), tq_pad),
        _pad_rows(k3, l_pad), _pad_rows(v3, l_pad), _pad_rows(ki3, l_pad),
        past=past, seq=seq, n_keys=n_keys)
    attn = attn[:, :seq].reshape(n, ATTN_DIM)

    tm = min(PROJ_ROWS, n)
    if seq < tm:
        tpos = jnp.arange(seq)[None, :, None]
        prev_a = jnp.where(tpos == 0, conv_prev[:, 1:2], 0.0).reshape(n, D_CONV)
        prev_b = jnp.where(tpos == 0, conv_prev[:, 0:1], jnp.where(tpos == 1, conv_prev[:, 1:2], 0.0)).reshape(n, D_CONV)
    else:
        prev_a = jnp.pad(conv_prev, ((0, 0), (SUBLANES - (CONV_WIDTH - 1), 0), (0, 0)))
        prev_b = prev_a
    x1, state = _mixer(x2, attn, mem_k_bf, mem_v_bf, prev_a, prev_b, ln["in_g"], ln["in_b"], w["conv"],
                       ln["g1"], ln["b1"], w, seq=seq, rows=PROJ_ROWS)
    if seq < tm:
        new_conv = state.reshape(b, seq, D_CONV)[:, seq - (CONV_WIDTH - 1):]
    else:
        new_conv = state[:, SUBLANES - (CONV_WIDTH - 1):]

    y = _moe(x1, w, ln["g2"], ln["b2"], MOE_ROWS)
    return (y.reshape(b, seq, D_MODEL), k.reshape(b, seq, N_KV_HEADS, HEAD_DIM),
            v.reshape(b, seq, N_KV_HEADS, HEAD_DIM), kiwi[:, :IDX_DIM].reshape(b, seq, IDX_DIM), new_conv)


def _prep_weights(l, w_in, conv_w, w_attn_out, w_conv_out, w_mem_out, w_o, w_router, b_router,
                  w_exp_gu, w_exp_down, w_sh_gu, w_sh_down):
    sizes = [ATTN_DIM, KV_DIM, KV_DIM, IDX_HEADS * IDX_DIM, IDX_DIM, IDX_HEADS,
             D_CONV, D_CONV, D_CONV, MEM_DIM, N_BRANCH * D_MODEL]
    offs = np.concatenate([[0], np.cumsum(sizes)]).tolist()
    wi = w_in[l]
    part = lambda j: wi[:, offs[j]:offs[j + 1]]
    w_qi = part(3).reshape(D_MODEL, IDX_HEADS, IDX_DIM)
    w_qi = jnp.pad(w_qi, ((0, 0), (0, 0), (0, LANES - IDX_DIM))).reshape(D_MODEL, IDX_HEADS * LANES)
    w_kw = jnp.pad(jnp.concatenate([part(4), part(5)], axis=1), ((0, 0), (0, LANES - IDX_DIM - IDX_HEADS)))
    wr = w_router[l].T
    wr_hi = wr.astype(BF16)
    return {
        "q": part(0).astype(BF16), "kv": jnp.concatenate([part(1), part(2)], axis=1).astype(BF16),
        "qi": w_qi.astype(BF16), "kw": w_kw.astype(BF16),
        "gb": part(6).astype(BF16), "gc": part(7).astype(BF16), "ci": part(8).astype(BF16),
        "mq": part(9).astype(BF16), "gl": part(10).astype(BF16),
        "ao": w_attn_out[l].astype(BF16), "co": w_conv_out[l].astype(BF16), "mo": w_mem_out[l].astype(BF16),
        "o": w_o[l].astype(BF16),
        "conv": jnp.pad(conv_w[l], ((0, SUBLANES - CONV_WIDTH), (0, 0))),
        "r_hi": wr_hi, "r_lo": (wr - wr_hi.astype(F32)).astype(BF16),
        "r_b": jnp.broadcast_to(b_router[l][:, None], (N_EXPERTS, LANES)),
        "e_gu": w_exp_gu[l].astype(BF16), "e_d": w_exp_down[l].astype(BF16),
        "s_gu": w_sh_gu[l].astype(BF16), "s_d": w_sh_down[l].astype(BF16),
    }


def kernel(x_prompt, x_sample, mem_prompt, cache_attn_k, cache_attn_v, cache_idx_k, cache_mem_k, cache_mem_v,
           state_conv, ln_in_g, ln_in_b, w_in, conv_w, w_mem_kv, w_attn_out, w_conv_out, w_mem_out, w_o,
           ln1_g, ln1_b, w_router, b_router, w_exp_gu, w_exp_down, w_sh_gu, w_sh_down, ln2_g, ln2_b):
    assert DEPTH == 1 and w_in.shape[0] == 1
    l = 0
    bp, n_mem = mem_prompt.shape[0], mem_prompt.shape[1]
    w = _prep_weights(l, w_in, conv_w, w_attn_out, w_conv_out, w_mem_out, w_o, w_router, b_router,
                      w_exp_gu, w_exp_down, w_sh_gu, w_sh_down)
    ln = {"in_g": ln_in_g[None, :], "in_b": ln_in_b[None, :], "g1": ln1_g[l][None, :], "b1": ln1_b[l][None, :],
          "g2": ln2_g[l][None, :], "b2": ln2_b[l][None, :]}

    mem_kv = _project(mem_prompt.reshape(bp * n_mem, D_MODEL), w_mem_kv[l].astype(BF16))
    mem_k = mem_kv[:, :MEM_DIM].reshape(bp, n_mem, MEM_DIM)
    mem_v = mem_kv[:, MEM_DIM:].reshape(bp, n_mem, MEM_DIM)

    zeros = lambda *s: jnp.zeros(s, F32)
    yp, kp, vp, kip, cp = _layer(
        x_prompt, mem_k.astype(BF16), mem_v.astype(BF16),
        zeros(bp, 0, N_KV_HEADS, HEAD_DIM), zeros(bp, 0, N_KV_HEADS, HEAD_DIM), zeros(bp, 0, IDX_DIM),
        zeros(bp, CONV_WIDTH - 1, D_CONV), w, ln, l)
    bs = x_sample.shape[0]
    ys, ks, vs, kis, cs = _layer(
        x_sample, cache_mem_k[l].reshape(bs, n_mem, MEM_DIM).astype(BF16),
        cache_mem_v[l].reshape(bs, n_mem, MEM_DIM).astype(BF16),
        cache_attn_k[l], cache_attn_v[l], cache_idx_k[l], state_conv[l], w, ln, l)

    mem_shape = (1, bp, n_mem, MEM_HEADS, MEM_HEAD_DIM)
    return (yp, ys, kp[None], vp[None], kip[None], mem_k.reshape(mem_shape), mem_v.reshape(mem_shape), cp[None],
            ks[None], vs[None], kis[None], cs[None])
```

```python
import functools

import jax
import jax.numpy as jnp
import numpy as np
from jax import lax
from jax.experimental import pallas as pl
from jax.experimental.pallas import tpu as pltpu

D_MODEL = 1024
DEPTH = 1
CHUNK = 64
HEAD_DIM = 128
N_HEADS = D_MODEL // HEAD_DIM
N_KV_HEADS = max(1, N_HEADS // 4)
KV_GROUP = N_HEADS // N_KV_HEADS
ATTN_DIM = N_HEADS * HEAD_DIM
KV_DIM = N_KV_HEADS * HEAD_DIM
ROT_FRACTION = 4
ROPE_THETA = 500000.0
IDX_HEADS = 8
IDX_DIM = 64
TOPK_MAX = 256
D_CONV = D_MODEL
CONV_WIDTH = 3
N_MEM = 256
MEM_HEADS = 4
MEM_HEAD_DIM = D_MODEL // MEM_HEADS
MEM_DIM = MEM_HEADS * MEM_HEAD_DIM
N_BRANCH = 3
N_EXPERTS = 64
TOP_K = 8
N_GROUPS = 8
GROUP_SIZE = N_EXPERTS // N_GROUPS
TOPK_GROUPS = 4
D_EXPERT = D_MODEL // 4
ROUTED_SCALE = 2.5
LN_EPS = 1e-5
ALPHA = (2.0 * DEPTH) ** 0.25

LANES = 128
SUBLANES = 8
VMEM_LIMIT_BYTES = 56 * 1024 * 1024

Q_TILE = 128
KEY_TILE = 512
HEADS_PER_UNIT = 2
ATTEND_TILES_PER_STEP = 1
COUNT_ROWS = 64
DENOM_ROWS = 16
PROJ_ROWS = 512
MOE_ROWS = 1024
MOE_EXPERTS_PER_STEP = 2

F32 = jnp.float32
BF16 = jnp.bfloat16
I32 = jnp.int32
INT_MIN = -(2 ** 31)
INT_MAX = 2 ** 31 - 1
MASKED_LOGIT = -1e30
QK_SCALE_LOG2E = HEAD_DIM ** -0.5 * float(np.log2(np.e))

_NT = (((1,), (1,)), ((), ()))


def _dot(a, b):
    return jnp.dot(a, b, preferred_element_type=F32)


def _dot_nt(a, b):
    return lax.dot_general(a, b, _NT, preferred_element_type=F32)


def _layer_norm(x, g, b):
    mu = jnp.mean(x, axis=-1, keepdims=True)
    xc = x - mu
    var = jnp.mean(xc * xc, axis=-1, keepdims=True)
    return xc * lax.rsqrt(var + LN_EPS) * g + b


def _sigmoid(x):
    return 1.0 / (1.0 + jnp.exp(-x))


def _resident(shape):
    nd = len(shape)
    return pl.BlockSpec(shape, lambda *_: (0,) * nd, pipeline_mode=pl.Buffered(1))


def _params(*sem):
    return pltpu.CompilerParams(dimension_semantics=sem, vmem_limit_bytes=VMEM_LIMIT_BYTES)


def _rope_tables(pos, half, extra_scale_lanes=None, extra_scale=1.0):
    inv_freq = ROPE_THETA ** (-jnp.arange(half, dtype=F32) / half)
    ang = pos.astype(F32)[:, None] * inv_freq[None, :]
    cos, sin = jnp.cos(ang), jnp.sin(ang)
    n = pos.shape[0]
    c = jnp.concatenate([cos, cos, jnp.ones((n, LANES - 2 * half), F32)], axis=1)
    s = jnp.concatenate([-sin, sin, jnp.zeros((n, LANES - 2 * half), F32)], axis=1)
    if extra_scale_lanes is not None:
        lo, hi = extra_scale_lanes
        lane = jnp.arange(LANES)[None, :]
        c = jnp.where((lane >= lo) & (lane < hi), jnp.float32(extra_scale), c)
    return c, s


def _rope(x, c, s, half):
    rows, width = x.shape
    lane = lax.broadcasted_iota(I32, (rows, LANES), 1)
    first = lane < half
    outs = []
    for j in range(width // LANES):
        xj = x[:, j * LANES:(j + 1) * LANES]
        swap = jnp.where(first, pltpu.roll(xj, LANES - half, 1), pltpu.roll(xj, half, 1))
        outs.append(xj * c + swap * s)
    return outs[0] if len(outs) == 1 else jnp.concatenate(outs, axis=1)


def _proj_kernel(x_ref, w_ref, o_ref):
    o_ref[...] = _dot(x_ref[...].astype(BF16), w_ref[...])


def _project(x, w_bf, col_tile=512):
    n, d = x.shape
    m = w_bf.shape[1]
    return pl.pallas_call(
        _proj_kernel,
        grid=(m // col_tile,),
        in_specs=[pl.BlockSpec((n, d), lambda j: (0, 0)),
                  pl.BlockSpec((d, col_tile), lambda j: (0, j))],
        out_specs=pl.BlockSpec((n, col_tile), lambda j: (0, j)),
        out_shape=jax.ShapeDtypeStruct((n, m), F32),
        compiler_params=_params("arbitrary"),
        name="mem_kv_proj",
    )(x, w_bf)


def _store_head_major(ref, x):
    for r in range(ref.shape[0]):
        for h in range(ref.shape[1]):
            ref[r, h] = x[r * Q_TILE:(r + 1) * Q_TILE, h * LANES:(h + 1) * LANES]


def _qkv_kernel(x_ref, g_ref, b_ref, wq_ref, wqi_ref, wkv_ref, wkw_ref, ch_ref, sh_ref, ci_ref, si_ref,
                q_ref, qi_ref, k_ref, v_ref, kiwi_ref, kbf_ref, kibf_ref):
    xb = _layer_norm(x_ref[...], g_ref[...], b_ref[...]).astype(BF16)
    ch, sh, ci, si = ch_ref[...], sh_ref[...], ci_ref[...], si_ref[...]
    q = _rope(_dot(xb, wq_ref[...]), ch, sh, HEAD_DIM // ROT_FRACTION // 2)
    _store_head_major(q_ref, (q * QK_SCALE_LOG2E).astype(BF16))
    qi = _rope(_dot(xb, wqi_ref[...]), ci, si, IDX_DIM // ROT_FRACTION // 2)
    _store_head_major(qi_ref, qi.astype(BF16))
    kv = _dot(xb, wkv_ref[...])
    k = _rope(kv[:, :KV_DIM], ch, sh, HEAD_DIM // ROT_FRACTION // 2)
    k_ref[...] = k
    kbf_ref[...] = k.astype(BF16)
    v_ref[...] = kv[:, KV_DIM:]
    kw = _rope(_dot(xb, wkw_ref[...]), ci, si, IDX_DIM // ROT_FRACTION // 2)
    kiwi_ref[...] = kw
    kibf_ref[...] = kw.astype(BF16)


def _qkv_project(x, ln_g, ln_b, w, tabs, rows, seq):
    n = x.shape[0]
    tm = min(rows, n)
    tab_blocks = max(seq, tm) // tm
    row = lambda width: pl.BlockSpec((tm, width), lambda i: (i, 0))
    tab = pl.BlockSpec((tm, LANES), lambda i: (i % tab_blocks, 0))
    hm_shape = lambda heads: (n // Q_TILE, heads, Q_TILE, LANES)
    hm = lambda heads: pl.BlockSpec((tm // Q_TILE, heads, Q_TILE, LANES), lambda i: (i, 0, 0, 0))
    out_shapes = (
        jax.ShapeDtypeStruct(hm_shape(N_HEADS), BF16),
        jax.ShapeDtypeStruct(hm_shape(IDX_HEADS), BF16),
        jax.ShapeDtypeStruct((n, KV_DIM), F32),
        jax.ShapeDtypeStruct((n, KV_DIM), F32),
        jax.ShapeDtypeStruct((n, LANES), F32),
        jax.ShapeDtypeStruct((n, KV_DIM), BF16),
        jax.ShapeDtypeStruct((n, LANES), BF16),
    )
    return pl.pallas_call(
        _qkv_kernel,
        grid=(n // tm,),
        in_specs=[row(D_MODEL), _resident((1, D_MODEL)), _resident((1, D_MODEL)),
                  _resident(w["q"].shape), _resident(w["qi"].shape), _resident(w["kv"].shape),
                  _resident(w["kw"].shape), tab, tab, tab, tab],
        out_specs=(hm(N_HEADS), hm(IDX_HEADS), row(KV_DIM), row(KV_DIM), row(LANES),
                   row(KV_DIM), row(LANES)),
        out_shape=out_shapes,
        compiler_params=_params("arbitrary"),
        name="qkv_proj",
    )(x, ln_g, ln_b, w["q"], w["qi"], w["kv"], w["kw"], *tabs)


def _sortable_key(score):
    bits = lax.bitcast_convert_type(score + 0.0, I32)
    return bits ^ ((bits >> 31) & INT_MAX)


def _dsa_kernel(q_ref, qi_ref, kiwi_ref, k_ref, v_ref, ki_ref, o_ref,
                vt_ref, key_ref, acc_ref, m_ref, j_ref,
                *, past, seq, n_keys, topk, n_key_tiles, interleaved):
    i = pl.program_id(1)
    tq, kt = Q_TILE, KEY_TILE
    uq = HEADS_PER_UNIT * tq

    def kv_tile(ref, j, n):
        if interleaved:
            return ref[0, pl.ds(j * (kt * N_KV_HEADS) + n, kt, stride=N_KV_HEADS), :]
        return ref[0, pl.ds(pl.multiple_of(j * kt, kt), kt), n * HEAD_DIM:(n + 1) * HEAD_DIM]

    @pl.when(i == 0)
    def _():
        ones = jnp.ones((DENOM_ROWS, kt), BF16)
        for c in range(n_key_tiles):
            for n in range(N_KV_HEADS):
                vt_ref[c, n, :HEAD_DIM, :] = kv_tile(v_ref, c, n).T.astype(BF16)
                vt_ref[c, n, HEAD_DIM:, :] = ones

    lane = lax.broadcasted_iota(I32, (1, tq), 1)
    t = i * tq + lane
    q_real = t < seq
    lim = jnp.minimum(((past + t) // CHUNK + 1) * CHUNK, n_keys)
    lim_max = jnp.minimum(((past + i * tq + tq - 1) // CHUNK + 1) * CHUNK, n_keys)
    n_tiles = (lim_max + kt - 1) // kt
    row = lax.broadcasted_iota(I32, (kt, tq), 0)

    wi = kiwi_ref[0].T[IDX_DIM:IDX_DIM + IDX_HEADS, :] * (IDX_DIM ** -0.5)

    def score_tile(j, n_nonneg):
        kib = ki_ref[0, pl.ds(pl.multiple_of(j * kt, kt), kt), :]
        rel_of = lambda u: _dot_nt(kib, qi_ref[0, 0, u * uq:(u + 1) * uq, :])
        n_units = IDX_HEADS // HEADS_PER_UNIT
        acc = jnp.zeros((kt, tq), F32)
        rel_next = rel_of(0)
        for u in range(n_units):
            rel, rel_next = rel_next, (rel_of(u + 1) if u + 1 < n_units else None)
            for g in range(HEADS_PER_UNIT):
                h = u * HEADS_PER_UNIT + g
                acc = acc + jnp.maximum(rel[:, g * tq:(g + 1) * tq], 0.0) * wi[h:h + 1, :]
        score = jnp.where(j * kt + row < lim, acc, -jnp.inf)
        key = _sortable_key(score)
        key_ref[j] = key
        for r in range(kt // COUNT_ROWS):
            n_nonneg = jnp.where(key[r * COUNT_ROWS:(r + 1) * COUNT_ROWS] >= 0, n_nonneg + 1, n_nonneg)
        return n_nonneg

    n_nonneg = lax.fori_loop(0, n_tiles, score_tile, jnp.zeros((COUNT_ROWS, tq), I32))

    def count(pred):
        def body(j, acc):
            blk = key_ref[j]
            for r in range(kt // COUNT_ROWS):
                rows = slice(r * COUNT_ROWS, (r + 1) * COUNT_ROWS)
                acc = jnp.where(pred(blk[rows], j * kt + row[rows]), acc + 1, acc)
            return acc
        acc = lax.fori_loop(0, n_tiles, body, jnp.zeros((COUNT_ROWS, tq), I32))
        return jnp.sum(acc, axis=0, keepdims=True)

    def count_ge(cand):
        return count(lambda keys, kidx: keys >= cand)

    def bisect(b, state):
        thr, cnt = state
        cand = thr | jnp.left_shift(jnp.int32(1), 30 - b)
        c = count_ge(cand)
        take = c >= topk
        return jnp.where(take, cand, thr), jnp.where(take, c, cnt)

    c0 = jnp.sum(n_nonneg, axis=0, keepdims=True)
    nonneg = c0 >= topk
    thr, cnt = lax.fori_loop(0, 31, bisect, (jnp.where(nonneg, 0, INT_MIN).astype(I32),
                                             jnp.where(nonneg, c0, n_tiles * kt).astype(I32)))
    tied = jnp.max(jnp.where(q_real, cnt, topk)) > topk

    j_ref[...] = jnp.full((1, tq), INT_MAX, I32)

    @pl.when(tied)
    def _():
        n_ties_wanted = topk - count(lambda keys, kidx: keys > thr)
        n_bits = int(n_key_tiles * kt).bit_length()

        def bisect_idx(it, cut):
            cand = cut | jnp.left_shift(jnp.int32(1), n_bits - 1 - it)
            c = count(lambda keys, kidx: (keys == thr) & (kidx < cand))
            return jnp.where(c <= n_ties_wanted, cand, cut)

        j_ref[...] = lax.fori_loop(0, n_bits, bisect_idx, jnp.zeros((1, tq), I32))

    j_cut = j_ref[...]

    m_ref[...] = jnp.full(m_ref.shape, MASKED_LOGIT, F32)
    acc_ref[...] = jnp.zeros(acc_ref.shape, F32)

    def attend_tiles(step, carry):
        n_units = N_HEADS // HEADS_PER_UNIT
        kv_of = lambda u: u * HEADS_PER_UNIT // KV_GROUP
        tiles = []
        for t in range(ATTEND_TILES_PER_STEP):
            j = step * ATTEND_TILES_PER_STEP + t
            jc = jnp.minimum(j, n_key_tiles - 1)
            keys = key_ref[jc]
            kidx = jc * kt + row
            sel = ((keys > thr) | ((keys == thr) & (kidx < j_cut))) & (kidx < lim) & (j < n_tiles)
            tiles.append((jc, sel))
        items = [(t, u) for t in range(ATTEND_TILES_PER_STEP) for u in range(n_units)]

        def logits(item):
            (jc, _), u = tiles[item[0]], item[1]
            kb = kv_tile(k_ref, jc, kv_of(u)).astype(BF16)
            return _dot_nt(kb, q_ref[0, 0, u * uq:(u + 1) * uq, :])

        s_next = logits(items[0])
        for k, (t, u) in enumerate(items):
            jc, sel = tiles[t]
            s, s_next = s_next, (logits(items[k + 1]) if k + 1 < len(items) else None)
            s = jnp.concatenate([jnp.where(sel, s[:, g * tq:(g + 1) * tq], MASKED_LOGIT)
                                 for g in range(HEADS_PER_UNIT)], axis=1)
            m_old = m_ref[u]
            m_new = jnp.maximum(m_old, jnp.max(s, axis=0, keepdims=True))
            p = jnp.exp2(s - m_new).astype(BF16)
            acc_ref[u] = jnp.exp2(m_old - m_new) * acc_ref[u] + _dot(vt_ref[jc, kv_of(u)], p)
            m_ref[u] = m_new
        return carry

    lax.fori_loop(0, (n_tiles + ATTEND_TILES_PER_STEP - 1) // ATTEND_TILES_PER_STEP, attend_tiles, 0)

    for h in range(N_HEADS):
        u, cols = h // HEADS_PER_UNIT, slice((h % HEADS_PER_UNIT) * tq, (h % HEADS_PER_UNIT + 1) * tq)
        o = acc_ref[u, :HEAD_DIM, cols] / acc_ref[u, HEAD_DIM:HEAD_DIM + 1, cols]
        o_ref[0, :, h * HEAD_DIM:(h + 1) * HEAD_DIM] = o.T.astype(BF16)


def _dsa_attention(q, qi, kiwi, k, v, ki_bf, *, past, seq, n_keys, interleaved):
    b, tq_pad, _ = kiwi.shape
    l_pad = ki_bf.shape[1]
    n_key_tiles = l_pad // KEY_TILE
    topk = min(TOPK_MAX, n_keys // 4)
    qblk = lambda width: pl.BlockSpec((1, Q_TILE, width), lambda bi, i: (bi, i, 0))
    hmblk = lambda heads: pl.BlockSpec((1, 1, heads * Q_TILE, LANES), lambda bi, i: (bi, i, 0, 0))
    kblk = lambda rows, width: pl.BlockSpec((1, rows, width), lambda bi, i: (bi, 0, 0),
                                            pipeline_mode=pl.Buffered(1))
    kvblk = kblk(l_pad * N_KV_HEADS, HEAD_DIM) if interleaved else kblk(l_pad, KV_DIM)
    kern = functools.partial(_dsa_kernel, past=past, seq=seq, n_keys=n_keys, topk=topk,
                             n_key_tiles=n_key_tiles, interleaved=interleaved)
    return pl.pallas_call(
        kern,
        grid=(b, tq_pad // Q_TILE),
        in_specs=[hmblk(N_HEADS), hmblk(IDX_HEADS), qblk(LANES),
                  kvblk, kvblk, kblk(l_pad, LANES)],
        out_specs=qblk(ATTN_DIM),
        out_shape=jax.ShapeDtypeStruct((b, tq_pad, ATTN_DIM), BF16),
        scratch_shapes=[
            pltpu.VMEM((n_key_tiles, N_KV_HEADS, HEAD_DIM + DENOM_ROWS, KEY_TILE), BF16),
            pltpu.VMEM((n_key_tiles, KEY_TILE, Q_TILE), I32),
            pltpu.VMEM((N_HEADS // HEADS_PER_UNIT, HEAD_DIM + DENOM_ROWS, HEADS_PER_UNIT * Q_TILE), F32),
            pltpu.VMEM((N_HEADS // HEADS_PER_UNIT, 1, HEADS_PER_UNIT * Q_TILE), F32),
            pltpu.VMEM((1, Q_TILE), I32),
        ],
        compiler_params=_params("arbitrary", "arbitrary"),
        name="dsa_attention",
    )(q, qi, kiwi, k, v, ki_bf)


def _mixer_kernel(x_ref, attn_ref, memk_ref, memv_ref, prev_a_ref, prev_b_ref,
                  lng_ref, lnb_ref, cw_ref, ln1g_ref, ln1b_ref,
                  wgb_ref, wgc_ref, wci_ref, wmq_ref, wgl_ref, wao_ref, wco_ref, wmo_ref, wo_ref,
                  x1_ref, state_ref, carry_ref, *, seq, rows):
    i = pl.program_id(0)
    multi_seq = seq < rows
    xn = _layer_norm(x_ref[...], lng_ref[...], lnb_ref[...])
    xb = xn.astype(BF16)

    u = _dot(xb, wgc_ref[...]) * _dot(xb, wci_ref[...])
    ridx = lax.broadcasted_iota(I32, u.shape, 0)
    if multi_seq:
        tpos = ridx & (seq - 1)
        s1 = jnp.where(tpos == 0, prev_a_ref[...], pltpu.roll(u, 1, 0))
        s2 = jnp.where(tpos < 2, prev_b_ref[...], pltpu.roll(u, 2, 0))
        state_ref[...] = u
    else:
        blocks_per_seq = seq // rows
        @pl.when(i % blocks_per_seq == 0)
        def _():
            carry_ref[...] = prev_a_ref[0]

        prev = carry_ref[...]
        p1, p2 = prev[SUBLANES - 1:SUBLANES, :], prev[SUBLANES - 2:SUBLANES - 1, :]
        s1 = jnp.where(ridx == 0, p1, pltpu.roll(u, 1, 0))
        s2 = jnp.where(ridx == 0, p2, jnp.where(ridx == 1, p1, pltpu.roll(u, 2, 0)))
        carry_ref[...] = u[rows - SUBLANES:, :]
        state_ref[0] = u[rows - SUBLANES:, :]
    cw = cw_ref[...]
    conv = cw[0:1, :] * s2 + cw[1:2, :] * s1 + cw[2:3, :] * u
    conv_in = (_dot(xb, wgb_ref[...]) * conv).astype(BF16)

    mq = _dot(xb, wmq_ref[...]).astype(BF16)
    n_batches = max(1, rows // seq)
    rb = rows // n_batches
    mem_rows = []
    for bi in range(n_batches):
        heads = []
        for h in range(MEM_HEADS):
            cols = slice(h * MEM_HEAD_DIM, (h + 1) * MEM_HEAD_DIM)
            s = _dot_nt(mq[bi * rb:(bi + 1) * rb, cols], memk_ref[bi, :, cols]) * (MEM_HEAD_DIM ** -0.5)
            e = jnp.exp(s - jnp.max(s, axis=-1, keepdims=True))
            p = e / jnp.sum(e, axis=-1, keepdims=True)
            heads.append(_dot(p.astype(BF16), memv_ref[bi, :, cols]))
        mem_rows.append(jnp.concatenate(heads, axis=1))
    mem = (mem_rows[0] if n_batches == 1 else jnp.concatenate(mem_rows, axis=0)).astype(BF16)

    gate = _sigmoid(_dot(xb, wgl_ref[...]))
    mixed = (gate[:, :D_MODEL] * _dot(attn_ref[...], wao_ref[...])
             + gate[:, D_MODEL:2 * D_MODEL] * _dot(conv_in, wco_ref[...])
             + gate[:, 2 * D_MODEL:] * _dot(mem, wmo_ref[...]))
    y = ALPHA * xn + _dot(mixed.astype(BF16), wo_ref[...])
    x1_ref[...] = _layer_norm(y, ln1g_ref[...], ln1b_ref[...])


def _mixer(x, attn, mem_k, mem_v, prev_a, prev_b, ln_g, ln_b, conv_w8, ln1_g, ln1_b, w, *, seq, rows):
    n = x.shape[0]
    tm = min(rows, n)
    multi_seq = seq < tm
    n_batches = max(1, tm // seq)
    row = lambda width: pl.BlockSpec((tm, width), lambda i: (i, 0))
    mem = pl.BlockSpec((n_batches, N_MEM, MEM_DIM), lambda i: ((i * tm) // (seq * n_batches), 0, 0))
    vec = _resident((1, D_MODEL))
    if multi_seq:
        prev_spec = row(D_CONV)
        state_spec = row(D_CONV)
        state_shape = jax.ShapeDtypeStruct((n, D_CONV), F32)
    else:
        prev_spec = pl.BlockSpec((1, SUBLANES, D_CONV), lambda i: ((i * tm) // seq, 0, 0))
        state_spec = prev_spec
        state_shape = jax.ShapeDtypeStruct((n // seq, SUBLANES, D_CONV), F32)
    weights = [w[k] for k in ("gb", "gc", "ci", "mq", "gl", "ao", "co", "mo", "o")]
    kern = functools.partial(_mixer_kernel, seq=seq, rows=tm)
    return pl.pallas_call(
        kern,
        grid=(n // tm,),
        in_specs=[row(D_MODEL), row(ATTN_DIM), mem, mem, prev_spec, prev_spec,
                  vec, vec, _resident((SUBLANES, D_CONV)), vec, vec] + [_resident(m.shape) for m in weights],
        out_specs=(row(D_MODEL), state_spec),
        out_shape=(jax.ShapeDtypeStruct((n, D_MODEL), F32), state_shape),
        scratch_shapes=[pltpu.VMEM((SUBLANES, D_CONV), F32)],
        compiler_params=_params("arbitrary"),
        name="mixer",
    )(x, attn, mem_k, mem_v, prev_a, prev_b, ln_g, ln_b, conv_w8, ln1_g, ln1_b, *weights)


def _route(x, wr_hi, wr_lo, bias):
    tm = x.shape[0]
    x_hi = x.astype(BF16)
    x_lo = (x - x_hi.astype(F32)).astype(BF16)
    logits = _dot_nt(wr_hi, x_hi) + (_dot_nt(wr_hi, x_lo) + _dot_nt(wr_lo, x_hi))
    s = _sigmoid(logits)
    sc = s + bias[:, 0:1]
    sub = lax.broadcasted_iota(I32, (GROUP_SIZE, tm), 0)
    neg = -jnp.inf
    groups, gscore = [], []
    for g in range(N_GROUPS):
        xg = sc[g * GROUP_SIZE:(g + 1) * GROUP_SIZE, :]
        m1 = jnp.max(xg, axis=0, keepdims=True)
        i1 = jnp.min(jnp.where(xg == m1, sub, GROUP_SIZE), axis=0, keepdims=True)
        m2 = jnp.max(jnp.where(sub == i1, neg, xg), axis=0, keepdims=True)
        groups.append(xg)
        gscore.append(m1 + m2)
    masked = []
    for g in range(N_GROUPS):
        rank = jnp.zeros((1, tm), I32)
        for o in range(N_GROUPS):
            if o == g:
                continue
            beats = (gscore[o] > gscore[g]) | ((gscore[o] == gscore[g]) & (o < g))
            rank = rank + beats.astype(I32)
        masked.append(jnp.where(rank < TOPK_GROUPS, groups[g], neg))
    v = jnp.concatenate(masked, axis=0)
    eidx = lax.broadcasted_iota(I32, (N_EXPERTS, tm), 0)
    chosen = jnp.zeros((N_EXPERTS, tm), jnp.bool_)
    for _ in range(TOP_K):
        m = jnp.max(v, axis=0, keepdims=True)
        first = jnp.min(jnp.where(v == m, eidx, N_EXPERTS), axis=0, keepdims=True)
        hit = eidx == first
        chosen = chosen | hit
        v = jnp.where(hit, neg, v)
    wsel = jnp.where(chosen, s, 0.0)
    return wsel / jnp.sum(wsel, axis=0, keepdims=True) * ROUTED_SCALE


def _expert_ffn(xb, w_gu, w_down, gate_col=None):
    gu = _dot(xb, w_gu)
    g, u = gu[:, :D_EXPERT], gu[:, D_EXPERT:]
    h = g * _sigmoid(g) * u
    if gate_col is not None:
        h = h * gate_col
    return _dot(h.astype(BF16), w_down)


def _moe_kernel(x_ref, wrh_ref, wrl_ref, br_ref, wgu_ref, wd_ref, wsgu_ref, wsd_ref, g_ref, b_ref,
                o_ref, xb_ref, gate_ref, acc_ref, *, experts_per_step):
    j = pl.program_id(1)

    @pl.when(j == 0)
    def _():
        x = x_ref[...]
        gate_t = _route(x, wrh_ref[...], wrl_ref[...], br_ref[...])
        pad = jnp.zeros((LANES - N_EXPERTS, x.shape[0]), F32)
        gate_ref[...] = jnp.concatenate([gate_t, pad], axis=0).T
        xb = x.astype(BF16)
        xb_ref[...] = xb
        acc_ref[...] = _expert_ffn(xb, wsgu_ref[...], wsd_ref[...])

    xb = xb_ref[...]
    gate = gate_ref[...]
    lane = lax.broadcasted_iota(I32, gate.shape, 1)
    acc = acc_ref[...]
    for e in range(experts_per_step):
        col = jnp.sum(jnp.where(lane == j * experts_per_step + e, gate, 0.0), axis=1, keepdims=True)
        acc = acc + _expert_ffn(xb, wgu_ref[e].astype(BF16), wd_ref[e].astype(BF16), col)
    acc_ref[...] = acc

    @pl.when(j == pl.num_programs(1) - 1)
    def _():
        o_ref[...] = _layer_norm(ALPHA * x_ref[...] + acc_ref[...], g_ref[...], b_ref[...])


def _moe(x, w, ln_g, ln_b, rows):
    n = x.shape[0]
    tm = min(rows, n)
    eps = MOE_EXPERTS_PER_STEP
    row = pl.BlockSpec((tm, D_MODEL), lambda i, j: (i, 0))
    kern = functools.partial(_moe_kernel, experts_per_step=eps)
    return pl.pallas_call(
        kern,
        grid=(n // tm, N_EXPERTS // eps),
        in_specs=[row, _resident(w["r_hi"].shape), _resident(w["r_lo"].shape), _resident(w["r_b"].shape),
                  pl.BlockSpec((eps, D_MODEL, 2 * D_EXPERT), lambda i, j: (j, 0, 0)),
                  pl.BlockSpec((eps, D_EXPERT, D_MODEL), lambda i, j: (j, 0, 0)),
                  _resident(w["s_gu"].shape), _resident(w["s_d"].shape),
                  _resident((1, D_MODEL)), _resident((1, D_MODEL))],
        out_specs=row,
        out_shape=jax.ShapeDtypeStruct((n, D_MODEL), F32),
        scratch_shapes=[pltpu.VMEM((tm, D_MODEL), BF16), pltpu.VMEM((tm, LANES), F32),
                        pltpu.VMEM((tm, D_MODEL), F32)],
        compiler_params=_params("arbitrary", "arbitrary"),
        name="moe",
    )(x, w["r_hi"], w["r_lo"], w["r_b"], w["e_gu"], w["e_d"], w["s_gu"], w["s_d"], ln_g, ln_b)


def _pad_rows(a, rows):
    return a if a.shape[1] == rows else jnp.pad(a, ((0, 0), (0, rows - a.shape[1]), (0, 0)))


def _head_major_per_batch(x, b, seq, tq_pad):
    heads = x.shape[1]
    if seq % Q_TILE == 0:
        return x.reshape(b, seq // Q_TILE, heads * Q_TILE, LANES)
    per_tile = Q_TILE // seq
    x = x.reshape(-1, heads, per_tile, seq, LANES).transpose(0, 2, 1, 3, 4).reshape(b, heads, seq, LANES)
    x = jnp.pad(x, ((0, 0), (0, 0), (0, tq_pad - seq), (0, 0)))
    return x.reshape(b, 1, heads * tq_pad, LANES)


def _layer(x, mem_k_bf, mem_v_bf, past_k, past_v, past_ki, conv_prev, w, ln, layer):
    b, seq, _ = x.shape
    past = past_k.shape[1]
    n = b * seq
    n_keys = past + seq
    x2 = x.reshape(n, D_MODEL)

    pos = past + jnp.arange(seq, dtype=I32)
    reps = max(1, min(PROJ_ROWS, n) // seq)
    pos_t = jnp.tile(pos, reps)
    tabs = (_rope_tables(pos_t, HEAD_DIM // ROT_FRACTION // 2)
            + _rope_tables(pos_t, IDX_DIM // ROT_FRACTION // 2,
                           (IDX_DIM, IDX_DIM + IDX_HEADS), IDX_HEADS ** -0.5))
    q, qi, k, v, kiwi, k_bf, ki_bf = _qkv_project(x2, ln["in_g"], ln["in_b"], w, tabs, PROJ_ROWS, seq)

    l_pad = -(-n_keys // KEY_TILE) * KEY_TILE
    tq_pad = -(-seq // Q_TILE) * Q_TILE
    interleaved = past > 0
    if interleaved:
        rows = lambda a, length: a.reshape(b, length * N_KV_HEADS, HEAD_DIM)
        k3 = _pad_rows(jnp.concatenate([rows(past_k, past), rows(k, seq)], axis=1), l_pad * N_KV_HEADS)
        v3 = _pad_rows(jnp.concatenate([rows(past_v, past), rows(v, seq)], axis=1), l_pad * N_KV_HEADS)
    else:
        k3 = _pad_rows(k_bf.reshape(b, seq, KV_DIM), l_pad)
        v3 = _pad_rows(v.reshape(b, seq, KV_DIM), l_pad)
    ki_past = jnp.pad(past_ki, ((0, 0), (0, 0), (0, LANES - IDX_DIM))).astype(BF16)
    ki3 = jnp.concatenate([ki_past, ki_bf.reshape(b, seq, LANES)], axis=1)
    attn = _dsa_attention(
        _head_major_per_batch(q, b, seq, tq_pad), _head_major_per_batch(qi, b, seq, tq_pad),
        _pad_rows(kiwi.reshape(b, seq, LANES), tq_pad), k3, v3, _pad_rows(ki3, l_pad),
        past=past, seq=seq, n_keys=n_keys, interleaved=interleaved)
    attn = attn[:, :seq].reshape(n, ATTN_DIM)

    tm = min(PROJ_ROWS, n)
    if seq < tm:
        tpos = jnp.arange(seq)[None, :, None]
        prev_a = jnp.where(tpos == 0, conv_prev[:, 1:2], 0.0).reshape(n, D_CONV)
        prev_b = jnp.where(tpos == 0, conv_prev[:, 0:1], jnp.where(tpos == 1, conv_prev[:, 1:2], 0.0)).reshape(n, D_CONV)
    else:
        prev_a = jnp.pad(conv_prev, ((0, 0), (SUBLANES - (CONV_WIDTH - 1), 0), (0, 0)))
        prev_b = prev_a
    x1, state = _mixer(x2, attn, mem_k_bf, mem_v_bf, prev_a, prev_b, ln["in_g"], ln["in_b"], w["conv"],
                       ln["g1"], ln["b1"], w, seq=seq, rows=PROJ_ROWS)
    if seq < tm:
        new_conv = state.reshape(b, seq, D_CONV)[:, seq - (CONV_WIDTH - 1):]
    else:
        new_conv = state[:, SUBLANES - (CONV_WIDTH - 1):]

    y = _moe(x1, w, ln["g2"], ln["b2"], MOE_ROWS)
    return (y.reshape(b, seq, D_MODEL), k.reshape(b, seq, N_KV_HEADS, HEAD_DIM),
            v.reshape(b, seq, N_KV_HEADS, HEAD_DIM), kiwi[:, :IDX_DIM].reshape(b, seq, IDX_DIM), new_conv)


def _prep_weights(l, w_in, conv_w, w_attn_out, w_conv_out, w_mem_out, w_o, w_router, b_router,
                  w_exp_gu, w_exp_down, w_sh_gu, w_sh_down):
    sizes = [ATTN_DIM, KV_DIM, KV_DIM, IDX_HEADS * IDX_DIM, IDX_DIM, IDX_HEADS,
             D_CONV, D_CONV, D_CONV, MEM_DIM, N_BRANCH * D_MODEL]
    offs = np.concatenate([[0], np.cumsum(sizes)]).tolist()
    wi = w_in[l]
    part = lambda j: wi[:, offs[j]:offs[j + 1]]
    w_qi = part(3).reshape(D_MODEL, IDX_HEADS, IDX_DIM)
    w_qi = jnp.pad(w_qi, ((0, 0), (0, 0), (0, LANES - IDX_DIM))).reshape(D_MODEL, IDX_HEADS * LANES)
    w_kw = jnp.pad(jnp.concatenate([part(4), part(5)], axis=1), ((0, 0), (0, LANES - IDX_DIM - IDX_HEADS)))
    wr = w_router[l].T
    wr_hi = wr.astype(BF16)
    return {
        "q": part(0).astype(BF16), "kv": jnp.concatenate([part(1), part(2)], axis=1).astype(BF16),
        "qi": w_qi.astype(BF16), "kw": w_kw.astype(BF16),
        "gb": part(6).astype(BF16), "gc": part(7).astype(BF16), "ci": part(8).astype(BF16),
        "mq": part(9).astype(BF16), "gl": part(10).astype(BF16),
        "ao": w_attn_out[l].astype(BF16), "co": w_conv_out[l].astype(BF16), "mo": w_mem_out[l].astype(BF16),
        "o": w_o[l].astype(BF16),
        "conv": jnp.pad(conv_w[l], ((0, SUBLANES - CONV_WIDTH), (0, 0))),
        "r_hi": wr_hi, "r_lo": (wr - wr_hi.astype(F32)).astype(BF16),
        "r_b": jnp.broadcast_to(b_router[l][:, None], (N_EXPERTS, LANES)),
        "e_gu": w_exp_gu[l], "e_d": w_exp_down[l],
        "s_gu": w_sh_gu[l].astype(BF16), "s_d": w_sh_down[l].astype(BF16),
    }


def kernel(x_prompt, x_sample, mem_prompt, cache_attn_k, cache_attn_v, cache_idx_k, cache_mem_k, cache_mem_v,
           state_conv, ln_in_g, ln_in_b, w_in, conv_w, w_mem_kv, w_attn_out, w_conv_out, w_mem_out, w_o,
           ln1_g, ln1_b, w_router, b_router, w_exp_gu, w_exp_down, w_sh_gu, w_sh_down, ln2_g, ln2_b):
    assert DEPTH == 1 and w_in.shape[0] == 1
    l = 0
    bp, n_mem = mem_prompt.shape[0], mem_prompt.shape[1]
    w = _prep_weights(l, w_in, conv_w, w_attn_out, w_conv_out, w_mem_out, w_o, w_router, b_router,
                      w_exp_gu, w_exp_down, w_sh_gu, w_sh_down)
    ln = {"in_g": ln_in_g[None, :], "in_b": ln_in_b[None, :], "g1": ln1_g[l][None, :], "b1": ln1_b[l][None, :],
          "g2": ln2_g[l][None, :], "b2": ln2_b[l][None, :]}

    mem_kv = _project(mem_prompt.reshape(bp * n_mem, D_MODEL), w_mem_kv[l].astype(BF16))
    mem_k = mem_kv[:, :MEM_DIM].reshape(bp, n_mem, MEM_DIM)
    mem_v = mem_kv[:, MEM_DIM:].reshape(bp, n_mem, MEM_DIM)

    zeros = lambda *s: jnp.zeros(s, F32)
    yp, kp, vp, kip, cp = _layer(
        x_prompt, mem_k.astype(BF16), mem_v.astype(BF16),
        zeros(bp, 0, N_KV_HEADS, HEAD_DIM), zeros(bp, 0, N_KV_HEADS, HEAD_DIM), zeros(bp, 0, IDX_DIM),
        zeros(bp, CONV_WIDTH - 1, D_CONV), w, ln, l)
    bs = x_sample.shape[0]
    ys, ks, vs, kis, cs = _layer(
        x_sample, cache_mem_k[l].reshape(bs, n_mem, MEM_DIM).astype(BF16),
        cache_mem_v[l].reshape(bs, n_mem, MEM_DIM).astype(BF16),
        cache_attn_k[l], cache_attn_v[l], cache_idx_k[l], state_conv[l], w, ln, l)

    mem_shape = (1, bp, n_mem, MEM_HEADS, MEM_HEAD_DIM)
    return (yp, ys, kp[None], vp[None], kip[None], mem_k.reshape(mem_shape), mem_v.reshape(mem_shape), cp[None],
            ks[None], vs[None], kis[None], cs[None])
```

```python
import functools

import jax
import jax.numpy as jnp
import numpy as np
from jax import lax
from jax.experimental import pallas as pl
from jax.experimental.pallas import tpu as pltpu

D_MODEL = 1024
DEPTH = 1
CHUNK = 64
HEAD_DIM = 128
N_HEADS = D_MODEL // HEAD_DIM
N_KV_HEADS = max(1, N_HEADS // 4)
KV_GROUP = N_HEADS // N_KV_HEADS
ATTN_DIM = N_HEADS * HEAD_DIM
KV_DIM = N_KV_HEADS * HEAD_DIM
ROT_FRACTION = 4
ROPE_THETA = 500000.0
IDX_HEADS = 8
IDX_DIM = 64
TOPK_MAX = 256
D_CONV = D_MODEL
CONV_WIDTH = 3
N_MEM = 256
MEM_HEADS = 4
MEM_HEAD_DIM = D_MODEL // MEM_HEADS
MEM_DIM = MEM_HEADS * MEM_HEAD_DIM
N_BRANCH = 3
N_EXPERTS = 64
TOP_K = 8
N_GROUPS = 8
GROUP_SIZE = N_EXPERTS // N_GROUPS
TOPK_GROUPS = 4
D_EXPERT = D_MODEL // 4
ROUTED_SCALE = 2.5
LN_EPS = 1e-5
ALPHA = (2.0 * DEPTH) ** 0.25

LANES = 128
SUBLANES = 8
VMEM_LIMIT_BYTES = 56 * 1024 * 1024

Q_TILE = 128
KEY_TILE = 512
HEADS_PER_UNIT = 2
ATTEND_TILES_PER_STEP = 1
COUNT_ROWS = 64
DENOM_ROWS = 16
PROJ_ROWS = 512
MOE_ROWS = 1024
MOE_EXPERTS_PER_STEP = 4

F32 = jnp.float32
BF16 = jnp.bfloat16
I32 = jnp.int32
INT_MIN = -(2 ** 31)
INT_MAX = 2 ** 31 - 1
MASKED_LOGIT = -1e30
QK_SCALE_LOG2E = HEAD_DIM ** -0.5 * float(np.log2(np.e))

_NT = (((1,), (1,)), ((), ()))


def _dot(a, b):
    return jnp.dot(a, b, preferred_element_type=F32)


def _dot_nt(a, b):
    return lax.dot_general(a, b, _NT, preferred_element_type=F32)


def _layer_norm(x, g, b):
    mu = jnp.mean(x, axis=-1, keepdims=True)
    xc = x - mu
    var = jnp.mean(xc * xc, axis=-1, keepdims=True)
    return xc * lax.rsqrt(var + LN_EPS) * g + b


def _sigmoid(x):
    return 1.0 / (1.0 + jnp.exp(-x))


def _resident(shape):
    nd = len(shape)
    return pl.BlockSpec(shape, lambda *_: (0,) * nd, pipeline_mode=pl.Buffered(1))


def _params(*sem):
    return pltpu.CompilerParams(dimension_semantics=sem, vmem_limit_bytes=VMEM_LIMIT_BYTES)


def _rope_tables(pos, half, extra_scale_lanes=None, extra_scale=1.0):
    inv_freq = ROPE_THETA ** (-jnp.arange(half, dtype=F32) / half)
    ang = pos.astype(F32)[:, None] * inv_freq[None, :]
    cos, sin = jnp.cos(ang), jnp.sin(ang)
    n = pos.shape[0]
    c = jnp.concatenate([cos, cos, jnp.ones((n, LANES - 2 * half), F32)], axis=1)
    s = jnp.concatenate([-sin, sin, jnp.zeros((n, LANES - 2 * half), F32)], axis=1)
    if extra_scale_lanes is not None:
        lo, hi = extra_scale_lanes
        lane = jnp.arange(LANES)[None, :]
        c = jnp.where((lane >= lo) & (lane < hi), jnp.float32(extra_scale), c)
    return c, s


def _rope(x, c, s, half):
    rows, width = x.shape
    lane = lax.broadcasted_iota(I32, (rows, LANES), 1)
    first = lane < half
    outs = []
    for j in range(width // LANES):
        xj = x[:, j * LANES:(j + 1) * LANES]
        swap = jnp.where(first, pltpu.roll(xj, LANES - half, 1), pltpu.roll(xj, half, 1))
        outs.append(xj * c + swap * s)
    return outs[0] if len(outs) == 1 else jnp.concatenate(outs, axis=1)


def _proj_kernel(x_ref, w_ref, o_ref):
    o_ref[...] = _dot(x_ref[...].astype(BF16), w_ref[...])


def _project(x, w_bf, col_tile=512):
    n, d = x.shape
    m = w_bf.shape[1]
    return pl.pallas_call(
        _proj_kernel,
        grid=(m // col_tile,),
        in_specs=[pl.BlockSpec((n, d), lambda j: (0, 0)),
                  pl.BlockSpec((d, col_tile), lambda j: (0, j))],
        out_specs=pl.BlockSpec((n, col_tile), lambda j: (0, j)),
        out_shape=jax.ShapeDtypeStruct((n, m), F32),
        compiler_params=_params("arbitrary"),
        name="mem_kv_proj",
    )(x, w_bf)


def _store_head_major(ref, x):
    for r in range(ref.shape[0]):
        for h in range(ref.shape[1]):
            ref[r, h] = x[r * Q_TILE:(r + 1) * Q_TILE, h * LANES:(h + 1) * LANES]


def _qkv_kernel(x_ref, g_ref, b_ref, wq_ref, wqi_ref, wkv_ref, wkw_ref, ch_ref, sh_ref, ci_ref, si_ref,
                q_ref, qi_ref, k_ref, v_ref, kiwi_ref, kbf_ref, kibf_ref):
    xb = _layer_norm(x_ref[...], g_ref[...], b_ref[...]).astype(BF16)
    ch, sh, ci, si = ch_ref[...], sh_ref[...], ci_ref[...], si_ref[...]
    q = _rope(_dot(xb, wq_ref[...]), ch, sh, HEAD_DIM // ROT_FRACTION // 2)
    _store_head_major(q_ref, (q * QK_SCALE_LOG2E).astype(BF16))
    qi = _rope(_dot(xb, wqi_ref[...]), ci, si, IDX_DIM // ROT_FRACTION // 2)
    _store_head_major(qi_ref, qi.astype(BF16))
    kv = _dot(xb, wkv_ref[...])
    k = _rope(kv[:, :KV_DIM], ch, sh, HEAD_DIM // ROT_FRACTION // 2)
    kbf_ref[...] = k.astype(BF16)
    rows = k.shape[0]
    for n in range(N_KV_HEADS):
        k_ref[pl.ds(n, rows, stride=N_KV_HEADS), :] = k[:, n * HEAD_DIM:(n + 1) * HEAD_DIM]
        v_ref[pl.ds(n, rows, stride=N_KV_HEADS), :] = kv[:, KV_DIM + n * HEAD_DIM:KV_DIM + (n + 1) * HEAD_DIM]
    kw = _rope(_dot(xb, wkw_ref[...]), ci, si, IDX_DIM // ROT_FRACTION // 2)
    kiwi_ref[...] = kw
    kibf_ref[...] = kw.astype(BF16)


def _qkv_project(x, ln_g, ln_b, w, tabs, rows, seq):
    n = x.shape[0]
    tm = min(rows, n)
    tab_blocks = max(seq, tm) // tm
    row = lambda width: pl.BlockSpec((tm, width), lambda i: (i, 0))
    tab = pl.BlockSpec((tm, LANES), lambda i: (i % tab_blocks, 0))
    cache_rows = pl.BlockSpec((tm * N_KV_HEADS, HEAD_DIM), lambda i: (i, 0))
    hm_shape = lambda heads: (n // Q_TILE, heads, Q_TILE, LANES)
    hm = lambda heads: pl.BlockSpec((tm // Q_TILE, heads, Q_TILE, LANES), lambda i: (i, 0, 0, 0))
    out_shapes = (
        jax.ShapeDtypeStruct(hm_shape(N_HEADS), BF16),
        jax.ShapeDtypeStruct(hm_shape(IDX_HEADS), BF16),
        jax.ShapeDtypeStruct((n * N_KV_HEADS, HEAD_DIM), F32),
        jax.ShapeDtypeStruct((n * N_KV_HEADS, HEAD_DIM), F32),
        jax.ShapeDtypeStruct((n, LANES), F32),
        jax.ShapeDtypeStruct((n, KV_DIM), BF16),
        jax.ShapeDtypeStruct((n, LANES), BF16),
    )
    return pl.pallas_call(
        _qkv_kernel,
        grid=(n // tm,),
        in_specs=[row(D_MODEL), _resident((1, D_MODEL)), _resident((1, D_MODEL)),
                  _resident(w["q"].shape), _resident(w["qi"].shape), _resident(w["kv"].shape),
                  _resident(w["kw"].shape), tab, tab, tab, tab],
        out_specs=(hm(N_HEADS), hm(IDX_HEADS), cache_rows, cache_rows, row(LANES),
                   row(KV_DIM), row(LANES)),
        out_shape=out_shapes,
        compiler_params=_params("arbitrary"),
        name="qkv_proj",
    )(x, ln_g, ln_b, w["q"], w["qi"], w["kv"], w["kw"], *tabs)


def _sortable_key(score):
    bits = lax.bitcast_convert_type(score + 0.0, I32)
    return bits ^ ((bits >> 31) & INT_MAX)


def _dsa_kernel(q_ref, qi_ref, kiwi_ref, k_ref, v_ref, ki_ref, o_ref,
                vt_ref, key_ref, acc_ref, m_ref, j_ref,
                *, past, seq, n_keys, topk, n_key_tiles, k_interleaved):
    i = pl.program_id(1)
    tq, kt = Q_TILE, KEY_TILE
    uq = HEADS_PER_UNIT * tq

    def kv_tile(ref, j, n, interleaved):
        if interleaved:
            return ref[0, pl.ds(j * (kt * N_KV_HEADS) + n, kt, stride=N_KV_HEADS), :]
        return ref[0, pl.ds(pl.multiple_of(j * kt, kt), kt), n * HEAD_DIM:(n + 1) * HEAD_DIM]

    @pl.when(i == 0)
    def _():
        ones = jnp.ones((DENOM_ROWS, kt), BF16)
        for c in range(n_key_tiles):
            for n in range(N_KV_HEADS):
                vt_ref[c, n, :HEAD_DIM, :] = kv_tile(v_ref, c, n, True).T.astype(BF16)
                vt_ref[c, n, HEAD_DIM:, :] = ones

    lane = lax.broadcasted_iota(I32, (1, tq), 1)
    t = i * tq + lane
    q_real = t < seq
    lim = jnp.minimum(((past + t) // CHUNK + 1) * CHUNK, n_keys)
    lim_max = jnp.minimum(((past + i * tq + tq - 1) // CHUNK + 1) * CHUNK, n_keys)
    n_tiles = (lim_max + kt - 1) // kt
    row = lax.broadcasted_iota(I32, (kt, tq), 0)

    wi = kiwi_ref[0].T[IDX_DIM:IDX_DIM + IDX_HEADS, :] * (IDX_DIM ** -0.5)

    def score_tile(j, n_nonneg):
        kib = ki_ref[0, pl.ds(pl.multiple_of(j * kt, kt), kt), :]
        rel_of = lambda u: _dot_nt(kib, qi_ref[0, 0, u * uq:(u + 1) * uq, :])
        n_units = IDX_HEADS // HEADS_PER_UNIT
        acc = jnp.zeros((kt, tq), F32)
        rel_next = rel_of(0)
        for u in range(n_units):
            rel, rel_next = rel_next, (rel_of(u + 1) if u + 1 < n_units else None)
            for g in range(HEADS_PER_UNIT):
                h = u * HEADS_PER_UNIT + g
                acc = acc + jnp.maximum(rel[:, g * tq:(g + 1) * tq], 0.0) * wi[h:h + 1, :]
        score = jnp.where(j * kt + row < lim, acc, -jnp.inf)
        key = _sortable_key(score)
        key_ref[j] = key
        for r in range(kt // COUNT_ROWS):
            n_nonneg = jnp.where(key[r * COUNT_ROWS:(r + 1) * COUNT_ROWS] >= 0, n_nonneg + 1, n_nonneg)
        return n_nonneg

    n_nonneg = lax.fori_loop(0, n_tiles, score_tile, jnp.zeros((COUNT_ROWS, tq), I32))

    def count(pred):
        def body(j, acc):
            blk = key_ref[j]
            for r in range(kt // COUNT_ROWS):
                rows = slice(r * COUNT_ROWS, (r + 1) * COUNT_ROWS)
                acc = jnp.where(pred(blk[rows], j * kt + row[rows]), acc + 1, acc)
            return acc
        acc = lax.fori_loop(0, n_tiles, body, jnp.zeros((COUNT_ROWS, tq), I32))
        return jnp.sum(acc, axis=0, keepdims=True)

    def count_ge(cand):
        return count(lambda keys, kidx: keys >= cand)

    def bisect(b, state):
        thr, cnt = state
        cand = thr | jnp.left_shift(jnp.int32(1), 30 - b)
        c = count_ge(cand)
        take = c >= topk
        return jnp.where(take, cand, thr), jnp.where(take, c, cnt)

    c0 = jnp.sum(n_nonneg, axis=0, keepdims=True)
    nonneg = c0 >= topk
    thr, cnt = lax.fori_loop(0, 31, bisect, (jnp.where(nonneg, 0, INT_MIN).astype(I32),
                                             jnp.where(nonneg, c0, n_tiles * kt).astype(I32)))
    tied = jnp.max(jnp.where(q_real, cnt, topk)) > topk

    j_ref[...] = jnp.full((1, tq), INT_MAX, I32)

    @pl.when(tied)
    def _():
        n_ties_wanted = topk - count(lambda keys, kidx: keys > thr)
        n_bits = int(n_key_tiles * kt).bit_length()

        def bisect_idx(it, cut):
            cand = cut | jnp.left_shift(jnp.int32(1), n_bits - 1 - it)
            c = count(lambda keys, kidx: (keys == thr) & (kidx < cand))
            return jnp.where(c <= n_ties_wanted, cand, cut)

        j_ref[...] = lax.fori_loop(0, n_bits, bisect_idx, jnp.zeros((1, tq), I32))

    j_cut = j_ref[...]

    m_ref[...] = jnp.full(m_ref.shape, MASKED_LOGIT, F32)
    acc_ref[...] = jnp.zeros(acc_ref.shape, F32)

    def attend_tiles(step, carry):
        n_units = N_HEADS // HEADS_PER_UNIT
        kv_of = lambda u: u * HEADS_PER_UNIT // KV_GROUP
        tiles = []
        for t in range(ATTEND_TILES_PER_STEP):
            j = step * ATTEND_TILES_PER_STEP + t
            jc = jnp.minimum(j, n_key_tiles - 1)
            keys = key_ref[jc]
            kidx = jc * kt + row
            sel = ((keys > thr) | ((keys == thr) & (kidx < j_cut))) & (kidx < lim) & (j < n_tiles)
            tiles.append((jc, sel))
        items = [(t, u) for t in range(ATTEND_TILES_PER_STEP) for u in range(n_units)]

        def logits(item):
            (jc, _), u = tiles[item[0]], item[1]
            kb = kv_tile(k_ref, jc, kv_of(u), k_interleaved).astype(BF16)
            return _dot_nt(kb, q_ref[0, 0, u * uq:(u + 1) * uq, :])

        s_next = logits(items[0])
        for k, (t, u) in enumerate(items):
            jc, sel = tiles[t]
            s, s_next = s_next, (logits(items[k + 1]) if k + 1 < len(items) else None)
            s = jnp.concatenate([jnp.where(sel, s[:, g * tq:(g + 1) * tq], MASKED_LOGIT)
                                 for g in range(HEADS_PER_UNIT)], axis=1)
            m_old = m_ref[u]
            m_new = jnp.maximum(m_old, jnp.max(s, axis=0, keepdims=True))
            p = jnp.exp2(s - m_new).astype(BF16)
            acc_ref[u] = jnp.exp2(m_old - m_new) * acc_ref[u] + _dot(vt_ref[jc, kv_of(u)], p)
            m_ref[u] = m_new
        return carry

    lax.fori_loop(0, (n_tiles + ATTEND_TILES_PER_STEP - 1) // ATTEND_TILES_PER_STEP, attend_tiles, 0)

    for h in range(N_HEADS):
        u, cols = h // HEADS_PER_UNIT, slice((h % HEADS_PER_UNIT) * tq, (h % HEADS_PER_UNIT + 1) * tq)
        o = acc_ref[u, :HEAD_DIM, cols] / acc_ref[u, HEAD_DIM:HEAD_DIM + 1, cols]
        o_ref[0, :, h * HEAD_DIM:(h + 1) * HEAD_DIM] = o.T.astype(BF16)


def _dsa_attention(q, qi, kiwi, k, v, ki_bf, *, past, seq, n_keys, k_interleaved):
    b, tq_pad, _ = kiwi.shape
    l_pad = ki_bf.shape[1]
    n_key_tiles = l_pad // KEY_TILE
    topk = min(TOPK_MAX, n_keys // 4)
    qblk = lambda width: pl.BlockSpec((1, Q_TILE, width), lambda bi, i: (bi, i, 0))
    hmblk = lambda heads: pl.BlockSpec((1, 1, heads * Q_TILE, LANES), lambda bi, i: (bi, i, 0, 0))
    kblk = lambda rows, width: pl.BlockSpec((1, rows, width), lambda bi, i: (bi, 0, 0),
                                            pipeline_mode=pl.Buffered(1))
    cache_blk = kblk(l_pad * N_KV_HEADS, HEAD_DIM)
    kern = functools.partial(_dsa_kernel, past=past, seq=seq, n_keys=n_keys, topk=topk,
                             n_key_tiles=n_key_tiles, k_interleaved=k_interleaved)
    return pl.pallas_call(
        kern,
        grid=(b, tq_pad // Q_TILE),
        in_specs=[hmblk(N_HEADS), hmblk(IDX_HEADS), qblk(LANES),
                  cache_blk if k_interleaved else kblk(l_pad, KV_DIM), cache_blk, kblk(l_pad, LANES)],
        out_specs=qblk(ATTN_DIM),
        out_shape=jax.ShapeDtypeStruct((b, tq_pad, ATTN_DIM), BF16),
        scratch_shapes=[
            pltpu.VMEM((n_key_tiles, N_KV_HEADS, HEAD_DIM + DENOM_ROWS, KEY_TILE), BF16),
            pltpu.VMEM((n_key_tiles, KEY_TILE, Q_TILE), I32),
            pltpu.VMEM((N_HEADS // HEADS_PER_UNIT, HEAD_DIM + DENOM_ROWS, HEADS_PER_UNIT * Q_TILE), F32),
            pltpu.VMEM((N_HEADS // HEADS_PER_UNIT, 1, HEADS_PER_UNIT * Q_TILE), F32),
            pltpu.VMEM((1, Q_TILE), I32),
        ],
        compiler_params=_params("arbitrary", "arbitrary"),
        name="dsa_attention",
    )(q, qi, kiwi, k, v, ki_bf)


def _mixer_kernel(x_ref, attn_ref, memk_ref, memv_ref, prev_a_ref, prev_b_ref,
                  lng_ref, lnb_ref, cw_ref, ln1g_ref, ln1b_ref,
                  wgb_ref, wgc_ref, wci_ref, wmq_ref, wgl_ref, wao_ref, wco_ref, wmo_ref, wo_ref,
                  x1_ref, state_ref, carry_ref, *, seq, rows):
    i = pl.program_id(0)
    multi_seq = seq < rows
    xn = _layer_norm(x_ref[...], lng_ref[...], lnb_ref[...])
    xb = xn.astype(BF16)

    u = _dot(xb, wgc_ref[...]) * _dot(xb, wci_ref[...])
    ridx = lax.broadcasted_iota(I32, u.shape, 0)
    if multi_seq:
        tpos = ridx & (seq - 1)
        s1 = jnp.where(tpos == 0, prev_a_ref[...], pltpu.roll(u, 1, 0))
        s2 = jnp.where(tpos < 2, prev_b_ref[...], pltpu.roll(u, 2, 0))
        state_ref[...] = u
    else:
        blocks_per_seq = seq // rows
        @pl.when(i % blocks_per_seq == 0)
        def _():
            carry_ref[...] = prev_a_ref[0]

        prev = carry_ref[...]
        p1, p2 = prev[SUBLANES - 1:SUBLANES, :], prev[SUBLANES - 2:SUBLANES - 1, :]
        s1 = jnp.where(ridx == 0, p1, pltpu.roll(u, 1, 0))
        s2 = jnp.where(ridx == 0, p2, jnp.where(ridx == 1, p1, pltpu.roll(u, 2, 0)))
        carry_ref[...] = u[rows - SUBLANES:, :]
        state_ref[0] = u[rows - SUBLANES:, :]
    cw = cw_ref[...]
    conv = cw[0:1, :] * s2 + cw[1:2, :] * s1 + cw[2:3, :] * u
    conv_in = (_dot(xb, wgb_ref[...]) * conv).astype(BF16)

    mq = _dot(xb, wmq_ref[...]).astype(BF16)
    n_batches = max(1, rows // seq)
    rb = rows // n_batches
    mem_rows = []
    for bi in range(n_batches):
        heads = []
        for h in range(MEM_HEADS):
            cols = slice(h * MEM_HEAD_DIM, (h + 1) * MEM_HEAD_DIM)
            s = _dot_nt(mq[bi * rb:(bi + 1) * rb, cols], memk_ref[bi, :, cols]) * (MEM_HEAD_DIM ** -0.5)
            e = jnp.exp(s - jnp.max(s, axis=-1, keepdims=True))
            p = e / jnp.sum(e, axis=-1, keepdims=True)
            heads.append(_dot(p.astype(BF16), memv_ref[bi, :, cols]))
        mem_rows.append(jnp.concatenate(heads, axis=1))
    mem = (mem_rows[0] if n_batches == 1 else jnp.concatenate(mem_rows, axis=0)).astype(BF16)

    gate = _sigmoid(_dot(xb, wgl_ref[...]))
    mixed = (gate[:, :D_MODEL] * _dot(attn_ref[...], wao_ref[...])
             + gate[:, D_MODEL:2 * D_MODEL] * _dot(conv_in, wco_ref[...])
             + gate[:, 2 * D_MODEL:] * _dot(mem, wmo_ref[...]))
    y = ALPHA * xn + _dot(mixed.astype(BF16), wo_ref[...])
    x1_ref[...] = _layer_norm(y, ln1g_ref[...], ln1b_ref[...])


def _mixer(x, attn, mem_k, mem_v, prev_a, prev_b, ln_g, ln_b, conv_w8, ln1_g, ln1_b, w, *, seq, rows):
    n = x.shape[0]
    tm = min(rows, n)
    multi_seq = seq < tm
    n_batches = max(1, tm // seq)
    row = lambda width: pl.BlockSpec((tm, width), lambda i: (i, 0))
    mem = pl.BlockSpec((n_batches, N_MEM, MEM_DIM), lambda i: ((i * tm) // (seq * n_batches), 0, 0))
    vec = _resident((1, D_MODEL))
    if multi_seq:
        prev_spec = row(D_CONV)
        state_spec = row(D_CONV)
        state_shape = jax.ShapeDtypeStruct((n, D_CONV), F32)
    else:
        prev_spec = pl.BlockSpec((1, SUBLANES, D_CONV), lambda i: ((i * tm) // seq, 0, 0))
        state_spec = prev_spec
        state_shape = jax.ShapeDtypeStruct((n // seq, SUBLANES, D_CONV), F32)
    weights = [w[k] for k in ("gb", "gc", "ci", "mq", "gl", "ao", "co", "mo", "o")]
    kern = functools.partial(_mixer_kernel, seq=seq, rows=tm)
    return pl.pallas_call(
        kern,
        grid=(n // tm,),
        in_specs=[row(D_MODEL), row(ATTN_DIM), mem, mem, prev_spec, prev_spec,
                  vec, vec, _resident((SUBLANES, D_CONV)), vec, vec] + [_resident(m.shape) for m in weights],
        out_specs=(row(D_MODEL), state_spec),
        out_shape=(jax.ShapeDtypeStruct((n, D_MODEL), F32), state_shape),
        scratch_shapes=[pltpu.VMEM((SUBLANES, D_CONV), F32)],
        compiler_params=_params("arbitrary"),
        name="mixer",
    )(x, attn, mem_k, mem_v, prev_a, prev_b, ln_g, ln_b, conv_w8, ln1_g, ln1_b, *weights)


def _route(x, wr_hi, wr_lo, bias):
    tm = x.shape[0]
    x_hi = x.astype(BF16)
    x_lo = (x - x_hi.astype(F32)).astype(BF16)
    logits = _dot_nt(wr_hi, x_hi) + (_dot_nt(wr_hi, x_lo) + _dot_nt(wr_lo, x_hi))
    s = _sigmoid(logits)
    sc = s + bias[:, 0:1]
    sub = lax.broadcasted_iota(I32, (GROUP_SIZE, tm), 0)
    neg = -jnp.inf
    groups, gscore = [], []
    for g in range(N_GROUPS):
        xg = sc[g * GROUP_SIZE:(g + 1) * GROUP_SIZE, :]
        m1 = jnp.max(xg, axis=0, keepdims=True)
        i1 = jnp.min(jnp.where(xg == m1, sub, GROUP_SIZE), axis=0, keepdims=True)
        m2 = jnp.max(jnp.where(sub == i1, neg, xg), axis=0, keepdims=True)
        groups.append(xg)
        gscore.append(m1 + m2)
    masked = []
    for g in range(N_GROUPS):
        rank = jnp.zeros((1, tm), I32)
        for o in range(N_GROUPS):
            if o == g:
                continue
            beats = (gscore[o] > gscore[g]) | ((gscore[o] == gscore[g]) & (o < g))
            rank = rank + beats.astype(I32)
        masked.append(jnp.where(rank < TOPK_GROUPS, groups[g], neg))
    v = jnp.concatenate(masked, axis=0)
    eidx = lax.broadcasted_iota(I32, (N_EXPERTS, tm), 0)
    chosen = jnp.zeros((N_EXPERTS, tm), jnp.bool_)
    for _ in range(TOP_K):
        m = jnp.max(v, axis=0, keepdims=True)
        first = jnp.min(jnp.where(v == m, eidx, N_EXPERTS), axis=0, keepdims=True)
        hit = eidx == first
        chosen = chosen | hit
        v = jnp.where(hit, neg, v)
    wsel = jnp.where(chosen, s, 0.0)
    return wsel / jnp.sum(wsel, axis=0, keepdims=True) * ROUTED_SCALE


def _expert_ffn(xb, w_gu, w_down, gate_col=None):
    gu = _dot(xb, w_gu)
    g, u = gu[:, :D_EXPERT], gu[:, D_EXPERT:]
    h = g * _sigmoid(g) * u
    if gate_col is not None:
        h = h * gate_col
    return _dot(h.astype(BF16), w_down)


def _moe_kernel(x_ref, wrh_ref, wrl_ref, br_ref, wgu_ref, wd_ref, wsgu_ref, wsd_ref, g_ref, b_ref,
                o_ref, xb_ref, gate_ref, acc_ref, *, experts_per_step):
    j = pl.program_id(1)

    @pl.when(j == 0)
    def _():
        x = x_ref[...]
        gate_t = _route(x, wrh_ref[...], wrl_ref[...], br_ref[...])
        pad = jnp.zeros((LANES - N_EXPERTS, x.shape[0]), F32)
        gate_ref[...] = jnp.concatenate([gate_t, pad], axis=0).T
        xb = x.astype(BF16)
        xb_ref[...] = xb
        acc_ref[...] = _expert_ffn(xb, wsgu_ref[...], wsd_ref[...])

    xb = xb_ref[...]
    gate = gate_ref[...]
    lane = lax.broadcasted_iota(I32, gate.shape, 1)
    acc = acc_ref[...]
    for e in range(experts_per_step):
        col = jnp.sum(jnp.where(lane == j * experts_per_step + e, gate, 0.0), axis=1, keepdims=True)
        acc = acc + _expert_ffn(xb, wgu_ref[e], wd_ref[e], col)
    acc_ref[...] = acc

    @pl.when(j == pl.num_programs(1) - 1)
    def _():
        o_ref[...] = _layer_norm(ALPHA * x_ref[...] + acc_ref[...], g_ref[...], b_ref[...])


def _moe(x, w, ln_g, ln_b, rows):
    n = x.shape[0]
    tm = min(rows, n)
    eps = MOE_EXPERTS_PER_STEP
    row = pl.BlockSpec((tm, D_MODEL), lambda i, j: (i, 0))
    kern = functools.partial(_moe_kernel, experts_per_step=eps)
    return pl.pallas_call(
        kern,
        grid=(n // tm, N_EXPERTS // eps),
        in_specs=[row, _resident(w["r_hi"].shape), _resident(w["r_lo"].shape), _resident(w["r_b"].shape),
                  pl.BlockSpec((eps, D_MODEL, 2 * D_EXPERT), lambda i, j: (j, 0, 0)),
                  pl.BlockSpec((eps, D_EXPERT, D_MODEL), lambda i, j: (j, 0, 0)),
                  _resident(w["s_gu"].shape), _resident(w["s_d"].shape),
                  _resident((1, D_MODEL)), _resident((1, D_MODEL))],
        out_specs=row,
        out_shape=jax.ShapeDtypeStruct((n, D_MODEL), F32),
        scratch_shapes=[pltpu.VMEM((tm, D_MODEL), BF16), pltpu.VMEM((tm, LANES), F32),
                        pltpu.VMEM((tm, D_MODEL), F32)],
        compiler_params=_params("arbitrary", "arbitrary"),
        name="moe",
    )(x, w["r_hi"], w["r_lo"], w["r_b"], w["e_gu"], w["e_d"], w["s_gu"], w["s_d"], ln_g, ln_b)


def _pad_rows(a, rows):
    return a if a.shape[1] == rows else jnp.pad(a, ((0, 0), (0, rows - a.shape[1]), (0, 0)))


def _head_major_per_batch(x, b, seq, tq_pad):
    heads = x.shape[1]
    if seq % Q_TILE == 0:
        return x.reshape(b, seq // Q_TILE, heads * Q_TILE, LANES)
    per_tile = Q_TILE // seq
    x = x.reshape(-1, heads, per_tile, seq, LANES).transpose(0, 2, 1, 3, 4).reshape(b, heads, seq, LANES)
    x = jnp.pad(x, ((0, 0), (0, 0), (0, tq_pad - seq), (0, 0)))
    return x.reshape(b, 1, heads * tq_pad, LANES)


def _layer(x, mem_k_bf, mem_v_bf, past_k, past_v, past_ki, conv_prev, w, ln, layer):
    b, seq, _ = x.shape
    past = past_k.shape[1]
    n = b * seq
    n_keys = past + seq
    x2 = x.reshape(n, D_MODEL)

    pos = past + jnp.arange(seq, dtype=I32)
    reps = max(1, min(PROJ_ROWS, n) // seq)
    pos_t = jnp.tile(pos, reps)
    tabs = (_rope_tables(pos_t, HEAD_DIM // ROT_FRACTION // 2)
            + _rope_tables(pos_t, IDX_DIM // ROT_FRACTION // 2,
                           (IDX_DIM, IDX_DIM + IDX_HEADS), IDX_HEADS ** -0.5))
    q, qi, k, v, kiwi, k_bf, ki_bf = _qkv_project(x2, ln["in_g"], ln["in_b"], w, tabs, PROJ_ROWS, seq)

    l_pad = -(-n_keys // KEY_TILE) * KEY_TILE
    tq_pad = -(-seq // Q_TILE) * Q_TILE
    rows = lambda a, length: a.reshape(b, length * N_KV_HEADS, HEAD_DIM)
    with_cache = lambda new, cache: new if past == 0 else jnp.concatenate([rows(cache, past), new], axis=1)
    v3 = _pad_rows(with_cache(rows(v, seq), past_v), l_pad * N_KV_HEADS)
    k_interleaved = past > 0
    if k_interleaved:
        k3 = _pad_rows(with_cache(rows(k, seq), past_k), l_pad * N_KV_HEADS)
    else:
        k3 = _pad_rows(k_bf.reshape(b, seq, KV_DIM), l_pad)
    ki_past = jnp.pad(past_ki, ((0, 0), (0, 0), (0, LANES - IDX_DIM))).astype(BF16)
    ki3 = jnp.concatenate([ki_past, ki_bf.reshape(b, seq, LANES)], axis=1)
    attn = _dsa_attention(
        _head_major_per_batch(q, b, seq, tq_pad), _head_major_per_batch(qi, b, seq, tq_pad),
        _pad_rows(kiwi.reshape(b, seq, LANES), tq_pad), k3, v3, _pad_rows(ki3, l_pad),
        past=past, seq=seq, n_keys=n_keys, k_interleaved=k_interleaved)
    attn = attn[:, :seq].reshape(n, ATTN_DIM)

    tm = min(PROJ_ROWS, n)
    if seq < tm:
        tpos = jnp.arange(seq)[None, :, None]
        prev_a = jnp.where(tpos == 0, conv_prev[:, 1:2], 0.0).reshape(n, D_CONV)
        prev_b = jnp.where(tpos == 0, conv_prev[:, 0:1], jnp.where(tpos == 1, conv_prev[:, 1:2], 0.0)).reshape(n, D_CONV)
    else:
        prev_a = jnp.pad(conv_prev, ((0, 0), (SUBLANES - (CONV_WIDTH - 1), 0), (0, 0)))
        prev_b = prev_a
    x1, state = _mixer(x2, attn, mem_k_bf, mem_v_bf, prev_a, prev_b, ln["in_g"], ln["in_b"], w["conv"],
                       ln["g1"], ln["b1"], w, seq=seq, rows=PROJ_ROWS)
    if seq < tm:
        new_conv = state.reshape(b, seq, D_CONV)[:, seq - (CONV_WIDTH - 1):]
    else:
        new_conv = state[:, SUBLANES - (CONV_WIDTH - 1):]

    y = _moe(x1, w, ln["g2"], ln["b2"], MOE_ROWS)
    return (y.reshape(b, seq, D_MODEL), k.reshape(b, seq, N_KV_HEADS, HEAD_DIM),
            v.reshape(b, seq, N_KV_HEADS, HEAD_DIM), kiwi[:, :IDX_DIM].reshape(b, seq, IDX_DIM), new_conv)


def _prep_weights(l, w_in, conv_w, w_attn_out, w_conv_out, w_mem_out, w_o, w_router, b_router,
                  w_exp_gu, w_exp_down, w_sh_gu, w_sh_down):
    sizes = [ATTN_DIM, KV_DIM, KV_DIM, IDX_HEADS * IDX_DIM, IDX_DIM, IDX_HEADS,
             D_CONV, D_CONV, D_CONV, MEM_DIM, N_BRANCH * D_MODEL]
    offs = np.concatenate([[0], np.cumsum(sizes)]).tolist()
    wi = w_in[l]
    part = lambda j: wi[:, offs[j]:offs[j + 1]]
    w_qi = part(3).reshape(D_MODEL, IDX_HEADS, IDX_DIM)
    w_qi = jnp.pad(w_qi, ((0, 0), (0, 0), (0, LANES - IDX_DIM))).reshape(D_MODEL, IDX_HEADS * LANES)
    w_kw = jnp.pad(jnp.concatenate([part(4), part(5)], axis=1), ((0, 0), (0, LANES - IDX_DIM - IDX_HEADS)))
    wr = w_router[l].T
    wr_hi = wr.astype(BF16)
    return {
        "q": part(0).astype(BF16), "kv": jnp.concatenate([part(1), part(2)], axis=1).astype(BF16),
        "qi": w_qi.astype(BF16), "kw": w_kw.astype(BF16),
        "gb": part(6).astype(BF16), "gc": part(7).astype(BF16), "ci": part(8).astype(BF16),
        "mq": part(9).astype(BF16), "gl": part(10).astype(BF16),
        "ao": w_attn_out[l].astype(BF16), "co": w_conv_out[l].astype(BF16), "mo": w_mem_out[l].astype(BF16),
        "o": w_o[l].astype(BF16),
        "conv": jnp.pad(conv_w[l], ((0, SUBLANES - CONV_WIDTH), (0, 0))),
        "r_hi": wr_hi, "r_lo": (wr - wr_hi.astype(F32)).astype(BF16),
        "r_b": jnp.broadcast_to(b_router[l][:, None], (N_EXPERTS, LANES)),
        "e_gu": w_exp_gu[l].astype(BF16), "e_d": w_exp_down[l].astype(BF16),
        "s_gu": w_sh_gu[l].astype(BF16), "s_d": w_sh_down[l].astype(BF16),
    }


def kernel(x_prompt, x_sample, mem_prompt, cache_attn_k, cache_attn_v, cache_idx_k, cache_mem_k, cache_mem_v,
           state_conv, ln_in_g, ln_in_b, w_in, conv_w, w_mem_kv, w_attn_out, w_conv_out, w_mem_out, w_o,
           ln1_g, ln1_b, w_router, b_router, w_exp_gu, w_exp_down, w_sh_gu, w_sh_down, ln2_g, ln2_b):
    assert DEPTH == 1 and w_in.shape[0] == 1
    l = 0
    bp, n_mem = mem_prompt.shape[0], mem_prompt.shape[1]
    w = _prep_weights(l, w_in, conv_w, w_attn_out, w_conv_out, w_mem_out, w_o, w_router, b_router,
                      w_exp_gu, w_exp_down, w_sh_gu, w_sh_down)
    ln = {"in_g": ln_in_g[None, :], "in_b": ln_in_b[None, :], "g1": ln1_g[l][None, :], "b1": ln1_b[l][None, :],
          "g2": ln2_g[l][None, :], "b2": ln2_b[l][None, :]}

    mem_kv = _project(mem_prompt.reshape(bp * n_mem, D_MODEL), w_mem_kv[l].astype(BF16))
    mem_k = mem_kv[:, :MEM_DIM].reshape(bp, n_mem, MEM_DIM)
    mem_v = mem_kv[:, MEM_DIM:].reshape(bp, n_mem, MEM_DIM)

    zeros = lambda *s: jnp.zeros(s, F32)
    yp, kp, vp, kip, cp = _layer(
        x_prompt, mem_k.astype(BF16), mem_v.astype(BF16),
        zeros(bp, 0, N_KV_HEADS, HEAD_DIM), zeros(bp, 0, N_KV_HEADS, HEAD_DIM), zeros(bp, 0, IDX_DIM),
        zeros(bp, CONV_WIDTH - 1, D_CONV), w, ln, l)
    bs = x_sample.shape[0]
    ys, ks, vs, kis, cs = _layer(
        x_sample, cache_mem_k[l].reshape(bs, n_mem, MEM_DIM).astype(BF16),
        cache_mem_v[l].reshape(bs, n_mem, MEM_DIM).astype(BF16),
        cache_attn_k[l], cache_attn_v[l], cache_idx_k[l], state_conv[l], w, ln, l)

    mem_shape = (1, bp, n_mem, MEM_HEADS, MEM_HEAD_DIM)
    return (yp, ys, kp[None], vp[None], kip[None], mem_k.reshape(mem_shape), mem_v.reshape(mem_shape), cp[None],
            ks[None], vs[None], kis[None], cs[None])
```

```python
import functools

import jax
import jax.numpy as jnp
import numpy as np
from jax import lax
from jax.experimental import pallas as pl
from jax.experimental.pallas import tpu as pltpu

D_MODEL = 1024
DEPTH = 1
CHUNK = 64
HEAD_DIM = 128
N_HEADS = D_MODEL // HEAD_DIM
N_KV_HEADS = max(1, N_HEADS // 4)
KV_GROUP = N_HEADS // N_KV_HEADS
ATTN_DIM = N_HEADS * HEAD_DIM
KV_DIM = N_KV_HEADS * HEAD_DIM
ROT_FRACTION = 4
ROPE_THETA = 500000.0
IDX_HEADS = 8
IDX_DIM = 64
TOPK_MAX = 256
D_CONV = D_MODEL
CONV_WIDTH = 3
N_MEM = 256
MEM_HEADS = 4
MEM_HEAD_DIM = D_MODEL // MEM_HEADS
MEM_DIM = MEM_HEADS * MEM_HEAD_DIM
N_BRANCH = 3
N_EXPERTS = 64
TOP_K = 8
N_GROUPS = 8
GROUP_SIZE = N_EXPERTS // N_GROUPS
TOPK_GROUPS = 4
D_EXPERT = D_MODEL // 4
ROUTED_SCALE = 2.5
LN_EPS = 1e-5
ALPHA = (2.0 * DEPTH) ** 0.25

LANES = 128
SUBLANES = 8
VMEM_LIMIT_BYTES = 56 * 1024 * 1024

Q_TILE = 128
KEY_TILE = 512
HEADS_PER_UNIT = 2
ATTEND_SPLITS = 1
COUNT_ROWS = 64
DENOM_ROWS = 16
PROJ_ROWS = 512
MOE_ROWS = 1024
MOE_EXPERTS_PER_STEP = 4

F32 = jnp.float32
BF16 = jnp.bfloat16
I32 = jnp.int32
INT_MIN = -(2 ** 31)
INT_MAX = 2 ** 31 - 1
MASKED_LOGIT = -1e30
QK_SCALE_LOG2E = HEAD_DIM ** -0.5 * float(np.log2(np.e))

_NT = (((1,), (1,)), ((), ()))


def _dot(a, b):
    return jnp.dot(a, b, preferred_element_type=F32)


def _dot_nt(a, b):
    return lax.dot_general(a, b, _NT, preferred_element_type=F32)


def _layer_norm(x, g, b):
    mu = jnp.mean(x, axis=-1, keepdims=True)
    xc = x - mu
    var = jnp.mean(xc * xc, axis=-1, keepdims=True)
    return xc * lax.rsqrt(var + LN_EPS) * g + b


def _sigmoid(x):
    return 1.0 / (1.0 + jnp.exp(-x))


def _resident(shape):
    nd = len(shape)
    return pl.BlockSpec(shape, lambda *_: (0,) * nd, pipeline_mode=pl.Buffered(1))


def _params(*sem):
    return pltpu.CompilerParams(dimension_semantics=sem, vmem_limit_bytes=VMEM_LIMIT_BYTES)


def _rope_tables(pos, half, extra_scale_lanes=None, extra_scale=1.0):
    inv_freq = ROPE_THETA ** (-jnp.arange(half, dtype=F32) / half)
    ang = pos.astype(F32)[:, None] * inv_freq[None, :]
    cos, sin = jnp.cos(ang), jnp.sin(ang)
    n = pos.shape[0]
    c = jnp.concatenate([cos, cos, jnp.ones((n, LANES - 2 * half), F32)], axis=1)
    s = jnp.concatenate([-sin, sin, jnp.zeros((n, LANES - 2 * half), F32)], axis=1)
    if extra_scale_lanes is not None:
        lo, hi = extra_scale_lanes
        lane = jnp.arange(LANES)[None, :]
        c = jnp.where((lane >= lo) & (lane < hi), jnp.float32(extra_scale), c)
    return c, s


def _rope(x, c, s, half):
    rows, width = x.shape
    lane = lax.broadcasted_iota(I32, (rows, LANES), 1)
    first = lane < half
    outs = []
    for j in range(width // LANES):
        xj = x[:, j * LANES:(j + 1) * LANES]
        swap = jnp.where(first, pltpu.roll(xj, LANES - half, 1), pltpu.roll(xj, half, 1))
        outs.append(xj * c + swap * s)
    return outs[0] if len(outs) == 1 else jnp.concatenate(outs, axis=1)


def _proj_kernel(x_ref, w_ref, o_ref):
    o_ref[...] = _dot(x_ref[...].astype(BF16), w_ref[...])


def _project(x, w_bf, col_tile=512):
    n, d = x.shape
    m = w_bf.shape[1]
    return pl.pallas_call(
        _proj_kernel,
        grid=(m // col_tile,),
        in_specs=[pl.BlockSpec((n, d), lambda j: (0, 0)),
                  pl.BlockSpec((d, col_tile), lambda j: (0, j))],
        out_specs=pl.BlockSpec((n, col_tile), lambda j: (0, j)),
        out_shape=jax.ShapeDtypeStruct((n, m), F32),
        compiler_params=_params("arbitrary"),
        name="mem_kv_proj",
    )(x, w_bf)


def _store_head_major(ref, x):
    for r in range(ref.shape[0]):
        for h in range(ref.shape[1]):
            ref[r, h] = x[r * Q_TILE:(r + 1) * Q_TILE, h * LANES:(h + 1) * LANES]


def _qkv_kernel(x_ref, g_ref, b_ref, wq_ref, wqi_ref, wkv_ref, wkw_ref, ch_ref, sh_ref, ci_ref, si_ref,
                q_ref, qi_ref, k_ref, v_ref, kiwi_ref, kbf_ref, kibf_ref):
    xb = _layer_norm(x_ref[...], g_ref[...], b_ref[...]).astype(BF16)
    ch, sh, ci, si = ch_ref[...], sh_ref[...], ci_ref[...], si_ref[...]
    q = _rope(_dot(xb, wq_ref[...]), ch, sh, HEAD_DIM // ROT_FRACTION // 2)
    _store_head_major(q_ref, (q * QK_SCALE_LOG2E).astype(BF16))
    qi = _rope(_dot(xb, wqi_ref[...]), ci, si, IDX_DIM // ROT_FRACTION // 2)
    _store_head_major(qi_ref, qi.astype(BF16))
    kv = _dot(xb, wkv_ref[...])
    k = _rope(kv[:, :KV_DIM], ch, sh, HEAD_DIM // ROT_FRACTION // 2)
    kbf_ref[...] = k.astype(BF16)
    rows = k.shape[0]
    for n in range(N_KV_HEADS):
        k_ref[pl.ds(n, rows, stride=N_KV_HEADS), :] = k[:, n * HEAD_DIM:(n + 1) * HEAD_DIM]
        v_ref[pl.ds(n, rows, stride=N_KV_HEADS), :] = kv[:, KV_DIM + n * HEAD_DIM:KV_DIM + (n + 1) * HEAD_DIM]
    kw = _rope(_dot(xb, wkw_ref[...]), ci, si, IDX_DIM // ROT_FRACTION // 2)
    kiwi_ref[...] = kw
    kibf_ref[...] = kw.astype(BF16)


def _qkv_project(x, ln_g, ln_b, w, tabs, rows, seq):
    n = x.shape[0]
    tm = min(rows, n)
    tab_blocks = max(seq, tm) // tm
    row = lambda width: pl.BlockSpec((tm, width), lambda i: (i, 0))
    tab = pl.BlockSpec((tm, LANES), lambda i: (i % tab_blocks, 0))
    cache_rows = pl.BlockSpec((tm * N_KV_HEADS, HEAD_DIM), lambda i: (i, 0))
    hm_shape = lambda heads: (n // Q_TILE, heads, Q_TILE, LANES)
    hm = lambda heads: pl.BlockSpec((tm // Q_TILE, heads, Q_TILE, LANES), lambda i: (i, 0, 0, 0))
    out_shapes = (
        jax.ShapeDtypeStruct(hm_shape(N_HEADS), BF16),
        jax.ShapeDtypeStruct(hm_shape(IDX_HEADS), BF16),
        jax.ShapeDtypeStruct((n * N_KV_HEADS, HEAD_DIM), F32),
        jax.ShapeDtypeStruct((n * N_KV_HEADS, HEAD_DIM), F32),
        jax.ShapeDtypeStruct((n, LANES), F32),
        jax.ShapeDtypeStruct((n, KV_DIM), BF16),
        jax.ShapeDtypeStruct((n, LANES), BF16),
    )
    return pl.pallas_call(
        _qkv_kernel,
        grid=(n // tm,),
        in_specs=[row(D_MODEL), _resident((1, D_MODEL)), _resident((1, D_MODEL)),
                  _resident(w["q"].shape), _resident(w["qi"].shape), _resident(w["kv"].shape),
                  _resident(w["kw"].shape), tab, tab, tab, tab],
        out_specs=(hm(N_HEADS), hm(IDX_HEADS), cache_rows, cache_rows, row(LANES),
                   row(KV_DIM), row(LANES)),
        out_shape=out_shapes,
        compiler_params=_params("arbitrary"),
        name="qkv_proj",
    )(x, ln_g, ln_b, w["q"], w["qi"], w["kv"], w["kw"], *tabs)


def _sortable_key(score):
    bits = lax.bitcast_convert_type(score + 0.0, I32)
    return bits ^ ((bits >> 31) & INT_MAX)


def _dsa_kernel(q_ref, qi_ref, kiwi_ref, k_ref, v_ref, ki_ref, o_ref,
                vt_ref, key_ref, acc_ref, m_ref, j_ref, qaug_ref,
                *, past, seq, n_keys, topk, n_key_tiles, k_interleaved):
    i = pl.program_id(1)
    tq, kt = Q_TILE, KEY_TILE
    uq = HEADS_PER_UNIT * tq

    def kv_tile(ref, j, n, interleaved, first=0, rows=KEY_TILE):
        if interleaved:
            return ref[0, pl.ds((j * kt + first) * N_KV_HEADS + n, rows, stride=N_KV_HEADS), :]
        return ref[0, pl.ds(pl.multiple_of(j * kt + first, SUBLANES), rows), n * HEAD_DIM:(n + 1) * HEAD_DIM]

    @pl.when(i == 0)
    def _():
        ones = jnp.ones((DENOM_ROWS, kt), BF16)
        for c in range(n_key_tiles):
            for n in range(N_KV_HEADS):
                vt_ref[c, n, :HEAD_DIM, :] = kv_tile(v_ref, c, n, True).T.astype(BF16)
                vt_ref[c, n, HEAD_DIM:, :] = ones

    lane = lax.broadcasted_iota(I32, (1, tq), 1)
    t = i * tq + lane
    q_real = t < seq
    lim = jnp.minimum(((past + t) // CHUNK + 1) * CHUNK, n_keys)
    lim_max = jnp.minimum(((past + i * tq + tq - 1) // CHUNK + 1) * CHUNK, n_keys)
    n_tiles = (lim_max + kt - 1) // kt
    row = lax.broadcasted_iota(I32, (kt, tq), 0)

    wi = kiwi_ref[0].T[IDX_DIM:IDX_DIM + IDX_HEADS, :] * (IDX_DIM ** -0.5)

    def score_tile(j, n_nonneg):
        kib = ki_ref[0, pl.ds(pl.multiple_of(j * kt, kt), kt), :]
        rel_of = lambda u: _dot_nt(kib, qi_ref[0, 0, u * uq:(u + 1) * uq, :])
        n_units = IDX_HEADS // HEADS_PER_UNIT
        acc = jnp.zeros((kt, tq), F32)
        rel_next = rel_of(0)
        for u in range(n_units):
            rel, rel_next = rel_next, (rel_of(u + 1) if u + 1 < n_units else None)
            for g in range(HEADS_PER_UNIT):
                h = u * HEADS_PER_UNIT + g
                acc = acc + jnp.maximum(rel[:, g * tq:(g + 1) * tq], 0.0) * wi[h:h + 1, :]
        score = jnp.where(row < lim - j * kt, acc, -jnp.inf)
        key = _sortable_key(score)
        key_ref[j] = key
        for r in range(kt // COUNT_ROWS):
            n_nonneg = jnp.where(key[r * COUNT_ROWS:(r + 1) * COUNT_ROWS] >= 0, n_nonneg + 1, n_nonneg)
        return n_nonneg

    n_nonneg = lax.fori_loop(0, n_tiles, score_tile, jnp.zeros((COUNT_ROWS, tq), I32))

    def count(pred):
        def body(j, acc):
            blk = key_ref[j]
            for r in range(kt // COUNT_ROWS):
                rows = slice(r * COUNT_ROWS, (r + 1) * COUNT_ROWS)
                acc = jnp.where(pred(blk[rows], j * kt + row[rows]), acc + 1, acc)
            return acc
        acc = lax.fori_loop(0, n_tiles, body, jnp.zeros((COUNT_ROWS, tq), I32))
        return jnp.sum(acc, axis=0, keepdims=True)

    def count_ge(cand):
        return count(lambda keys, kidx: keys >= cand)

    def bisect(b, state):
        thr, cnt = state
        cand = thr | jnp.left_shift(jnp.int32(1), 30 - b)
        c = count_ge(cand)
        take = c >= topk
        return jnp.where(take, cand, thr), jnp.where(take, c, cnt)

    c0 = jnp.sum(n_nonneg, axis=0, keepdims=True)
    nonneg = c0 >= topk
    thr, cnt = lax.fori_loop(0, 31, bisect, (jnp.where(nonneg, 0, INT_MIN).astype(I32),
                                             jnp.where(nonneg, c0, n_tiles * kt).astype(I32)))
    tied = jnp.max(jnp.where(q_real, cnt, topk)) > topk

    j_ref[...] = jnp.full((1, tq), INT_MAX, I32)

    @pl.when(tied)
    def _():
        n_ties_wanted = topk - count(lambda keys, kidx: keys > thr)
        n_bits = int(n_key_tiles * kt).bit_length()

        def bisect_idx(it, cut):
            cand = cut | jnp.left_shift(jnp.int32(1), n_bits - 1 - it)
            c = count(lambda keys, kidx: (keys == thr) & (kidx < cand))
            return jnp.where(c <= n_ties_wanted, cand, cut)

        j_ref[...] = lax.fori_loop(0, n_bits, bisect_idx, jnp.zeros((1, tq), I32))

    j_cut = j_ref[...]

    m_ref[...] = jnp.full(m_ref.shape, MASKED_LOGIT, F32)
    acc_ref[...] = jnp.zeros(acc_ref.shape, F32)
    eye = (lax.broadcasted_iota(I32, (tq, tq), 0) == lax.broadcasted_iota(I32, (tq, tq), 1)).astype(BF16)
    for u in range(N_HEADS // HEADS_PER_UNIT):
        qaug_ref[u, :, :HEAD_DIM] = q_ref[0, 0, u * uq:(u + 1) * uq, :]
        for g in range(HEADS_PER_UNIT):
            qaug_ref[u, g * tq:(g + 1) * tq, HEAD_DIM:] = eye

    def attend_tile(j, carry):
        n_units = N_HEADS // HEADS_PER_UNIT
        kv_of = lambda u: u * HEADS_PER_UNIT // KV_GROUP
        ar = kt // ATTEND_SPLITS
        keys = key_ref[j]
        cut_row = jnp.maximum(j_cut - j * kt, 0)
        lim_row = lim - j * kt
        masks = []
        for h in range(ATTEND_SPLITS):
            kh, ih = keys[h * ar:(h + 1) * ar], row[h * ar:(h + 1) * ar]
            sel = ((kh > thr) | ((kh == thr) & (ih < cut_row))) & (ih < lim_row)
            masks.append(jnp.where(sel, 0.0, MASKED_LOGIT).astype(BF16))
        items = [(h, u) for h in range(ATTEND_SPLITS) for u in range(n_units)]

        def logits(item):
            h, u = item
            kb = kv_tile(k_ref, j, kv_of(u), k_interleaved, h * ar, ar).astype(BF16)
            return _dot_nt(jnp.concatenate([kb, masks[h]], axis=1), qaug_ref[u])

        s_next = logits(items[0])
        for k, (h, u) in enumerate(items):
            s, s_next = s_next, (logits(items[k + 1]) if k + 1 < len(items) else None)
            m_old = m_ref[u]
            m_new = jnp.maximum(m_old, jnp.max(s, axis=0, keepdims=True))
            p = jnp.exp2(s - m_new).astype(BF16)
            vt = vt_ref[j, kv_of(u), :, h * ar:(h + 1) * ar]
            acc_ref[u] = jnp.exp2(m_old - m_new) * acc_ref[u] + _dot(vt, p)
            m_ref[u] = m_new
        return carry

    lax.fori_loop(0, n_tiles, attend_tile, 0)

    for h in range(N_HEADS):
        u, cols = h // HEADS_PER_UNIT, slice((h % HEADS_PER_UNIT) * tq, (h % HEADS_PER_UNIT + 1) * tq)
        o = acc_ref[u, :HEAD_DIM, cols] / acc_ref[u, HEAD_DIM:HEAD_DIM + 1, cols]
        o_ref[0, :, h * HEAD_DIM:(h + 1) * HEAD_DIM] = o.T.astype(BF16)


def _dsa_attention(q, qi, kiwi, k, v, ki_bf, *, past, seq, n_keys, k_interleaved):
    b, tq_pad, _ = kiwi.shape
    l_pad = ki_bf.shape[1]
    n_key_tiles = l_pad // KEY_TILE
    topk = min(TOPK_MAX, n_keys // 4)
    qblk = lambda width: pl.BlockSpec((1, Q_TILE, width), lambda bi, i: (bi, i, 0))
    hmblk = lambda heads: pl.BlockSpec((1, 1, heads * Q_TILE, LANES), lambda bi, i: (bi, i, 0, 0))
    kblk = lambda rows, width: pl.BlockSpec((1, rows, width), lambda bi, i: (bi, 0, 0),
                                            pipeline_mode=pl.Buffered(1))
    cache_blk = kblk(l_pad * N_KV_HEADS, HEAD_DIM)
    kern = functools.partial(_dsa_kernel, past=past, seq=seq, n_keys=n_keys, topk=topk,
                             n_key_tiles=n_key_tiles, k_interleaved=k_interleaved)
    return pl.pallas_call(
        kern,
        grid=(b, tq_pad // Q_TILE),
        in_specs=[hmblk(N_HEADS), hmblk(IDX_HEADS), qblk(LANES),
                  cache_blk if k_interleaved else kblk(l_pad, KV_DIM), cache_blk, kblk(l_pad, LANES)],
        out_specs=qblk(ATTN_DIM),
        out_shape=jax.ShapeDtypeStruct((b, tq_pad, ATTN_DIM), BF16),
        scratch_shapes=[
            pltpu.VMEM((n_key_tiles, N_KV_HEADS, HEAD_DIM + DENOM_ROWS, KEY_TILE), BF16),
            pltpu.VMEM((n_key_tiles, KEY_TILE, Q_TILE), I32),
            pltpu.VMEM((N_HEADS // HEADS_PER_UNIT, HEAD_DIM + DENOM_ROWS, HEADS_PER_UNIT * Q_TILE), F32),
            pltpu.VMEM((N_HEADS // HEADS_PER_UNIT, 1, HEADS_PER_UNIT * Q_TILE), F32),
            pltpu.VMEM((1, Q_TILE), I32),
            pltpu.VMEM((N_HEADS // HEADS_PER_UNIT, HEADS_PER_UNIT * Q_TILE, HEAD_DIM + Q_TILE), BF16),
        ],
        compiler_params=_params("arbitrary", "arbitrary"),
        name="dsa_attention",
    )(q, qi, kiwi, k, v, ki_bf)


def _mixer_kernel(x_ref, attn_ref, memk_ref, memv_ref, prev_a_ref, prev_b_ref,
                  lng_ref, lnb_ref, cw_ref, ln1g_ref, ln1b_ref,
                  wgb_ref, wgc_ref, wci_ref, wmq_ref, wgl_ref, wao_ref, wco_ref, wmo_ref, wo_ref,
                  x1_ref, state_ref, carry_ref, *, seq, rows):
    i = pl.program_id(0)
    multi_seq = seq < rows
    xn = _layer_norm(x_ref[...], lng_ref[...], lnb_ref[...])
    xb = xn.astype(BF16)

    u = _dot(xb, wgc_ref[...]) * _dot(xb, wci_ref[...])
    ridx = lax.broadcasted_iota(I32, u.shape, 0)
    if multi_seq:
        tpos = ridx & (seq - 1)
        s1 = jnp.where(tpos == 0, prev_a_ref[...], pltpu.roll(u, 1, 0))
        s2 = jnp.where(tpos < 2, prev_b_ref[...], pltpu.roll(u, 2, 0))
        state_ref[...] = u
    else:
        blocks_per_seq = seq // rows
        @pl.when(i % blocks_per_seq == 0)
        def _():
            carry_ref[...] = prev_a_ref[0]

        prev = carry_ref[...]
        p1, p2 = prev[SUBLANES - 1:SUBLANES, :], prev[SUBLANES - 2:SUBLANES - 1, :]
        s1 = jnp.where(ridx == 0, p1, pltpu.roll(u, 1, 0))
        s2 = jnp.where(ridx == 0, p2, jnp.where(ridx == 1, p1, pltpu.roll(u, 2, 0)))
        carry_ref[...] = u[rows - SUBLANES:, :]
        state_ref[0] = u[rows - SUBLANES:, :]
    cw = cw_ref[...]
    conv = cw[0:1, :] * s2 + cw[1:2, :] * s1 + cw[2:3, :] * u
    conv_in = (_dot(xb, wgb_ref[...]) * conv).astype(BF16)

    mq = _dot(xb, wmq_ref[...]).astype(BF16)
    n_batches = max(1, rows // seq)
    rb = rows // n_batches
    mem_rows = []
    for bi in range(n_batches):
        heads = []
        for h in range(MEM_HEADS):
            cols = slice(h * MEM_HEAD_DIM, (h + 1) * MEM_HEAD_DIM)
            s = _dot_nt(mq[bi * rb:(bi + 1) * rb, cols], memk_ref[bi, :, cols]) * (MEM_HEAD_DIM ** -0.5)
            e = jnp.exp(s - jnp.max(s, axis=-1, keepdims=True))
            p = e / jnp.sum(e, axis=-1, keepdims=True)
            heads.append(_dot(p.astype(BF16), memv_ref[bi, :, cols]))
        mem_rows.append(jnp.concatenate(heads, axis=1))
    mem = (mem_rows[0] if n_batches == 1 else jnp.concatenate(mem_rows, axis=0)).astype(BF16)

    gate = _sigmoid(_dot(xb, wgl_ref[...]))
    mixed = (gate[:, :D_MODEL] * _dot(attn_ref[...], wao_ref[...])
             + gate[:, D_MODEL:2 * D_MODEL] * _dot(conv_in, wco_ref[...])
             + gate[:, 2 * D_MODEL:] * _dot(mem, wmo_ref[...]))
    y = ALPHA * xn + _dot(mixed.astype(BF16), wo_ref[...])
    x1_ref[...] = _layer_norm(y, ln1g_ref[...], ln1b_ref[...])


def _mixer(x, attn, mem_k, mem_v, prev_a, prev_b, ln_g, ln_b, conv_w8, ln1_g, ln1_b, w, *, seq, rows):
    n = x.shape[0]
    tm = min(rows, n)
    multi_seq = seq < tm
    n_batches = max(1, tm // seq)
    row = lambda width: pl.BlockSpec((tm, width), lambda i: (i, 0))
    mem = pl.BlockSpec((n_batches, N_MEM, MEM_DIM), lambda i: ((i * tm) // (seq * n_batches), 0, 0))
    vec = _resident((1, D_MODEL))
    if multi_seq:
        prev_spec = row(D_CONV)
        state_spec = row(D_CONV)
        state_shape = jax.ShapeDtypeStruct((n, D_CONV), F32)
    else:
        prev_spec = pl.BlockSpec((1, SUBLANES, D_CONV), lambda i: ((i * tm) // seq, 0, 0))
        state_spec = prev_spec
        state_shape = jax.ShapeDtypeStruct((n // seq, SUBLANES, D_CONV), F32)
    weights = [w[k] for k in ("gb", "gc", "ci", "mq", "gl", "ao", "co", "mo", "o")]
    kern = functools.partial(_mixer_kernel, seq=seq, rows=tm)
    return pl.pallas_call(
        kern,
        grid=(n // tm,),
        in_specs=[row(D_MODEL), row(ATTN_DIM), mem, mem, prev_spec, prev_spec,
                  vec, vec, _resident((SUBLANES, D_CONV)), vec, vec] + [_resident(m.shape) for m in weights],
        out_specs=(row(D_MODEL), state_spec),
        out_shape=(jax.ShapeDtypeStruct((n, D_MODEL), F32), state_shape),
        scratch_shapes=[pltpu.VMEM((SUBLANES, D_CONV), F32)],
        compiler_params=_params("arbitrary"),
        name="mixer",
    )(x, attn, mem_k, mem_v, prev_a, prev_b, ln_g, ln_b, conv_w8, ln1_g, ln1_b, *weights)


def _route(x, wr_hi, wr_lo, bias):
    tm = x.shape[0]
    x_hi = x.astype(BF16)
    x_lo = (x - x_hi.astype(F32)).astype(BF16)
    logits = _dot_nt(wr_hi, x_hi) + (_dot_nt(wr_hi, x_lo) + _dot_nt(wr_lo, x_hi))
    s = _sigmoid(logits)
    sc = s + bias[:, 0:1]
    sub = lax.broadcasted_iota(I32, (GROUP_SIZE, tm), 0)
    neg = -jnp.inf
    groups, gscore = [], []
    for g in range(N_GROUPS):
        xg = sc[g * GROUP_SIZE:(g + 1) * GROUP_SIZE, :]
        m1 = jnp.max(xg, axis=0, keepdims=True)
        i1 = jnp.min(jnp.where(xg == m1, sub, GROUP_SIZE), axis=0, keepdims=True)
        m2 = jnp.max(jnp.where(sub == i1, neg, xg), axis=0, keepdims=True)
        groups.append(xg)
        gscore.append(m1 + m2)
    masked = []
    for g in range(N_GROUPS):
        rank = jnp.zeros((1, tm), I32)
        for o in range(N_GROUPS):
            if o == g:
                continue
            beats = (gscore[o] > gscore[g]) | ((gscore[o] == gscore[g]) & (o < g))
            rank = rank + beats.astype(I32)
        masked.append(jnp.where(rank < TOPK_GROUPS, groups[g], neg))
    v = jnp.concatenate(masked, axis=0)
    eidx = lax.broadcasted_iota(I32, (N_EXPERTS, tm), 0)
    chosen = jnp.zeros((N_EXPERTS, tm), jnp.bool_)
    for _ in range(TOP_K):
        m = jnp.max(v, axis=0, keepdims=True)
        first = jnp.min(jnp.where(v == m, eidx, N_EXPERTS), axis=0, keepdims=True)
        hit = eidx == first
        chosen = chosen | hit
        v = jnp.where(hit, neg, v)
    wsel = jnp.where(chosen, s, 0.0)
    return wsel / jnp.sum(wsel, axis=0, keepdims=True) * ROUTED_SCALE


def _expert_ffn(xb, w_gu, w_down, gate_col=None):
    gu = _dot(xb, w_gu)
    g, u = gu[:, :D_EXPERT], gu[:, D_EXPERT:]
    h = g * _sigmoid(g) * u
    if gate_col is not None:
        h = h * gate_col
    return _dot(h.astype(BF16), w_down)


def _moe_kernel(x_ref, wrh_ref, wrl_ref, br_ref, wgu_ref, wd_ref, wsgu_ref, wsd_ref, g_ref, b_ref,
                o_ref, xb_ref, gate_ref, acc_ref, *, experts_per_step):
    j = pl.program_id(1)

    @pl.when(j == 0)
    def _():
        x = x_ref[...]
        gate_t = _route(x, wrh_ref[...], wrl_ref[...], br_ref[...])
        pad = jnp.zeros((LANES - N_EXPERTS, x.shape[0]), F32)
        gate_ref[...] = jnp.concatenate([gate_t, pad], axis=0).T
        xb = x.astype(BF16)
        xb_ref[...] = xb
        acc_ref[...] = _expert_ffn(xb, wsgu_ref[...], wsd_ref[...])

    xb = xb_ref[...]
    gate = gate_ref[...]
    lane = lax.broadcasted_iota(I32, gate.shape, 1)
    acc = acc_ref[...]
    for e in range(experts_per_step):
        col = jnp.sum(jnp.where(lane == j * experts_per_step + e, gate, 0.0), axis=1, keepdims=True)
        acc = acc + _expert_ffn(xb, wgu_ref[e], wd_ref[e], col)
    acc_ref[...] = acc

    @pl.when(j == pl.num_programs(1) - 1)
    def _():
        o_ref[...] = _layer_norm(ALPHA * x_ref[...] + acc_ref[...], g_ref[...], b_ref[...])


def _moe(x, w, ln_g, ln_b, rows):
    n = x.shape[0]
    tm = min(rows, n)
    eps = MOE_EXPERTS_PER_STEP
    row = pl.BlockSpec((tm, D_MODEL), lambda i, j: (i, 0))
    kern = functools.partial(_moe_kernel, experts_per_step=eps)
    return pl.pallas_call(
        kern,
        grid=(n // tm, N_EXPERTS // eps),
        in_specs=[row, _resident(w["r_hi"].shape), _resident(w["r_lo"].shape), _resident(w["r_b"].shape),
                  pl.BlockSpec((eps, D_MODEL, 2 * D_EXPERT), lambda i, j: (j, 0, 0)),
                  pl.BlockSpec((eps, D_EXPERT, D_MODEL), lambda i, j: (j, 0, 0)),
                  _resident(w["s_gu"].shape), _resident(w["s_d"].shape),
                  _resident((1, D_MODEL)), _resident((1, D_MODEL))],
        out_specs=row,
        out_shape=jax.ShapeDtypeStruct((n, D_MODEL), F32),
        scratch_shapes=[pltpu.VMEM((tm, D_MODEL), BF16), pltpu.VMEM((tm, LANES), F32),
                        pltpu.VMEM((tm, D_MODEL), F32)],
        compiler_params=_params("arbitrary", "arbitrary"),
        name="moe",
    )(x, w["r_hi"], w["r_lo"], w["r_b"], w["e_gu"], w["e_d"], w["s_gu"], w["s_d"], ln_g, ln_b)


def _pad_rows(a, rows):
    return a if a.shape[1] == rows else jnp.pad(a, ((0, 0), (0, rows - a.shape[1]), (0, 0)))


def _head_major_per_batch(x, b, seq, tq_pad):
    heads = x.shape[1]
    if seq % Q_TILE == 0:
        return x.reshape(b, seq // Q_TILE, heads * Q_TILE, LANES)
    per_tile = Q_TILE // seq
    x = x.reshape(-1, heads, per_tile, seq, LANES).transpose(0, 2, 1, 3, 4).reshape(b, heads, seq, LANES)
    x = jnp.pad(x, ((0, 0), (0, 0), (0, tq_pad - seq), (0, 0)))
    return x.reshape(b, 1, heads * tq_pad, LANES)


def _layer(x, mem_k_bf, mem_v_bf, past_k, past_v, past_ki, conv_prev, w, ln, layer):
    b, seq, _ = x.shape
    past = past_k.shape[1]
    n = b * seq
    n_keys = past + seq
    x2 = x.reshape(n, D_MODEL)

    pos = past + jnp.arange(seq, dtype=I32)
    reps = max(1, min(PROJ_ROWS, n) // seq)
    pos_t = jnp.tile(pos, reps)
    tabs = (_rope_tables(pos_t, HEAD_DIM // ROT_FRACTION // 2)
            + _rope_tables(pos_t, IDX_DIM // ROT_FRACTION // 2,
                           (IDX_DIM, IDX_DIM + IDX_HEADS), IDX_HEADS ** -0.5))
    q, qi, k, v, kiwi, k_bf, ki_bf = _qkv_project(x2, ln["in_g"], ln["in_b"], w, tabs, PROJ_ROWS, seq)

    l_pad = -(-n_keys // KEY_TILE) * KEY_TILE
    tq_pad = -(-seq // Q_TILE) * Q_TILE
    rows = lambda a, length: a.reshape(b, length * N_KV_HEADS, HEAD_DIM)
    with_cache = lambda new, cache: new if past == 0 else jnp.concatenate([rows(cache, past), new], axis=1)
    v3 = _pad_rows(with_cache(rows(v, seq), past_v), l_pad * N_KV_HEADS)
    k_interleaved = past > 0
    if k_interleaved:
        k3 = _pad_rows(with_cache(rows(k, seq), past_k), l_pad * N_KV_HEADS)
    else:
        k3 = _pad_rows(k_bf.reshape(b, seq, KV_DIM), l_pad)
    ki_past = jnp.pad(past_ki, ((0, 0), (0, 0), (0, LANES - IDX_DIM))).astype(BF16)
    ki3 = jnp.concatenate([ki_past, ki_bf.reshape(b, seq, LANES)], axis=1)
    attn = _dsa_attention(
        _head_major_per_batch(q, b, seq, tq_pad), _head_major_per_batch(qi, b, seq, tq_pad),
        _pad_rows(kiwi.reshape(b, seq, LANES), tq_pad), k3, v3, _pad_rows(ki3, l_pad),
        past=past, seq=seq, n_keys=n_keys, k_interleaved=k_interleaved)
    attn = attn[:, :seq].reshape(n, ATTN_DIM)

    tm = min(PROJ_ROWS, n)
    if seq < tm:
        tpos = jnp.arange(seq)[None, :, None]
        prev_a = jnp.where(tpos == 0, conv_prev[:, 1:2], 0.0).reshape(n, D_CONV)
        prev_b = jnp.where(tpos == 0, conv_prev[:, 0:1], jnp.where(tpos == 1, conv_prev[:, 1:2], 0.0)).reshape(n, D_CONV)
    else:
        prev_a = jnp.pad(conv_prev, ((0, 0), (SUBLANES - (CONV_WIDTH - 1), 0), (0, 0)))
        prev_b = prev_a
    x1, state = _mixer(x2, attn, mem_k_bf, mem_v_bf, prev_a, prev_b, ln["in_g"], ln["in_b"], w["conv"],
                       ln["g1"], ln["b1"], w, seq=seq, rows=PROJ_ROWS)
    if seq < tm:
        new_conv = state.reshape(b, seq, D_CONV)[:, seq - (CONV_WIDTH - 1):]
    else:
        new_conv = state[:, SUBLANES - (CONV_WIDTH - 1):]

    y = _moe(x1, w, ln["g2"], ln["b2"], MOE_ROWS)
    return (y.reshape(b, seq, D_MODEL), k.reshape(b, seq, N_KV_HEADS, HEAD_DIM),
            v.reshape(b, seq, N_KV_HEADS, HEAD_DIM), kiwi[:, :IDX_DIM].reshape(b, seq, IDX_DIM), new_conv)


def _prep_weights(l, w_in, conv_w, w_attn_out, w_conv_out, w_mem_out, w_o, w_router, b_router,
                  w_exp_gu, w_exp_down, w_sh_gu, w_sh_down):
    sizes = [ATTN_DIM, KV_DIM, KV_DIM, IDX_HEADS * IDX_DIM, IDX_DIM, IDX_HEADS,
             D_CONV, D_CONV, D_CONV, MEM_DIM, N_BRANCH * D_MODEL]
    offs = np.concatenate([[0], np.cumsum(sizes)]).tolist()
    wi = w_in[l]
    part = lambda j: wi[:, offs[j]:offs[j + 1]]
    w_qi = part(3).reshape(D_MODEL, IDX_HEADS, IDX_DIM)
    w_qi = jnp.pad(w_qi, ((0, 0), (0, 0), (0, LANES - IDX_DIM))).reshape(D_MODEL, IDX_HEADS * LANES)
    w_kw = jnp.pad(jnp.concatenate([part(4), part(5)], axis=1), ((0, 0), (0, LANES - IDX_DIM - IDX_HEADS)))
    wr = w_router[l].T
    wr_hi = wr.astype(BF16)
    return {
        "q": part(0).astype(BF16), "kv": jnp.concatenate([part(1), part(2)], axis=1).astype(BF16),
        "qi": w_qi.astype(BF16), "kw": w_kw.astype(BF16),
        "gb": part(6).astype(BF16), "gc": part(7).astype(BF16), "ci": part(8).astype(BF16),
        "mq": part(9).astype(BF16), "gl": part(10).astype(BF16),
        "ao": w_attn_out[l].astype(BF16), "co": w_conv_out[l].astype(BF16), "mo": w_mem_out[l].astype(BF16),
        "o": w_o[l].astype(BF16),
        "conv": jnp.pad(conv_w[l], ((0, SUBLANES - CONV_WIDTH), (0, 0))),
        "r_hi": wr_hi, "r_lo": (wr - wr_hi.astype(F32)).astype(BF16),
        "r_b": jnp.broadcast_to(b_router[l][:, None], (N_EXPERTS, LANES)),
        "e_gu": w_exp_gu[l].astype(BF16), "e_d": w_exp_down[l].astype(BF16),
        "s_gu": w_sh_gu[l].astype(BF16), "s_d": w_sh_down[l].astype(BF16),
    }


def kernel(x_prompt, x_sample, mem_prompt, cache_attn_k, cache_attn_v, cache_idx_k, cache_mem_k, cache_mem_v,
           state_conv, ln_in_g, ln_in_b, w_in, conv_w, w_mem_kv, w_attn_out, w_conv_out, w_mem_out, w_o,
           ln1_g, ln1_b, w_router, b_router, w_exp_gu, w_exp_down, w_sh_gu, w_sh_down, ln2_g, ln2_b):
    assert DEPTH == 1 and w_in.shape[0] == 1
    l = 0
    bp, n_mem = mem_prompt.shape[0], mem_prompt.shape[1]
    w = _prep_weights(l, w_in, conv_w, w_attn_out, w_conv_out, w_mem_out, w_o, w_router, b_router,
                      w_exp_gu, w_exp_down, w_sh_gu, w_sh_down)
    ln = {"in_g": ln_in_g[None, :], "in_b": ln_in_b[None, :], "g1": ln1_g[l][None, :], "b1": ln1_b[l][None, :],
          "g2": ln2_g[l][None, :], "b2": ln2_b[l][None, :]}

    mem_kv = _project(mem_prompt.reshape(bp * n_mem, D_MODEL), w_mem_kv[l].astype(BF16))
    mem_k = mem_kv[:, :MEM_DIM].reshape(bp, n_mem, MEM_DIM)
    mem_v = mem_kv[:, MEM_DIM:].reshape(bp, n_mem, MEM_DIM)

    zeros = lambda *s: jnp.zeros(s, F32)
    yp, kp, vp, kip, cp = _layer(
        x_prompt, mem_k.astype(BF16), mem_v.astype(BF16),
        zeros(bp, 0, N_KV_HEADS, HEAD_DIM), zeros(bp, 0, N_KV_HEADS, HEAD_DIM), zeros(bp, 0, IDX_DIM),
        zeros(bp, CONV_WIDTH - 1, D_CONV), w, ln, l)
    bs = x_sample.shape[0]
    ys, ks, vs, kis, cs = _layer(
        x_sample, cache_mem_k[l].reshape(bs, n_mem, MEM_DIM).astype(BF16),
        cache_mem_v[l].reshape(bs, n_mem, MEM_DIM).astype(BF16),
        cache_attn_k[l], cache_attn_v[l], cache_idx_k[l], state_conv[l], w, ln, l)

    mem_shape = (1, bp, n_mem, MEM_HEADS, MEM_HEAD_DIM)
    return (yp, ys, kp[None], vp[None], kip[None], mem_k.reshape(mem_shape), mem_v.reshape(mem_shape), cp[None],
            ks[None], vs[None], kis[None], cs[None])
```

```python
import functools

import jax
import jax.numpy as jnp
import numpy as np
from jax import lax
from jax.experimental import pallas as pl
from jax.experimental.pallas import tpu as pltpu

D_MODEL = 1024
DEPTH = 1
CHUNK = 64
HEAD_DIM = 128
N_HEADS = D_MODEL // HEAD_DIM
N_KV_HEADS = max(1, N_HEADS // 4)
KV_GROUP = N_HEADS // N_KV_HEADS
ATTN_DIM = N_HEADS * HEAD_DIM
KV_DIM = N_KV_HEADS * HEAD_DIM
ROT_FRACTION = 4
ROPE_THETA = 500000.0
IDX_HEADS = 8
IDX_DIM = 64
TOPK_MAX = 256
D_CONV = D_MODEL
CONV_WIDTH = 3
N_MEM = 256
MEM_HEADS = 4
MEM_HEAD_DIM = D_MODEL // MEM_HEADS
MEM_DIM = MEM_HEADS * MEM_HEAD_DIM
N_BRANCH = 3
N_EXPERTS = 64
TOP_K = 8
N_GROUPS = 8
GROUP_SIZE = N_EXPERTS // N_GROUPS
TOPK_GROUPS = 4
D_EXPERT = D_MODEL // 4
ROUTED_SCALE = 2.5
LN_EPS = 1e-5
ALPHA = (2.0 * DEPTH) ** 0.25

LANES = 128
SUBLANES = 8
VMEM_LIMIT_BYTES = 56 * 1024 * 1024

Q_TILE = 128
KEY_TILE = 512
HEADS_PER_UNIT = 2
COUNT_ROWS = 64
DENOM_ROWS = 16
PROJ_ROWS = 512
MOE_ROWS = 1024
MOE_EXPERTS_PER_STEP = 4

F32 = jnp.float32
BF16 = jnp.bfloat16
I32 = jnp.int32
INT_MIN = -(2 ** 31)
INT_MAX = 2 ** 31 - 1
MASKED_LOGIT = -1e30
QK_SCALE_LOG2E = HEAD_DIM ** -0.5 * float(np.log2(np.e))

_NT = (((1,), (1,)), ((), ()))


def _dot(a, b):
    return jnp.dot(a, b, preferred_element_type=F32)


def _dot_nt(a, b):
    return lax.dot_general(a, b, _NT, preferred_element_type=F32)


def _layer_norm(x, g, b):
    mu = jnp.mean(x, axis=-1, keepdims=True)
    xc = x - mu
    var = jnp.mean(xc * xc, axis=-1, keepdims=True)
    return xc * lax.rsqrt(var + LN_EPS) * g + b


def _sigmoid(x):
    return 1.0 / (1.0 + jnp.exp(-x))


def _resident(shape):
    nd = len(shape)
    return pl.BlockSpec(shape, lambda *_: (0,) * nd, pipeline_mode=pl.Buffered(1))


def _params(*sem):
    return pltpu.CompilerParams(dimension_semantics=sem, vmem_limit_bytes=VMEM_LIMIT_BYTES)


def _rope_tables(pos, half, extra_scale_lanes=None, extra_scale=1.0):
    inv_freq = ROPE_THETA ** (-np.arange(half, dtype=np.float64) / half)
    ang = pos.astype(np.float64)[:, None] * inv_freq[None, :]
    cos, sin = np.cos(ang), np.sin(ang)
    n = pos.shape[0]
    c = np.concatenate([cos, cos, np.ones((n, LANES - 2 * half))], axis=1)
    s = np.concatenate([-sin, sin, np.zeros((n, LANES - 2 * half))], axis=1)
    if extra_scale_lanes is not None:
        lo, hi = extra_scale_lanes
        c[:, lo:hi] = extra_scale
    return jnp.asarray(c, F32), jnp.asarray(s, F32)


def _rope(x, c, s, half):
    rows, width = x.shape
    lane = lax.broadcasted_iota(I32, (rows, LANES), 1)
    first = lane < half
    outs = []
    for j in range(width // LANES):
        xj = x[:, j * LANES:(j + 1) * LANES]
        swap = jnp.where(first, pltpu.roll(xj, LANES - half, 1), pltpu.roll(xj, half, 1))
        outs.append(xj * c + swap * s)
    return outs[0] if len(outs) == 1 else jnp.concatenate(outs, axis=1)


def _proj_kernel(x_ref, w_ref, o_ref):
    o_ref[...] = _dot(x_ref[...].astype(BF16), w_ref[...])


def _project(x, w_bf, col_tile=512):
    n, d = x.shape
    m = w_bf.shape[1]
    return pl.pallas_call(
        _proj_kernel,
        grid=(m // col_tile,),
        in_specs=[pl.BlockSpec((n, d), lambda j: (0, 0)),
                  pl.BlockSpec((d, col_tile), lambda j: (0, j))],
        out_specs=pl.BlockSpec((n, col_tile), lambda j: (0, j)),
        out_shape=jax.ShapeDtypeStruct((n, m), F32),
        compiler_params=_params("arbitrary"),
        name="mem_kv_proj",
    )(x, w_bf)


def _store_head_major(ref, x):
    for r in range(ref.shape[0]):
        for h in range(ref.shape[1]):
            ref[r, h] = x[r * Q_TILE:(r + 1) * Q_TILE, h * LANES:(h + 1) * LANES]


def _qkv_kernel(x_ref, g_ref, b_ref, wq_ref, wqi_ref, wkv_ref, wkw_ref, ch_ref, sh_ref, ci_ref, si_ref,
                q_ref, qi_ref, k_ref, v_ref, kiwi_ref, kbf_ref, kibf_ref):
    xb = _layer_norm(x_ref[...], g_ref[...], b_ref[...]).astype(BF16)
    ch, sh, ci, si = ch_ref[...], sh_ref[...], ci_ref[...], si_ref[...]
    q = _rope(_dot(xb, wq_ref[...]), ch, sh, HEAD_DIM // ROT_FRACTION // 2)
    _store_head_major(q_ref, (q * QK_SCALE_LOG2E).astype(BF16))
    qi = _rope(_dot(xb, wqi_ref[...]), ci, si, IDX_DIM // ROT_FRACTION // 2)
    _store_head_major(qi_ref, qi.astype(BF16))
    kv = _dot(xb, wkv_ref[...])
    k = _rope(kv[:, :KV_DIM], ch, sh, HEAD_DIM // ROT_FRACTION // 2)
    kbf_ref[...] = k.astype(BF16)
    rows = k.shape[0]
    for n in range(N_KV_HEADS):
        k_ref[pl.ds(n, rows, stride=N_KV_HEADS), :] = k[:, n * HEAD_DIM:(n + 1) * HEAD_DIM]
        v_ref[pl.ds(n, rows, stride=N_KV_HEADS), :] = kv[:, KV_DIM + n * HEAD_DIM:KV_DIM + (n + 1) * HEAD_DIM]
    kw = _rope(_dot(xb, wkw_ref[...]), ci, si, IDX_DIM // ROT_FRACTION // 2)
    kiwi_ref[...] = kw
    kibf_ref[...] = kw.astype(BF16)


def _qkv_project(x, ln_g, ln_b, w, tabs, rows, seq):
    n = x.shape[0]
    tm = min(rows, n)
    tab_blocks = max(seq, tm) // tm
    row = lambda width: pl.BlockSpec((tm, width), lambda i: (i, 0))
    tab = pl.BlockSpec((tm, LANES), lambda i: (i % tab_blocks, 0))
    cache_rows = pl.BlockSpec((tm * N_KV_HEADS, HEAD_DIM), lambda i: (i, 0))
    hm_shape = lambda heads: (n // Q_TILE, heads, Q_TILE, LANES)
    hm = lambda heads: pl.BlockSpec((tm // Q_TILE, heads, Q_TILE, LANES), lambda i: (i, 0, 0, 0))
    out_shapes = (
        jax.ShapeDtypeStruct(hm_shape(N_HEADS), BF16),
        jax.ShapeDtypeStruct(hm_shape(IDX_HEADS), BF16),
        jax.ShapeDtypeStruct((n * N_KV_HEADS, HEAD_DIM), F32),
        jax.ShapeDtypeStruct((n * N_KV_HEADS, HEAD_DIM), F32),
        jax.ShapeDtypeStruct((n, LANES), F32),
        jax.ShapeDtypeStruct((n, KV_DIM), BF16),
        jax.ShapeDtypeStruct((n, LANES), BF16),
    )
    return pl.pallas_call(
        _qkv_kernel,
        grid=(n // tm,),
        in_specs=[row(D_MODEL), _resident((1, D_MODEL)), _resident((1, D_MODEL)),
                  _resident(w["q"].shape), _resident(w["qi"].shape), _resident(w["kv"].shape),
                  _resident(w["kw"].shape), tab, tab, tab, tab],
        out_specs=(hm(N_HEADS), hm(IDX_HEADS), cache_rows, cache_rows, row(LANES),
                   row(KV_DIM), row(LANES)),
        out_shape=out_shapes,
        compiler_params=_params("arbitrary"),
        name="qkv_proj",
    )(x, ln_g, ln_b, w["q"], w["qi"], w["kv"], w["kw"], *tabs)


def _sortable_key(score):
    bits = lax.bitcast_convert_type(score + 0.0, I32)
    return bits ^ ((bits >> 31) & INT_MAX)


def _dsa_kernel(q_ref, qi_ref, kiwi_ref, k_ref, v_ref, ki_ref, o_ref,
                vt_ref, key_ref, acc_ref, m_ref, j_ref,
                *, past, seq, n_keys, topk, n_key_tiles, k_interleaved):
    i = pl.program_id(1)
    tq, kt = Q_TILE, KEY_TILE
    uq = HEADS_PER_UNIT * tq

    def kv_tile(ref, j, n, interleaved):
        if interleaved:
            return ref[0, pl.ds(j * (kt * N_KV_HEADS) + n, kt, stride=N_KV_HEADS), :]
        return ref[0, pl.ds(pl.multiple_of(j * kt, kt), kt), n * HEAD_DIM:(n + 1) * HEAD_DIM]

    @pl.when(i == 0)
    def _():
        ones = jnp.ones((DENOM_ROWS, kt), BF16)
        for c in range(n_key_tiles):
            for n in range(N_KV_HEADS):
                vt_ref[c, n, :HEAD_DIM, :] = kv_tile(v_ref, c, n, True).T.astype(BF16)
                vt_ref[c, n, HEAD_DIM:, :] = ones

    lane = lax.broadcasted_iota(I32, (1, tq), 1)
    t = i * tq + lane
    q_real = t < seq
    lim = jnp.minimum(((past + t) // CHUNK + 1) * CHUNK, n_keys)
    lim_max = jnp.minimum(((past + i * tq + tq - 1) // CHUNK + 1) * CHUNK, n_keys)
    n_tiles = (lim_max + kt - 1) // kt
    row = lax.broadcasted_iota(I32, (kt, tq), 0)

    wi = kiwi_ref[0].T[IDX_DIM:IDX_DIM + IDX_HEADS, :] * (IDX_DIM ** -0.5)

    def score_tile(j, n_nonneg):
        kib = ki_ref[0, pl.ds(pl.multiple_of(j * kt, kt), kt), :]
        rel_of = lambda u: _dot_nt(kib, qi_ref[0, 0, u * uq:(u + 1) * uq, :])
        n_units = IDX_HEADS // HEADS_PER_UNIT
        acc = jnp.zeros((kt, tq), F32)
        rel_next = rel_of(0)
        for u in range(n_units):
            rel, rel_next = rel_next, (rel_of(u + 1) if u + 1 < n_units else None)
            for g in range(HEADS_PER_UNIT):
                h = u * HEADS_PER_UNIT + g
                acc = acc + jnp.maximum(rel[:, g * tq:(g + 1) * tq], 0.0) * wi[h:h + 1, :]
        score = jnp.where(row < lim - j * kt, acc, -jnp.inf)
        key = _sortable_key(score)
        key_ref[j] = key
        for r in range(kt // COUNT_ROWS):
            n_nonneg = jnp.where(key[r * COUNT_ROWS:(r + 1) * COUNT_ROWS] >= 0, n_nonneg + 1, n_nonneg)
        return n_nonneg

    n_nonneg = lax.fori_loop(0, n_tiles, score_tile, jnp.zeros((COUNT_ROWS, tq), I32))

    def count(pred):
        def body(j, acc):
            blk = key_ref[j]
            for r in range(kt // COUNT_ROWS):
                rows = slice(r * COUNT_ROWS, (r + 1) * COUNT_ROWS)
                acc = jnp.where(pred(blk[rows], j * kt + row[rows]), acc + 1, acc)
            return acc
        acc = lax.fori_loop(0, n_tiles, body, jnp.zeros((COUNT_ROWS, tq), I32))
        return jnp.sum(acc, axis=0, keepdims=True)

    def count_ge(cand):
        return count(lambda keys, kidx: keys >= cand)

    def bisect(b, state):
        thr, cnt = state
        cand = thr | jnp.left_shift(jnp.int32(1), 30 - b)
        c = count_ge(cand)
        take = c >= topk
        return jnp.where(take, cand, thr), jnp.where(take, c, cnt)

    c0 = jnp.sum(n_nonneg, axis=0, keepdims=True)
    nonneg = c0 >= topk
    thr, cnt = lax.fori_loop(0, 31, bisect, (jnp.where(nonneg, 0, INT_MIN).astype(I32),
                                             jnp.where(nonneg, c0, n_tiles * kt).astype(I32)))
    tied = jnp.max(jnp.where(q_real, cnt, topk)) > topk

    j_ref[...] = jnp.full((1, tq), INT_MAX, I32)

    @pl.when(tied)
    def _():
        n_ties_wanted = topk - count(lambda keys, kidx: keys > thr)
        n_bits = int(n_key_tiles * kt).bit_length()

        def bisect_idx(it, cut):
            cand = cut | jnp.left_shift(jnp.int32(1), n_bits - 1 - it)
            c = count(lambda keys, kidx: (keys == thr) & (kidx < cand))
            return jnp.where(c <= n_ties_wanted, cand, cut)

        j_ref[...] = lax.fori_loop(0, n_bits, bisect_idx, jnp.zeros((1, tq), I32))

    j_cut = j_ref[...]

    m_ref[...] = jnp.full(m_ref.shape, MASKED_LOGIT, F32)
    acc_ref[...] = jnp.zeros(acc_ref.shape, F32)

    def attend_tile(j, carry):
        n_units = N_HEADS // HEADS_PER_UNIT
        kv_of = lambda u: u * HEADS_PER_UNIT // KV_GROUP
        keys = key_ref[j]
        sel = ((keys > thr) | ((keys == thr) & (row < j_cut - j * kt))) & (row < lim - j * kt)

        def logits(u):
            kb = kv_tile(k_ref, j, kv_of(u), k_interleaved).astype(BF16)
            return _dot_nt(kb, q_ref[0, 0, u * uq:(u + 1) * uq, :])

        s_next = logits(0)
        for u in range(n_units):
            s, s_next = s_next, (logits(u + 1) if u + 1 < n_units else None)
            s = jnp.concatenate([jnp.where(sel, s[:, g * tq:(g + 1) * tq], MASKED_LOGIT)
                                 for g in range(HEADS_PER_UNIT)], axis=1)
            m_old = m_ref[u]
            m_new = jnp.maximum(m_old, jnp.max(s, axis=0, keepdims=True))
            p = jnp.exp2(s - m_new).astype(BF16)
            acc_ref[u] = jnp.exp2(m_old - m_new) * acc_ref[u] + _dot(vt_ref[j, kv_of(u)], p)
            m_ref[u] = m_new
        return carry

    lax.fori_loop(0, n_tiles, attend_tile, 0)

    for h in range(N_HEADS):
        u, cols = h // HEADS_PER_UNIT, slice((h % HEADS_PER_UNIT) * tq, (h % HEADS_PER_UNIT + 1) * tq)
        o = acc_ref[u, :HEAD_DIM, cols] / acc_ref[u, HEAD_DIM:HEAD_DIM + 1, cols]
        o_ref[0, :, h * HEAD_DIM:(h + 1) * HEAD_DIM] = o.T.astype(BF16)


def _dsa_attention(q, qi, kiwi, k, v, ki_bf, *, past, seq, n_keys, k_interleaved):
    b, tq_pad, _ = kiwi.shape
    l_pad = ki_bf.shape[1]
    n_key_tiles = l_pad // KEY_TILE
    topk = min(TOPK_MAX, n_keys // 4)
    qblk = lambda width: pl.BlockSpec((1, Q_TILE, width), lambda bi, i: (bi, i, 0))
    hmblk = lambda heads: pl.BlockSpec((1, 1, heads * Q_TILE, LANES), lambda bi, i: (bi, i, 0, 0))
    kblk = lambda rows, width: pl.BlockSpec((1, rows, width), lambda bi, i: (bi, 0, 0),
                                            pipeline_mode=pl.Buffered(1))
    cache_blk = kblk(l_pad * N_KV_HEADS, HEAD_DIM)
    kern = functools.partial(_dsa_kernel, past=past, seq=seq, n_keys=n_keys, topk=topk,
                             n_key_tiles=n_key_tiles, k_interleaved=k_interleaved)
    return pl.pallas_call(
        kern,
        grid=(b, tq_pad // Q_TILE),
        in_specs=[hmblk(N_HEADS), hmblk(IDX_HEADS), qblk(LANES),
                  cache_blk if k_interleaved else kblk(l_pad, KV_DIM), cache_blk, kblk(l_pad, LANES)],
        out_specs=qblk(ATTN_DIM),
        out_shape=jax.ShapeDtypeStruct((b, tq_pad, ATTN_DIM), BF16),
        scratch_shapes=[
            pltpu.VMEM((n_key_tiles, N_KV_HEADS, HEAD_DIM + DENOM_ROWS, KEY_TILE), BF16),
            pltpu.VMEM((n_key_tiles, KEY_TILE, Q_TILE), I32),
            pltpu.VMEM((N_HEADS // HEADS_PER_UNIT, HEAD_DIM + DENOM_ROWS, HEADS_PER_UNIT * Q_TILE), F32),
            pltpu.VMEM((N_HEADS // HEADS_PER_UNIT, 1, HEADS_PER_UNIT * Q_TILE), F32),
            pltpu.VMEM((1, Q_TILE), I32),
        ],
        compiler_params=_params("arbitrary", "arbitrary"),
        name="dsa_attention",
    )(q, qi, kiwi, k, v, ki_bf)


def _mixer_kernel(x_ref, attn_ref, memk_ref, memv_ref, prev_a_ref, prev_b_ref,
                  lng_ref, lnb_ref, cw_ref, ln1g_ref, ln1b_ref,
                  wgb_ref, wgc_ref, wci_ref, wmq_ref, wgl_ref, wao_ref, wco_ref, wmo_ref, wo_ref,
                  x1_ref, state_ref, carry_ref, *, seq, rows):
    i = pl.program_id(0)
    multi_seq = seq < rows
    xn = _layer_norm(x_ref[...], lng_ref[...], lnb_ref[...])
    xb = xn.astype(BF16)

    u = _dot(xb, wgc_ref[...]) * _dot(xb, wci_ref[...])
    ridx = lax.broadcasted_iota(I32, u.shape, 0)
    if multi_seq:
        tpos = ridx & (seq - 1)
        s1 = jnp.where(tpos == 0, prev_a_ref[...], pltpu.roll(u, 1, 0))
        s2 = jnp.where(tpos < 2, prev_b_ref[...], pltpu.roll(u, 2, 0))
        state_ref[...] = u
    else:
        blocks_per_seq = seq // rows
        @pl.when(i % blocks_per_seq == 0)
        def _():
            carry_ref[...] = prev_a_ref[0]

        prev = carry_ref[...]
        p1, p2 = prev[SUBLANES - 1:SUBLANES, :], prev[SUBLANES - 2:SUBLANES - 1, :]
        s1 = jnp.where(ridx == 0, p1, pltpu.roll(u, 1, 0))
        s2 = jnp.where(ridx == 0, p2, jnp.where(ridx == 1, p1, pltpu.roll(u, 2, 0)))
        carry_ref[...] = u[rows - SUBLANES:, :]
        state_ref[0] = u[rows - SUBLANES:, :]
    cw = cw_ref[...]
    conv = cw[0:1, :] * s2 + cw[1:2, :] * s1 + cw[2:3, :] * u
    conv_in = (_dot(xb, wgb_ref[...]) * conv).astype(BF16)

    mq = _dot(xb, wmq_ref[...]).astype(BF16)
    n_batches = max(1, rows // seq)
    rb = rows // n_batches
    mem_rows = []
    for bi in range(n_batches):
        heads = []
        for h in range(MEM_HEADS):
            cols = slice(h * MEM_HEAD_DIM, (h + 1) * MEM_HEAD_DIM)
            s = _dot_nt(mq[bi * rb:(bi + 1) * rb, cols], memk_ref[bi, :, cols]) * (MEM_HEAD_DIM ** -0.5)
            e = jnp.exp(s - jnp.max(s, axis=-1, keepdims=True))
            p = e / jnp.sum(e, axis=-1, keepdims=True)
            heads.append(_dot(p.astype(BF16), memv_ref[bi, :, cols]))
        mem_rows.append(jnp.concatenate(heads, axis=1))
    mem = (mem_rows[0] if n_batches == 1 else jnp.concatenate(mem_rows, axis=0)).astype(BF16)

    gate = _sigmoid(_dot(xb, wgl_ref[...]))
    mixed = (gate[:, :D_MODEL] * _dot(attn_ref[...], wao_ref[...])
             + gate[:, D_MODEL:2 * D_MODEL] * _dot(conv_in, wco_ref[...])
             + gate[:, 2 * D_MODEL:] * _dot(mem, wmo_ref[...]))
    y = ALPHA * xn + _dot(mixed.astype(BF16), wo_ref[...])
    x1_ref[...] = _layer_norm(y, ln1g_ref[...], ln1b_ref[...])


def _mixer(x, attn, mem_k, mem_v, prev_a, prev_b, ln_g, ln_b, conv_w8, ln1_g, ln1_b, w, *, seq, rows):
    n = x.shape[0]
    tm = min(rows, n)
    multi_seq = seq < tm
    n_batches = max(1, tm // seq)
    row = lambda width: pl.BlockSpec((tm, width), lambda i: (i, 0))
    mem = pl.BlockSpec((n_batches, N_MEM, MEM_DIM), lambda i: ((i * tm) // (seq * n_batches), 0, 0))
    vec = _resident((1, D_MODEL))
    if multi_seq:
        prev_spec = row(D_CONV)
        state_spec = row(D_CONV)
        state_shape = jax.ShapeDtypeStruct((n, D_CONV), F32)
    else:
        prev_spec = pl.BlockSpec((1, SUBLANES, D_CONV), lambda i: ((i * tm) // seq, 0, 0))
        state_spec = prev_spec
        state_shape = jax.ShapeDtypeStruct((n // seq, SUBLANES, D_CONV), F32)
    weights = [w[k] for k in ("gb", "gc", "ci", "mq", "gl", "ao", "co", "mo", "o")]
    kern = functools.partial(_mixer_kernel, seq=seq, rows=tm)
    return pl.pallas_call(
        kern,
        grid=(n // tm,),
        in_specs=[row(D_MODEL), row(ATTN_DIM), mem, mem, prev_spec, prev_spec,
                  vec, vec, _resident((SUBLANES, D_CONV)), vec, vec] + [_resident(m.shape) for m in weights],
        out_specs=(row(D_MODEL), state_spec),
        out_shape=(jax.ShapeDtypeStruct((n, D_MODEL), F32), state_shape),
        scratch_shapes=[pltpu.VMEM((SUBLANES, D_CONV), F32)],
        compiler_params=_params("arbitrary"),
        name="mixer",
    )(x, attn, mem_k, mem_v, prev_a, prev_b, ln_g, ln_b, conv_w8, ln1_g, ln1_b, *weights)


def _route(x, wr_hi, wr_lo, bias):
    tm = x.shape[0]
    x_hi = x.astype(BF16)
    x_lo = (x - x_hi.astype(F32)).astype(BF16)
    logits = _dot_nt(wr_hi, x_hi) + (_dot_nt(wr_hi, x_lo) + _dot_nt(wr_lo, x_hi))
    s = _sigmoid(logits)
    sc = s + bias[:, 0:1]
    sub = lax.broadcasted_iota(I32, (GROUP_SIZE, tm), 0)
    neg = -jnp.inf
    groups, gscore = [], []
    for g in range(N_GROUPS):
        xg = sc[g * GROUP_SIZE:(g + 1) * GROUP_SIZE, :]
        m1 = jnp.max(xg, axis=0, keepdims=True)
        i1 = jnp.min(jnp.where(xg == m1, sub, GROUP_SIZE), axis=0, keepdims=True)
        m2 = jnp.max(jnp.where(sub == i1, neg, xg), axis=0, keepdims=True)
        groups.append(xg)
        gscore.append(m1 + m2)
    masked = []
    for g in range(N_GROUPS):
        rank = jnp.zeros((1, tm), I32)
        for o in range(N_GROUPS):
            if o == g:
                continue
            beats = (gscore[o] > gscore[g]) | ((gscore[o] == gscore[g]) & (o < g))
            rank = rank + beats.astype(I32)
        masked.append(jnp.where(rank < TOPK_GROUPS, groups[g], neg))
    v = jnp.concatenate(masked, axis=0)
    eidx = lax.broadcasted_iota(I32, (N_EXPERTS, tm), 0)
    chosen = jnp.zeros((N_EXPERTS, tm), jnp.bool_)
    for _ in range(TOP_K):
        m = jnp.max(v, axis=0, keepdims=True)
        first = jnp.min(jnp.where(v == m, eidx, N_EXPERTS), axis=0, keepdims=True)
        hit = eidx == first
        chosen = chosen | hit
        v = jnp.where(hit, neg, v)
    wsel = jnp.where(chosen, s, 0.0)
    return wsel / jnp.sum(wsel, axis=0, keepdims=True) * ROUTED_SCALE


def _expert_ffn(xb, w_gu, w_down, gate_col=None):
    gu = _dot(xb, w_gu)
    g, u = gu[:, :D_EXPERT], gu[:, D_EXPERT:]
    h = g * _sigmoid(g) * u
    if gate_col is not None:
        h = h * gate_col
    return _dot(h.astype(BF16), w_down)


def _moe_kernel(x_ref, wrh_ref, wrl_ref, br_ref, wgu_ref, wd_ref, wsgu_ref, wsd_ref, g_ref, b_ref,
                o_ref, xb_ref, gate_ref, acc_ref, *, experts_per_step):
    j = pl.program_id(1)

    @pl.when(j == 0)
    def _():
        x = x_ref[...]
        gate_t = _route(x, wrh_ref[...], wrl_ref[...], br_ref[...])
        pad = jnp.zeros((LANES - N_EXPERTS, x.shape[0]), F32)
        gate_ref[...] = jnp.concatenate([gate_t, pad], axis=0).T
        xb = x.astype(BF16)
        xb_ref[...] = xb
        acc_ref[...] = _expert_ffn(xb, wsgu_ref[...], wsd_ref[...])

    xb = xb_ref[...]
    gate = gate_ref[...]
    lane = lax.broadcasted_iota(I32, gate.shape, 1)
    acc = acc_ref[...]
    for e in range(experts_per_step):
        col = jnp.sum(jnp.where(lane == j * experts_per_step + e, gate, 0.0), axis=1, keepdims=True)
        acc = acc + _expert_ffn(xb, wgu_ref[e], wd_ref[e], col)
    acc_ref[...] = acc

    @pl.when(j == pl.num_programs(1) - 1)
    def _():
        o_ref[...] = _layer_norm(ALPHA * x_ref[...] + acc_ref[...], g_ref[...], b_ref[...])


def _moe(x, w, ln_g, ln_b, rows):
    n = x.shape[0]
    tm = min(rows, n)
    eps = MOE_EXPERTS_PER_STEP
    row = pl.BlockSpec((tm, D_MODEL), lambda i, j: (i, 0))
    kern = functools.partial(_moe_kernel, experts_per_step=eps)
    return pl.pallas_call(
        kern,
        grid=(n // tm, N_EXPERTS // eps),
        in_specs=[row, _resident(w["r_hi"].shape), _resident(w["r_lo"].shape), _resident(w["r_b"].shape),
                  pl.BlockSpec((eps, D_MODEL, 2 * D_EXPERT), lambda i, j: (j, 0, 0)),
                  pl.BlockSpec((eps, D_EXPERT, D_MODEL), lambda i, j: (j, 0, 0)),
                  _resident(w["s_gu"].shape), _resident(w["s_d"].shape),
                  _resident((1, D_MODEL)), _resident((1, D_MODEL))],
        out_specs=row,
        out_shape=jax.ShapeDtypeStruct((n, D_MODEL), F32),
        scratch_shapes=[pltpu.VMEM((tm, D_MODEL), BF16), pltpu.VMEM((tm, LANES), F32),
                        pltpu.VMEM((tm, D_MODEL), F32)],
        compiler_params=_params("arbitrary", "arbitrary"),
        name="moe",
    )(x, w["r_hi"], w["r_lo"], w["r_b"], w["e_gu"], w["e_d"], w["s_gu"], w["s_d"], ln_g, ln_b)


def _pad_rows(a, rows):
    return a if a.shape[1] == rows else jnp.pad(a, ((0, 0), (0, rows - a.shape[1]), (0, 0)))


def _head_major_per_batch(x, b, seq, tq_pad):
    heads = x.shape[1]
    if seq % Q_TILE == 0:
        return x.reshape(b, seq // Q_TILE, heads * Q_TILE, LANES)
    per_tile = Q_TILE // seq
    x = x.reshape(-1, heads, per_tile, seq, LANES).transpose(0, 2, 1, 3, 4).reshape(b, heads, seq, LANES)
    x = jnp.pad(x, ((0, 0), (0, 0), (0, tq_pad - seq), (0, 0)))
    return x.reshape(b, 1, heads * tq_pad, LANES)


def _layer(x, mem_k_bf, mem_v_bf, past_k, past_v, past_ki, conv_prev, w, ln, layer):
    b, seq, _ = x.shape
    past = past_k.shape[1]
    n = b * seq
    n_keys = past + seq
    x2 = x.reshape(n, D_MODEL)

    reps = max(1, min(PROJ_ROWS, n) // seq)
    pos_t = np.tile(past + np.arange(seq), reps)
    tabs = (_rope_tables(pos_t, HEAD_DIM // ROT_FRACTION // 2)
            + _rope_tables(pos_t, IDX_DIM // ROT_FRACTION // 2,
                           (IDX_DIM, IDX_DIM + IDX_HEADS), IDX_HEADS ** -0.5))
    q, qi, k, v, kiwi, k_bf, ki_bf = _qkv_project(x2, ln["in_g"], ln["in_b"], w, tabs, PROJ_ROWS, seq)

    l_pad = -(-n_keys // KEY_TILE) * KEY_TILE
    tq_pad = -(-seq // Q_TILE) * Q_TILE
    rows = lambda a, length: a.reshape(b, length * N_KV_HEADS, HEAD_DIM)
    with_cache = lambda new, cache: new if past == 0 else jnp.concatenate([rows(cache, past), new], axis=1)
    v3 = _pad_rows(with_cache(rows(v, seq), past_v), l_pad * N_KV_HEADS)
    k_interleaved = past > 0
    if k_interleaved:
        k3 = _pad_rows(with_cache(rows(k, seq), past_k), l_pad * N_KV_HEADS)
    else:
        k3 = _pad_rows(k_bf.reshape(b, seq, KV_DIM), l_pad)
    ki_past = jnp.pad(past_ki, ((0, 0), (0, 0), (0, LANES - IDX_DIM))).astype(BF16)
    ki3 = jnp.concatenate([ki_past, ki_bf.reshape(b, seq, LANES)], axis=1)
    attn = _dsa_attention(
        _head_major_per_batch(q, b, seq, tq_pad), _head_major_per_batch(qi, b, seq, tq_pad),
        _pad_rows(kiwi.reshape(b, seq, LANES), tq_pad), k3, v3, _pad_rows(ki3, l_pad),
        past=past, seq=seq, n_keys=n_keys, k_interleaved=k_interleaved)
    attn = attn[:, :seq].reshape(n, ATTN_DIM)

    tm = min(PROJ_ROWS, n)
    if seq < tm:
        tpos = jnp.arange(seq)[None, :, None]
        prev_a = jnp.where(tpos == 0, conv_prev[:, 1:2], 0.0).reshape(n, D_CONV)
        prev_b = jnp.where(tpos == 0, conv_prev[:, 0:1], jnp.where(tpos == 1, conv_prev[:, 1:2], 0.0)).reshape(n, D_CONV)
    else:
        prev_a = jnp.pad(conv_prev, ((0, 0), (SUBLANES - (CONV_WIDTH - 1), 0), (0, 0)))
        prev_b = prev_a
    x1, state = _mixer(x2, attn, mem_k_bf, mem_v_bf, prev_a, prev_b, ln["in_g"], ln["in_b"], w["conv"],
                       ln["g1"], ln["b1"], w, seq=seq, rows=PROJ_ROWS)
    if seq < tm:
        new_conv = state.reshape(b, seq, D_CONV)[:, seq - (CONV_WIDTH - 1):]
    else:
        new_conv = state[:, SUBLANES - (CONV_WIDTH - 1):]

    y = _moe(x1, w, ln["g2"], ln["b2"], MOE_ROWS)
    return (y.reshape(b, seq, D_MODEL), k.reshape(b, seq, N_KV_HEADS, HEAD_DIM),
            v.reshape(b, seq, N_KV_HEADS, HEAD_DIM), kiwi[:, :IDX_DIM].reshape(b, seq, IDX_DIM), new_conv)


def _prep_weights(l, w_in, conv_w, w_attn_out, w_conv_out, w_mem_out, w_o, w_router, b_router,
                  w_exp_gu, w_exp_down, w_sh_gu, w_sh_down):
    sizes = [ATTN_DIM, KV_DIM, KV_DIM, IDX_HEADS * IDX_DIM, IDX_DIM, IDX_HEADS,
             D_CONV, D_CONV, D_CONV, MEM_DIM, N_BRANCH * D_MODEL]
    offs = np.concatenate([[0], np.cumsum(sizes)]).tolist()
    wi = w_in[l]
    part = lambda j: wi[:, offs[j]:offs[j + 1]]
    w_qi = part(3).reshape(D_MODEL, IDX_HEADS, IDX_DIM)
    w_qi = jnp.pad(w_qi, ((0, 0), (0, 0), (0, LANES - IDX_DIM))).reshape(D_MODEL, IDX_HEADS * LANES)
    w_kw = jnp.pad(jnp.concatenate([part(4), part(5)], axis=1), ((0, 0), (0, LANES - IDX_DIM - IDX_HEADS)))
    wr = w_router[l].T
    wr_hi = wr.astype(BF16)
    return {
        "q": part(0).astype(BF16), "kv": jnp.concatenate([part(1), part(2)], axis=1).astype(BF16),
        "qi": w_qi.astype(BF16), "kw": w_kw.astype(BF16),
        "gb": part(6).astype(BF16), "gc": part(7).astype(BF16), "ci": part(8).astype(BF16),
        "mq": part(9).astype(BF16), "gl": part(10).astype(BF16),
        "ao": w_attn_out[l].astype(BF16), "co": w_conv_out[l].astype(BF16), "mo": w_mem_out[l].astype(BF16),
        "o": w_o[l].astype(BF16),
        "conv": jnp.pad(conv_w[l], ((0, SUBLANES - CONV_WIDTH), (0, 0))),
        "r_hi": wr_hi, "r_lo": (wr - wr_hi.astype(F32)).astype(BF16),
        "r_b": jnp.broadcast_to(b_router[l][:, None], (N_EXPERTS, LANES)),
        "e_gu": w_exp_gu[l].astype(BF16), "e_d": w_exp_down[l].astype(BF16),
        "s_gu": w_sh_gu[l].astype(BF16), "s_d": w_sh_down[l].astype(BF16),
    }


def kernel(x_prompt, x_sample, mem_prompt, cache_attn_k, cache_attn_v, cache_idx_k, cache_mem_k, cache_mem_v,
           state_conv, ln_in_g, ln_in_b, w_in, conv_w, w_mem_kv, w_attn_out, w_conv_out, w_mem_out, w_o,
           ln1_g, ln1_b, w_router, b_router, w_exp_gu, w_exp_down, w_sh_gu, w_sh_down, ln2_g, ln2_b):
    assert DEPTH == 1 and w_in.shape[0] == 1
    l = 0
    bp, n_mem = mem_prompt.shape[0], mem_prompt.shape[1]
    w = _prep_weights(l, w_in, conv_w, w_attn_out, w_conv_out, w_mem_out, w_o, w_router, b_router,
                      w_exp_gu, w_exp_down, w_sh_gu, w_sh_down)
    ln = {"in_g": ln_in_g[None, :], "in_b": ln_in_b[None, :], "g1": ln1_g[l][None, :], "b1": ln1_b[l][None, :],
          "g2": ln2_g[l][None, :], "b2": ln2_b[l][None, :]}

    mem_kv = _project(mem_prompt.reshape(bp * n_mem, D_MODEL), w_mem_kv[l].astype(BF16))
    mem_k = mem_kv[:, :MEM_DIM].reshape(bp, n_mem, MEM_DIM)
    mem_v = mem_kv[:, MEM_DIM:].reshape(bp, n_mem, MEM_DIM)

    zeros = lambda *s: jnp.zeros(s, F32)
    yp, kp, vp, kip, cp = _layer(
        x_prompt, mem_k.astype(BF16), mem_v.astype(BF16),
        zeros(bp, 0, N_KV_HEADS, HEAD_DIM), zeros(bp, 0, N_KV_HEADS, HEAD_DIM), zeros(bp, 0, IDX_DIM),
        zeros(bp, CONV_WIDTH - 1, D_CONV), w, ln, l)
    bs = x_sample.shape[0]
    ys, ks, vs, kis, cs = _layer(
        x_sample, cache_mem_k[l].reshape(bs, n_mem, MEM_DIM).astype(BF16),
        cache_mem_v[l].reshape(bs, n_mem, MEM_DIM).astype(BF16),
        cache_attn_k[l], cache_attn_v[l], cache_idx_k[l], state_conv[l], w, ln, l)

    mem_shape = (1, bp, n_mem, MEM_HEADS, MEM_HEAD_DIM)
    return (yp, ys, kp[None], vp[None], kip[None], mem_k.reshape(mem_shape), mem_v.reshape(mem_shape), cp[None],
            ks[None], vs[None], kis[None], cs[None])
```

```python
import functools

import jax
import jax.numpy as jnp
import numpy as np
from jax import lax
from jax.experimental import pallas as pl
from jax.experimental.pallas import tpu as pltpu

D_MODEL = 1024
DEPTH = 1
CHUNK = 64
HEAD_DIM = 128
N_HEADS = D_MODEL // HEAD_DIM
N_KV_HEADS = max(1, N_HEADS // 4)
KV_GROUP = N_HEADS // N_KV_HEADS
ATTN_DIM = N_HEADS * HEAD_DIM
KV_DIM = N_KV_HEADS * HEAD_DIM
ROT_FRACTION = 4
ROPE_THETA = 500000.0
IDX_HEADS = 8
IDX_DIM = 64
TOPK_MAX = 256
D_CONV = D_MODEL
CONV_WIDTH = 3
N_MEM = 256
MEM_HEADS = 4
MEM_HEAD_DIM = D_MODEL // MEM_HEADS
MEM_DIM = MEM_HEADS * MEM_HEAD_DIM
N_BRANCH = 3
N_EXPERTS = 64
TOP_K = 8
N_GROUPS = 8
GROUP_SIZE = N_EXPERTS // N_GROUPS
TOPK_GROUPS = 4
D_EXPERT = D_MODEL // 4
ROUTED_SCALE = 2.5
LN_EPS = 1e-5
ALPHA = (2.0 * DEPTH) ** 0.25

LANES = 128
SUBLANES = 8
VMEM_LIMIT_BYTES = 56 * 1024 * 1024

Q_TILE = 128
KEY_TILE = 1024
HEADS_PER_UNIT = 2
COUNT_ROWS = 64
DENOM_ROWS = 16
PROJ_ROWS = 512
MOE_ROWS = 1024
MOE_EXPERTS_PER_STEP = 4

F32 = jnp.float32
BF16 = jnp.bfloat16
I32 = jnp.int32
INT_MIN = -(2 ** 31)
INT_MAX = 2 ** 31 - 1
MASKED_LOGIT = -1e30
QK_SCALE_LOG2E = HEAD_DIM ** -0.5 * float(np.log2(np.e))

_NT = (((1,), (1,)), ((), ()))


def _dot(a, b):
    return jnp.dot(a, b, preferred_element_type=F32)


def _dot_nt(a, b):
    return lax.dot_general(a, b, _NT, preferred_element_type=F32)


def _layer_norm(x, g, b):
    mu = jnp.mean(x, axis=-1, keepdims=True)
    xc = x - mu
    var = jnp.mean(xc * xc, axis=-1, keepdims=True)
    return xc * lax.rsqrt(var + LN_EPS) * g + b


def _sigmoid(x):
    return 1.0 / (1.0 + jnp.exp(-x))


def _resident(shape):
    nd = len(shape)
    return pl.BlockSpec(shape, lambda *_: (0,) * nd, pipeline_mode=pl.Buffered(1))


def _params(*sem):
    return pltpu.CompilerParams(dimension_semantics=sem, vmem_limit_bytes=VMEM_LIMIT_BYTES)


def _rope_tables(pos, half, extra_scale_lanes=None, extra_scale=1.0):
    inv_freq = ROPE_THETA ** (-np.arange(half, dtype=np.float64) / half)
    ang = pos.astype(np.float64)[:, None] * inv_freq[None, :]
    cos, sin = np.cos(ang), np.sin(ang)
    n = pos.shape[0]
    c = np.concatenate([cos, cos, np.ones((n, LANES - 2 * half))], axis=1)
    s = np.concatenate([-sin, sin, np.zeros((n, LANES - 2 * half))], axis=1)
    if extra_scale_lanes is not None:
        lo, hi = extra_scale_lanes
        c[:, lo:hi] = extra_scale
    return jnp.asarray(c, F32), jnp.asarray(s, F32)


def _rope(x, c, s, half):
    rows, width = x.shape
    lane = lax.broadcasted_iota(I32, (rows, LANES), 1)
    first = lane < half
    outs = []
    for j in range(width // LANES):
        xj = x[:, j * LANES:(j + 1) * LANES]
        swap = jnp.where(first, pltpu.roll(xj, LANES - half, 1), pltpu.roll(xj, half, 1))
        outs.append(xj * c + swap * s)
    return outs[0] if len(outs) == 1 else jnp.concatenate(outs, axis=1)


def _proj_kernel(x_ref, w_ref, o_ref):
    o_ref[...] = _dot(x_ref[...].astype(BF16), w_ref[...])


def _project(x, w_bf, col_tile=512):
    n, d = x.shape
    m = w_bf.shape[1]
    return pl.pallas_call(
        _proj_kernel,
        grid=(m // col_tile,),
        in_specs=[pl.BlockSpec((n, d), lambda j: (0, 0)),
                  pl.BlockSpec((d, col_tile), lambda j: (0, j))],
        out_specs=pl.BlockSpec((n, col_tile), lambda j: (0, j)),
        out_shape=jax.ShapeDtypeStruct((n, m), F32),
        compiler_params=_params("arbitrary"),
        name="mem_kv_proj",
    )(x, w_bf)


def _store_head_major(ref, x):
    for r in range(ref.shape[0]):
        for h in range(ref.shape[1]):
            ref[r, h] = x[r * Q_TILE:(r + 1) * Q_TILE, h * LANES:(h + 1) * LANES]


def _qkv_kernel(x_ref, g_ref, b_ref, wq_ref, wqi_ref, wkv_ref, wkw_ref, ch_ref, sh_ref, ci_ref, si_ref,
                q_ref, qi_ref, k_ref, v_ref, kiwi_ref, kbf_ref, kibf_ref):
    xb = _layer_norm(x_ref[...], g_ref[...], b_ref[...]).astype(BF16)
    ch, sh, ci, si = ch_ref[...], sh_ref[...], ci_ref[...], si_ref[...]
    q = _rope(_dot(xb, wq_ref[...]), ch, sh, HEAD_DIM // ROT_FRACTION // 2)
    _store_head_major(q_ref, (q * QK_SCALE_LOG2E).astype(BF16))
    qi = _rope(_dot(xb, wqi_ref[...]), ci, si, IDX_DIM // ROT_FRACTION // 2)
    _store_head_major(qi_ref, qi.astype(BF16))
    kv = _dot(xb, wkv_ref[...])
    k = _rope(kv[:, :KV_DIM], ch, sh, HEAD_DIM // ROT_FRACTION // 2)
    kbf_ref[...] = k.astype(BF16)
    rows = k.shape[0]
    for n in range(N_KV_HEADS):
        k_ref[pl.ds(n, rows, stride=N_KV_HEADS), :] = k[:, n * HEAD_DIM:(n + 1) * HEAD_DIM]
        v_ref[pl.ds(n, rows, stride=N_KV_HEADS), :] = kv[:, KV_DIM + n * HEAD_DIM:KV_DIM + (n + 1) * HEAD_DIM]
    kw = _rope(_dot(xb, wkw_ref[...]), ci, si, IDX_DIM // ROT_FRACTION // 2)
    kiwi_ref[...] = kw
    kibf_ref[...] = kw.astype(BF16)


def _qkv_project(x, ln_g, ln_b, w, tabs, rows, seq):
    n = x.shape[0]
    tm = min(rows, n)
    tab_blocks = max(seq, tm) // tm
    row = lambda width: pl.BlockSpec((tm, width), lambda i: (i, 0))
    tab = pl.BlockSpec((tm, LANES), lambda i: (i % tab_blocks, 0))
    cache_rows = pl.BlockSpec((tm * N_KV_HEADS, HEAD_DIM), lambda i: (i, 0))
    hm_shape = lambda heads: (n // Q_TILE, heads, Q_TILE, LANES)
    hm = lambda heads: pl.BlockSpec((tm // Q_TILE, heads, Q_TILE, LANES), lambda i: (i, 0, 0, 0))
    out_shapes = (
        jax.ShapeDtypeStruct(hm_shape(N_HEADS), BF16),
        jax.ShapeDtypeStruct(hm_shape(IDX_HEADS), BF16),
        jax.ShapeDtypeStruct((n * N_KV_HEADS, HEAD_DIM), F32),
        jax.ShapeDtypeStruct((n * N_KV_HEADS, HEAD_DIM), F32),
        jax.ShapeDtypeStruct((n, LANES), F32),
        jax.ShapeDtypeStruct((n, KV_DIM), BF16),
        jax.ShapeDtypeStruct((n, LANES), BF16),
    )
    return pl.pallas_call(
        _qkv_kernel,
        grid=(n // tm,),
        in_specs=[row(D_MODEL), _resident((1, D_MODEL)), _resident((1, D_MODEL)),
                  _resident(w["q"].shape), _resident(w["qi"].shape), _resident(w["kv"].shape),
                  _resident(w["kw"].shape), tab, tab, tab, tab],
        out_specs=(hm(N_HEADS), hm(IDX_HEADS), cache_rows, cache_rows, row(LANES),
                   row(KV_DIM), row(LANES)),
        out_shape=out_shapes,
        compiler_params=_params("arbitrary"),
        name="qkv_proj",
    )(x, ln_g, ln_b, w["q"], w["qi"], w["kv"], w["kw"], *tabs)


def _sortable_key(score):
    bits = lax.bitcast_convert_type(score + 0.0, I32)
    return bits ^ ((bits >> 31) & INT_MAX)


def _dsa_kernel(q_ref, qi_ref, kiwi_ref, k_ref, v_ref, ki_ref, o_ref,
                vt_ref, key_ref, acc_ref, m_ref, j_ref,
                *, past, seq, n_keys, topk, n_key_tiles, k_interleaved):
    i = pl.program_id(1)
    tq, kt = Q_TILE, KEY_TILE
    uq = HEADS_PER_UNIT * tq

    def kv_tile(ref, j, n, interleaved):
        if interleaved:
            return ref[0, pl.ds(j * (kt * N_KV_HEADS) + n, kt, stride=N_KV_HEADS), :]
        return ref[0, pl.ds(pl.multiple_of(j * kt, kt), kt), n * HEAD_DIM:(n + 1) * HEAD_DIM]

    @pl.when(i == 0)
    def _():
        ones = jnp.ones((DENOM_ROWS, kt), BF16)
        for c in range(n_key_tiles):
            for n in range(N_KV_HEADS):
                vt_ref[c, n, :HEAD_DIM, :] = kv_tile(v_ref, c, n, True).T.astype(BF16)
                vt_ref[c, n, HEAD_DIM:, :] = ones

    lane = lax.broadcasted_iota(I32, (1, tq), 1)
    t = i * tq + lane
    q_real = t < seq
    lim = jnp.minimum(((past + t) // CHUNK + 1) * CHUNK, n_keys)
    lim_max = jnp.minimum(((past + i * tq + tq - 1) // CHUNK + 1) * CHUNK, n_keys)
    n_tiles = (lim_max + kt - 1) // kt
    row = lax.broadcasted_iota(I32, (kt, tq), 0)

    wi = kiwi_ref[0].T[IDX_DIM:IDX_DIM + IDX_HEADS, :] * (IDX_DIM ** -0.5)

    def score_tile(j, n_nonneg):
        kib = ki_ref[0, pl.ds(pl.multiple_of(j * kt, kt), kt), :]
        rel_of = lambda u: _dot_nt(kib, qi_ref[0, 0, u * uq:(u + 1) * uq, :])
        n_units = IDX_HEADS // HEADS_PER_UNIT
        acc = jnp.zeros((kt, tq), F32)
        rel_next = rel_of(0)
        for u in range(n_units):
            rel, rel_next = rel_next, (rel_of(u + 1) if u + 1 < n_units else None)
            for g in range(HEADS_PER_UNIT):
                h = u * HEADS_PER_UNIT + g
                acc = acc + jnp.maximum(rel[:, g * tq:(g + 1) * tq], 0.0) * wi[h:h + 1, :]
        score = jnp.where(row < lim - j * kt, acc, -jnp.inf)
        key = _sortable_key(score)
        key_ref[j] = key
        for r in range(kt // COUNT_ROWS):
            n_nonneg = jnp.where(key[r * COUNT_ROWS:(r + 1) * COUNT_ROWS] >= 0, n_nonneg + 1, n_nonneg)
        return n_nonneg

    n_nonneg = lax.fori_loop(0, n_tiles, score_tile, jnp.zeros((COUNT_ROWS, tq), I32))

    def count(pred):
        def body(j, acc):
            blk = key_ref[j]
            for r in range(kt // COUNT_ROWS):
                rows = slice(r * COUNT_ROWS, (r + 1) * COUNT_ROWS)
                acc = jnp.where(pred(blk[rows], j * kt + row[rows]), acc + 1, acc)
            return acc
        acc = lax.fori_loop(0, n_tiles, body, jnp.zeros((COUNT_ROWS, tq), I32))
        return jnp.sum(acc, axis=0, keepdims=True)

    def count_ge(cand):
        return count(lambda keys, kidx: keys >= cand)

    def bisect(b, state):
        thr, cnt = state
        cand = thr | jnp.left_shift(jnp.int32(1), 30 - b)
        c = count_ge(cand)
        take = c >= topk
        return jnp.where(take, cand, thr), jnp.where(take, c, cnt)

    c0 = jnp.sum(n_nonneg, axis=0, keepdims=True)
    nonneg = c0 >= topk
    thr, cnt = lax.fori_loop(0, 31, bisect, (jnp.where(nonneg, 0, INT_MIN).astype(I32),
                                             jnp.where(nonneg, c0, n_tiles * kt).astype(I32)))
    tied = jnp.max(jnp.where(q_real, cnt, topk)) > topk

    j_ref[...] = jnp.full((1, tq), INT_MAX, I32)

    @pl.when(tied)
    def _():
        n_ties_wanted = topk - count(lambda keys, kidx: keys > thr)
        n_bits = int(n_key_tiles * kt).bit_length()

        def bisect_idx(it, cut):
            cand = cut | jnp.left_shift(jnp.int32(1), n_bits - 1 - it)
            c = count(lambda keys, kidx: (keys == thr) & (kidx < cand))
            return jnp.where(c <= n_ties_wanted, cand, cut)

        j_ref[...] = lax.fori_loop(0, n_bits, bisect_idx, jnp.zeros((1, tq), I32))

    j_cut = j_ref[...]

    m_ref[...] = jnp.full(m_ref.shape, MASKED_LOGIT, F32)
    acc_ref[...] = jnp.zeros(acc_ref.shape, F32)

    def attend_tile(j, carry):
        n_units = N_HEADS // HEADS_PER_UNIT
        kv_of = lambda u: u * HEADS_PER_UNIT // KV_GROUP
        keys = key_ref[j]
        sel = ((keys > thr) | ((keys == thr) & (row < j_cut - j * kt))) & (row < lim - j * kt)

        def logits(u):
            kb = kv_tile(k_ref, j, kv_of(u), k_interleaved).astype(BF16)
            return _dot_nt(kb, q_ref[0, 0, u * uq:(u + 1) * uq, :])

        s_next = logits(0)
        for u in range(n_units):
            s, s_next = s_next, (logits(u + 1) if u + 1 < n_units else None)
            s = jnp.concatenate([jnp.where(sel, s[:, g * tq:(g + 1) * tq], MASKED_LOGIT)
                                 for g in range(HEADS_PER_UNIT)], axis=1)
            m_old = m_ref[u]
            m_new = jnp.maximum(m_old, jnp.max(s, axis=0, keepdims=True))
            p = jnp.exp2(s - m_new).astype(BF16)
            acc_ref[u] = jnp.exp2(m_old - m_new) * acc_ref[u] + _dot(vt_ref[j, kv_of(u)], p)
            m_ref[u] = m_new
        return carry

    lax.fori_loop(0, n_tiles, attend_tile, 0)

    for h in range(N_HEADS):
        u, cols = h // HEADS_PER_UNIT, slice((h % HEADS_PER_UNIT) * tq, (h % HEADS_PER_UNIT + 1) * tq)
        o = acc_ref[u, :HEAD_DIM, cols] / acc_ref[u, HEAD_DIM:HEAD_DIM + 1, cols]
        o_ref[0, :, h * HEAD_DIM:(h + 1) * HEAD_DIM] = o.T.astype(BF16)


def _dsa_attention(q, qi, kiwi, k, v, ki_bf, *, past, seq, n_keys, k_interleaved):
    b, tq_pad, _ = kiwi.shape
    l_pad = ki_bf.shape[1]
    n_key_tiles = l_pad // KEY_TILE
    topk = min(TOPK_MAX, n_keys // 4)
    qblk = lambda width: pl.BlockSpec((1, Q_TILE, width), lambda bi, i: (bi, i, 0))
    hmblk = lambda heads: pl.BlockSpec((1, 1, heads * Q_TILE, LANES), lambda bi, i: (bi, i, 0, 0))
    kblk = lambda rows, width: pl.BlockSpec((1, rows, width), lambda bi, i: (bi, 0, 0),
                                            pipeline_mode=pl.Buffered(1))
    cache_blk = kblk(l_pad * N_KV_HEADS, HEAD_DIM)
    kern = functools.partial(_dsa_kernel, past=past, seq=seq, n_keys=n_keys, topk=topk,
                             n_key_tiles=n_key_tiles, k_interleaved=k_interleaved)
    return pl.pallas_call(
        kern,
        grid=(b, tq_pad // Q_TILE),
        in_specs=[hmblk(N_HEADS), hmblk(IDX_HEADS), qblk(LANES),
                  cache_blk if k_interleaved else kblk(l_pad, KV_DIM), cache_blk, kblk(l_pad, LANES)],
        out_specs=qblk(ATTN_DIM),
        out_shape=jax.ShapeDtypeStruct((b, tq_pad, ATTN_DIM), BF16),
        scratch_shapes=[
            pltpu.VMEM((n_key_tiles, N_KV_HEADS, HEAD_DIM + DENOM_ROWS, KEY_TILE), BF16),
            pltpu.VMEM((n_key_tiles, KEY_TILE, Q_TILE), I32),
            pltpu.VMEM((N_HEADS // HEADS_PER_UNIT, HEAD_DIM + DENOM_ROWS, HEADS_PER_UNIT * Q_TILE), F32),
            pltpu.VMEM((N_HEADS // HEADS_PER_UNIT, 1, HEADS_PER_UNIT * Q_TILE), F32),
            pltpu.VMEM((1, Q_TILE), I32),
        ],
        compiler_params=_params("arbitrary", "arbitrary"),
        name="dsa_attention",
    )(q, qi, kiwi, k, v, ki_bf)


def _mixer_kernel(x_ref, attn_ref, memk_ref, memv_ref, prev_a_ref, prev_b_ref,
                  lng_ref, lnb_ref, cw_ref, ln1g_ref, ln1b_ref,
                  wgb_ref, wgc_ref, wci_ref, wmq_ref, wgl_ref, wao_ref, wco_ref, wmo_ref, wo_ref,
                  x1_ref, state_ref, carry_ref, *, seq, rows):
    i = pl.program_id(0)
    multi_seq = seq < rows
    xn = _layer_norm(x_ref[...], lng_ref[...], lnb_ref[...])
    xb = xn.astype(BF16)

    u = _dot(xb, wgc_ref[...]) * _dot(xb, wci_ref[...])
    ridx = lax.broadcasted_iota(I32, u.shape, 0)
    if multi_seq:
        tpos = ridx & (seq - 1)
        s1 = jnp.where(tpos == 0, prev_a_ref[...], pltpu.roll(u, 1, 0))
        s2 = jnp.where(tpos < 2, prev_b_ref[...], pltpu.roll(u, 2, 0))
        state_ref[...] = u
    else:
        blocks_per_seq = seq // rows
        @pl.when(i % blocks_per_seq == 0)
        def _():
            carry_ref[...] = prev_a_ref[0]

        prev = carry_ref[...]
        p1, p2 = prev[SUBLANES - 1:SUBLANES, :], prev[SUBLANES - 2:SUBLANES - 1, :]
        s1 = jnp.where(ridx == 0, p1, pltpu.roll(u, 1, 0))
        s2 = jnp.where(ridx == 0, p2, jnp.where(ridx == 1, p1, pltpu.roll(u, 2, 0)))
        carry_ref[...] = u[rows - SUBLANES:, :]
        state_ref[0] = u[rows - SUBLANES:, :]
    cw = cw_ref[...]
    conv = cw[0:1, :] * s2 + cw[1:2, :] * s1 + cw[2:3, :] * u
    conv_in = (_dot(xb, wgb_ref[...]) * conv).astype(BF16)

    mq = _dot(xb, wmq_ref[...]).astype(BF16)
    n_batches = max(1, rows // seq)
    rb = rows // n_batches
    mem_rows = []
    for bi in range(n_batches):
        heads = []
        for h in range(MEM_HEADS):
            cols = slice(h * MEM_HEAD_DIM, (h + 1) * MEM_HEAD_DIM)
            s = _dot_nt(mq[bi * rb:(bi + 1) * rb, cols], memk_ref[bi, :, cols]) * (MEM_HEAD_DIM ** -0.5)
            e = jnp.exp(s - jnp.max(s, axis=-1, keepdims=True))
            p = e / jnp.sum(e, axis=-1, keepdims=True)
            heads.append(_dot(p.astype(BF16), memv_ref[bi, :, cols]))
        mem_rows.append(jnp.concatenate(heads, axis=1))
    mem = (mem_rows[0] if n_batches == 1 else jnp.concatenate(mem_rows, axis=0)).astype(BF16)

    gate = _sigmoid(_dot(xb, wgl_ref[...]))
    mixed = (gate[:, :D_MODEL] * _dot(attn_ref[...], wao_ref[...])
             + gate[:, D_MODEL:2 * D_MODEL] * _dot(conv_in, wco_ref[...])
             + gate[:, 2 * D_MODEL:] * _dot(mem, wmo_ref[...]))
    y = ALPHA * xn + _dot(mixed.astype(BF16), wo_ref[...])
    x1_ref[...] = _layer_norm(y, ln1g_ref[...], ln1b_ref[...])


def _mixer(x, attn, mem_k, mem_v, prev_a, prev_b, ln_g, ln_b, conv_w8, ln1_g, ln1_b, w, *, seq, rows):
    n = x.shape[0]
    tm = min(rows, n)
    multi_seq = seq < tm
    n_batches = max(1, tm // seq)
    row = lambda width: pl.BlockSpec((tm, width), lambda i: (i, 0))
    mem = pl.BlockSpec((n_batches, N_MEM, MEM_DIM), lambda i: ((i * tm) // (seq * n_batches), 0, 0))
    vec = _resident((1, D_MODEL))
    if multi_seq:
        prev_spec = row(D_CONV)
        state_spec = row(D_CONV)
        state_shape = jax.ShapeDtypeStruct((n, D_CONV), F32)
    else:
        prev_spec = pl.BlockSpec((1, SUBLANES, D_CONV), lambda i: ((i * tm) // seq, 0, 0))
        state_spec = prev_spec
        state_shape = jax.ShapeDtypeStruct((n // seq, SUBLANES, D_CONV), F32)
    weights = [w[k] for k in ("gb", "gc", "ci", "mq", "gl", "ao", "co", "mo", "o")]
    kern = functools.partial(_mixer_kernel, seq=seq, rows=tm)
    return pl.pallas_call(
        kern,
        grid=(n // tm,),
        in_specs=[row(D_MODEL), row(ATTN_DIM), mem, mem, prev_spec, prev_spec,
                  vec, vec, _resident((SUBLANES, D_CONV)), vec, vec] + [_resident(m.shape) for m in weights],
        out_specs=(row(D_MODEL), state_spec),
        out_shape=(jax.ShapeDtypeStruct((n, D_MODEL), F32), state_shape),
        scratch_shapes=[pltpu.VMEM((SUBLANES, D_CONV), F32)],
        compiler_params=_params("arbitrary"),
        name="mixer",
    )(x, attn, mem_k, mem_v, prev_a, prev_b, ln_g, ln_b, conv_w8, ln1_g, ln1_b, *weights)


def _route(x, wr_hi, wr_lo, bias):
    tm = x.shape[0]
    x_hi = x.astype(BF16)
    x_lo = (x - x_hi.astype(F32)).astype(BF16)
    logits = _dot_nt(wr_hi, x_hi) + (_dot_nt(wr_hi, x_lo) + _dot_nt(wr_lo, x_hi))
    s = _sigmoid(logits)
    sc = s + bias[:, 0:1]
    sub = lax.broadcasted_iota(I32, (GROUP_SIZE, tm), 0)
    neg = -jnp.inf
    groups, gscore = [], []
    for g in range(N_GROUPS):
        xg = sc[g * GROUP_SIZE:(g + 1) * GROUP_SIZE, :]
        m1 = jnp.max(xg, axis=0, keepdims=True)
        i1 = jnp.min(jnp.where(xg == m1, sub, GROUP_SIZE), axis=0, keepdims=True)
        m2 = jnp.max(jnp.where(sub == i1, neg, xg), axis=0, keepdims=True)
        groups.append(xg)
        gscore.append(m1 + m2)
    masked = []
    for g in range(N_GROUPS):
        rank = jnp.zeros((1, tm), I32)
        for o in range(N_GROUPS):
            if o == g:
                continue
            beats = (gscore[o] > gscore[g]) | ((gscore[o] == gscore[g]) & (o < g))
            rank = rank + beats.astype(I32)
        masked.append(jnp.where(rank < TOPK_GROUPS, groups[g], neg))
    v = jnp.concatenate(masked, axis=0)
    eidx = lax.broadcasted_iota(I32, (N_EXPERTS, tm), 0)
    chosen = jnp.zeros((N_EXPERTS, tm), jnp.bool_)
    for _ in range(TOP_K):
        m = jnp.max(v, axis=0, keepdims=True)
        first = jnp.min(jnp.where(v == m, eidx, N_EXPERTS), axis=0, keepdims=True)
        hit = eidx == first
        chosen = chosen | hit
        v = jnp.where(hit, neg, v)
    wsel = jnp.where(chosen, s, 0.0)
    return wsel / jnp.sum(wsel, axis=0, keepdims=True) * ROUTED_SCALE


def _expert_ffn(xb, w_gu, w_down, gate_col=None):
    gu = _dot(xb, w_gu)
    g, u = gu[:, :D_EXPERT], gu[:, D_EXPERT:]
    h = g * _sigmoid(g) * u
    if gate_col is not None:
        h = h * gate_col
    return _dot(h.astype(BF16), w_down)


def _moe_kernel(x_ref, wrh_ref, wrl_ref, br_ref, wgu_ref, wd_ref, wsgu_ref, wsd_ref, g_ref, b_ref,
                o_ref, xb_ref, gate_ref, acc_ref, *, experts_per_step):
    j = pl.program_id(1)

    @pl.when(j == 0)
    def _():
        x = x_ref[...]
        gate_t = _route(x, wrh_ref[...], wrl_ref[...], br_ref[...])
        pad = jnp.zeros((LANES - N_EXPERTS, x.shape[0]), F32)
        gate_ref[...] = jnp.concatenate([gate_t, pad], axis=0).T
        xb = x.astype(BF16)
        xb_ref[...] = xb
        acc_ref[...] = _expert_ffn(xb, wsgu_ref[...], wsd_ref[...])

    xb = xb_ref[...]
    gate = gate_ref[...]
    lane = lax.broadcasted_iota(I32, gate.shape, 1)
    acc = acc_ref[...]
    for e in range(experts_per_step):
        col = jnp.sum(jnp.where(lane == j * experts_per_step + e, gate, 0.0), axis=1, keepdims=True)
        acc = acc + _expert_ffn(xb, wgu_ref[e], wd_ref[e], col)
    acc_ref[...] = acc

    @pl.when(j == pl.num_programs(1) - 1)
    def _():
        o_ref[...] = _layer_norm(ALPHA * x_ref[...] + acc_ref[...], g_ref[...], b_ref[...])


def _moe(x, w, ln_g, ln_b, rows):
    n = x.shape[0]
    tm = min(rows, n)
    eps = MOE_EXPERTS_PER_STEP
    row = pl.BlockSpec((tm, D_MODEL), lambda i, j: (i, 0))
    kern = functools.partial(_moe_kernel, experts_per_step=eps)
    return pl.pallas_call(
        kern,
        grid=(n // tm, N_EXPERTS // eps),
        in_specs=[row, _resident(w["r_hi"].shape), _resident(w["r_lo"].shape), _resident(w["r_b"].shape),
                  pl.BlockSpec((eps, D_MODEL, 2 * D_EXPERT), lambda i, j: (j, 0, 0)),
                  pl.BlockSpec((eps, D_EXPERT, D_MODEL), lambda i, j: (j, 0, 0)),
                  _resident(w["s_gu"].shape), _resident(w["s_d"].shape),
                  _resident((1, D_MODEL)), _resident((1, D_MODEL))],
        out_specs=row,
        out_shape=jax.ShapeDtypeStruct((n, D_MODEL), F32),
        scratch_shapes=[pltpu.VMEM((tm, D_MODEL), BF16), pltpu.VMEM((tm, LANES), F32),
                        pltpu.VMEM((tm, D_MODEL), F32)],
        compiler_params=_params("arbitrary", "arbitrary"),
        name="moe",
    )(x, w["r_hi"], w["r_lo"], w["r_b"], w["e_gu"], w["e_d"], w["s_gu"], w["s_d"], ln_g, ln_b)


def _pad_rows(a, rows):
    return a if a.shape[1] == rows else jnp.pad(a, ((0, 0), (0, rows - a.shape[1]), (0, 0)))


def _head_major_per_batch(x, b, seq, tq_pad):
    heads = x.shape[1]
    if seq % Q_TILE == 0:
        return x.reshape(b, seq // Q_TILE, heads * Q_TILE, LANES)
    per_tile = Q_TILE // seq
    x = x.reshape(-1, heads, per_tile, seq, LANES).transpose(0, 2, 1, 3, 4).reshape(b, heads, seq, LANES)
    x = jnp.pad(x, ((0, 0), (0, 0), (0, tq_pad - seq), (0, 0)))
    return x.reshape(b, 1, heads * tq_pad, LANES)


def _layer(x, mem_k_bf, mem_v_bf, past_k, past_v, past_ki, conv_prev, w, ln, layer):
    b, seq, _ = x.shape
    past = past_k.shape[1]
    n = b * seq
    n_keys = past + seq
    x2 = x.reshape(n, D_MODEL)

    reps = max(1, min(PROJ_ROWS, n) // seq)
    pos_t = np.tile(past + np.arange(seq), reps)
    tabs = (_rope_tables(pos_t, HEAD_DIM // ROT_FRACTION // 2)
            + _rope_tables(pos_t, IDX_DIM // ROT_FRACTION // 2,
                           (IDX_DIM, IDX_DIM + IDX_HEADS), IDX_HEADS ** -0.5))
    q, qi, k, v, kiwi, k_bf, ki_bf = _qkv_project(x2, ln["in_g"], ln["in_b"], w, tabs, PROJ_ROWS, seq)

    l_pad = -(-n_keys // KEY_TILE) * KEY_TILE
    tq_pad = -(-seq // Q_TILE) * Q_TILE
    rows = lambda a, length: a.reshape(b, length * N_KV_HEADS, HEAD_DIM)
    with_cache = lambda new, cache: new if past == 0 else jnp.concatenate([rows(cache, past), new], axis=1)
    v3 = _pad_rows(with_cache(rows(v, seq), past_v), l_pad * N_KV_HEADS)
    k_interleaved = past > 0
    if k_interleaved:
        k3 = _pad_rows(with_cache(rows(k, seq), past_k), l_pad * N_KV_HEADS)
    else:
        k3 = _pad_rows(k_bf.reshape(b, seq, KV_DIM), l_pad)
    ki_past = jnp.pad(past_ki, ((0, 0), (0, 0), (0, LANES - IDX_DIM))).astype(BF16)
    ki3 = jnp.concatenate([ki_past, ki_bf.reshape(b, seq, LANES)], axis=1)
    attn = _dsa_attention(
        _head_major_per_batch(q, b, seq, tq_pad), _head_major_per_batch(qi, b, seq, tq_pad),
        _pad_rows(kiwi.reshape(b, seq, LANES), tq_pad), k3, v3, _pad_rows(ki3, l_pad),
        past=past, seq=seq, n_keys=n_keys, k_interleaved=k_interleaved)
    attn = attn[:, :seq].reshape(n, ATTN_DIM)

    tm = min(PROJ_ROWS, n)
    if seq < tm:
        tpos = jnp.arange(seq)[None, :, None]
        prev_a = jnp.where(tpos == 0, conv_prev[:, 1:2], 0.0).reshape(n, D_CONV)
        prev_b = jnp.where(tpos == 0, conv_prev[:, 0:1], jnp.where(tpos == 1, conv_prev[:, 1:2], 0.0)).reshape(n, D_CONV)
    else:
        prev_a = jnp.pad(conv_prev, ((0, 0), (SUBLANES - (CONV_WIDTH - 1), 0), (0, 0)))
        prev_b = prev_a
    x1, state = _mixer(x2, attn, mem_k_bf, mem_v_bf, prev_a, prev_b, ln["in_g"], ln["in_b"], w["conv"],
                       ln["g1"], ln["b1"], w, seq=seq, rows=PROJ_ROWS)
    if seq < tm:
        new_conv = state.reshape(b, seq, D_CONV)[:, seq - (CONV_WIDTH - 1):]
    else:
        new_conv = state[:, SUBLANES - (CONV_WIDTH - 1):]

    y = _moe(x1, w, ln["g2"], ln["b2"], MOE_ROWS)
    return (y.reshape(b, seq, D_MODEL), k.reshape(b, seq, N_KV_HEADS, HEAD_DIM),
            v.reshape(b, seq, N_KV_HEADS, HEAD_DIM), kiwi[:, :IDX_DIM].reshape(b, seq, IDX_DIM), new_conv)


def _prep_weights(l, w_in, conv_w, w_attn_out, w_conv_out, w_mem_out, w_o, w_router, b_router,
                  w_exp_gu, w_exp_down, w_sh_gu, w_sh_down):
    sizes = [ATTN_DIM, KV_DIM, KV_DIM, IDX_HEADS * IDX_DIM, IDX_DIM, IDX_HEADS,
             D_CONV, D_CONV, D_CONV, MEM_DIM, N_BRANCH * D_MODEL]
    offs = np.concatenate([[0], np.cumsum(sizes)]).tolist()
    wi = w_in[l]
    part = lambda j: wi[:, offs[j]:offs[j + 1]]
    w_qi = part(3).reshape(D_MODEL, IDX_HEADS, IDX_DIM)
    w_qi = jnp.pad(w_qi, ((0, 0), (0, 0), (0, LANES - IDX_DIM))).reshape(D_MODEL, IDX_HEADS * LANES)
    w_kw = jnp.pad(jnp.concatenate([part(4), part(5)], axis=1), ((0, 0), (0, LANES - IDX_DIM - IDX_HEADS)))
    wr = w_router[l].T
    wr_hi = wr.astype(BF16)
    return {
        "q": part(0).astype(BF16), "kv": jnp.concatenate([part(1), part(2)], axis=1).astype(BF16),
        "qi": w_qi.astype(BF16), "kw": w_kw.astype(BF16),
        "gb": part(6).astype(BF16), "gc": part(7).astype(BF16), "ci": part(8).astype(BF16),
        "mq": part(9).astype(BF16), "gl": part(10).astype(BF16),
        "ao": w_attn_out[l].astype(BF16), "co": w_conv_out[l].astype(BF16), "mo": w_mem_out[l].astype(BF16),
        "o": w_o[l].astype(BF16),
        "conv": jnp.pad(conv_w[l], ((0, SUBLANES - CONV_WIDTH), (0, 0))),
        "r_hi": wr_hi, "r_lo": (wr - wr_hi.astype(F32)).astype(BF16),
        "r_b": jnp.broadcast_to(b_router[l][:, None], (N_EXPERTS, LANES)),
        "e_gu": w_exp_gu[l].astype(BF16), "e_d": w_exp_down[l].astype(BF16),
        "s_gu": w_sh_gu[l].astype(BF16), "s_d": w_sh_down[l].astype(BF16),
    }


def kernel(x_prompt, x_sample, mem_prompt, cache_attn_k, cache_attn_v, cache_idx_k, cache_mem_k, cache_mem_v,
           state_conv, ln_in_g, ln_in_b, w_in, conv_w, w_mem_kv, w_attn_out, w_conv_out, w_mem_out, w_o,
           ln1_g, ln1_b, w_router, b_router, w_exp_gu, w_exp_down, w_sh_gu, w_sh_down, ln2_g, ln2_b):
    assert DEPTH == 1 and w_in.shape[0] == 1
    l = 0
    bp, n_mem = mem_prompt.shape[0], mem_prompt.shape[1]
    w = _prep_weights(l, w_in, conv_w, w_attn_out, w_conv_out, w_mem_out, w_o, w_router, b_router,
                      w_exp_gu, w_exp_down, w_sh_gu, w_sh_down)
    ln = {"in_g": ln_in_g[None, :], "in_b": ln_in_b[None, :], "g1": ln1_g[l][None, :], "b1": ln1_b[l][None, :],
          "g2": ln2_g[l][None, :], "b2": ln2_b[l][None, :]}

    mem_kv = _project(mem_prompt.reshape(bp * n_mem, D_MODEL), w_mem_kv[l].astype(BF16))
    mem_k = mem_kv[:, :MEM_DIM].reshape(bp, n_mem, MEM_DIM)
    mem_v = mem_kv[:, MEM_DIM:].reshape(bp, n_mem, MEM_DIM)

    zeros = lambda *s: jnp.zeros(s, F32)
    yp, kp, vp, kip, cp = _layer(
        x_prompt, mem_k.astype(BF16), mem_v.astype(BF16),
        zeros(bp, 0, N_KV_HEADS, HEAD_DIM), zeros(bp, 0, N_KV_HEADS, HEAD_DIM), zeros(bp, 0, IDX_DIM),
        zeros(bp, CONV_WIDTH - 1, D_CONV), w, ln, l)
    bs = x_sample.shape[0]
    ys, ks, vs, kis, cs = _layer(
        x_sample, cache_mem_k[l].reshape(bs, n_mem, MEM_DIM).astype(BF16),
        cache_mem_v[l].reshape(bs, n_mem, MEM_DIM).astype(BF16),
        cache_attn_k[l], cache_attn_v[l], cache_idx_k[l], state_conv[l], w, ln, l)

    mem_shape = (1, bp, n_mem, MEM_HEADS, MEM_HEAD_DIM)
    return (yp, ys, kp[None], vp[None], kip[None], mem_k.reshape(mem_shape), mem_v.reshape(mem_shape), cp[None],
            ks[None], vs[None], kis[None], cs[None])
```

```python
import functools

import jax
import jax.numpy as jnp
import numpy as np
from jax import lax
from jax.experimental import pallas as pl
from jax.experimental.pallas import tpu as pltpu

D_MODEL = 1024
DEPTH = 1
CHUNK = 64
HEAD_DIM = 128
N_HEADS = D_MODEL // HEAD_DIM
N_KV_HEADS = max(1, N_HEADS // 4)
KV_GROUP = N_HEADS // N_KV_HEADS
ATTN_DIM = N_HEADS * HEAD_DIM
KV_DIM = N_KV_HEADS * HEAD_DIM
ROT_FRACTION = 4
ROPE_THETA = 500000.0
IDX_HEADS = 8
IDX_DIM = 64
TOPK_MAX = 256
D_CONV = D_MODEL
CONV_WIDTH = 3
N_MEM = 256
MEM_HEADS = 4
MEM_HEAD_DIM = D_MODEL // MEM_HEADS
MEM_DIM = MEM_HEADS * MEM_HEAD_DIM
N_BRANCH = 3
N_EXPERTS = 64
TOP_K = 8
N_GROUPS = 8
GROUP_SIZE = N_EXPERTS // N_GROUPS
TOPK_GROUPS = 4
D_EXPERT = D_MODEL // 4
ROUTED_SCALE = 2.5
LN_EPS = 1e-5
ALPHA = (2.0 * DEPTH) ** 0.25

LANES = 128
SUBLANES = 8
VMEM_LIMIT_BYTES = 56 * 1024 * 1024

Q_TILE = 128
KEY_TILE = 1024
HEADS_PER_UNIT = 2
COUNT_ROWS = 64
DENOM_ROWS = 16
PROJ_ROWS = 512
MOE_ROWS = 1024
MOE_EXPERTS_PER_STEP = 4

F32 = jnp.float32
BF16 = jnp.bfloat16
I32 = jnp.int32
INT_MIN = -(2 ** 31)
INT_MAX = 2 ** 31 - 1
MASKED_LOGIT = -1e30
QK_SCALE_LOG2E = HEAD_DIM ** -0.5 * float(np.log2(np.e))

_NT = (((1,), (1,)), ((), ()))


def _dot(a, b):
    return jnp.dot(a, b, preferred_element_type=F32)


def _dot_nt(a, b):
    return lax.dot_general(a, b, _NT, preferred_element_type=F32)


def _layer_norm(x, g, b):
    mu = jnp.mean(x, axis=-1, keepdims=True)
    xc = x - mu
    var = jnp.mean(xc * xc, axis=-1, keepdims=True)
    return xc * lax.rsqrt(var + LN_EPS) * g + b


def _sigmoid(x):
    return 1.0 / (1.0 + jnp.exp(-x))


def _resident(shape):
    nd = len(shape)
    return pl.BlockSpec(shape, lambda *_: (0,) * nd, pipeline_mode=pl.Buffered(1))


def _params(*sem):
    return pltpu.CompilerParams(dimension_semantics=sem, vmem_limit_bytes=VMEM_LIMIT_BYTES)


def _rope_tables(pos, half, extra_scale_lanes=None, extra_scale=1.0):
    inv_freq = ROPE_THETA ** (-np.arange(half, dtype=np.float64) / half)
    ang = pos.astype(np.float64)[:, None] * inv_freq[None, :]
    cos, sin = np.cos(ang), np.sin(ang)
    n = pos.shape[0]
    c = np.concatenate([cos, cos, np.ones((n, LANES - 2 * half))], axis=1)
    s = np.concatenate([-sin, sin, np.zeros((n, LANES - 2 * half))], axis=1)
    if extra_scale_lanes is not None:
        lo, hi = extra_scale_lanes
        c[:, lo:hi] = extra_scale
    return jnp.asarray(c, F32), jnp.asarray(s, F32)


def _rope(x, c, s, half):
    rows, width = x.shape
    lane = lax.broadcasted_iota(I32, (rows, LANES), 1)
    first = lane < half
    outs = []
    for j in range(width // LANES):
        xj = x[:, j * LANES:(j + 1) * LANES]
        swap = jnp.where(first, pltpu.roll(xj, LANES - half, 1), pltpu.roll(xj, half, 1))
        outs.append(xj * c + swap * s)
    return outs[0] if len(outs) == 1 else jnp.concatenate(outs, axis=1)


def _proj_kernel(x_ref, w_ref, o_ref):
    o_ref[...] = _dot(x_ref[...].astype(BF16), w_ref[...])


def _project(x, w_bf, col_tile=512):
    n, d = x.shape
    m = w_bf.shape[1]
    return pl.pallas_call(
        _proj_kernel,
        grid=(m // col_tile,),
        in_specs=[pl.BlockSpec((n, d), lambda j: (0, 0)),
                  pl.BlockSpec((d, col_tile), lambda j: (0, j))],
        out_specs=pl.BlockSpec((n, col_tile), lambda j: (0, j)),
        out_shape=jax.ShapeDtypeStruct((n, m), F32),
        compiler_params=_params("arbitrary"),
        name="mem_kv_proj",
    )(x, w_bf)


def _store_head_major(ref, x):
    for r in range(ref.shape[0]):
        for h in range(ref.shape[1]):
            ref[r, h] = x[r * Q_TILE:(r + 1) * Q_TILE, h * LANES:(h + 1) * LANES]


def _qkv_kernel(x_ref, g_ref, b_ref, wq_ref, wqi_ref, wkv_ref, wkw_ref, ch_ref, sh_ref, ci_ref, si_ref,
                q_ref, qi_ref, k_ref, v_ref, kiwi_ref, kbf_ref, kibf_ref):
    xb = _layer_norm(x_ref[...], g_ref[...], b_ref[...]).astype(BF16)
    ch, sh, ci, si = ch_ref[...], sh_ref[...], ci_ref[...], si_ref[...]
    q = _rope(_dot(xb, wq_ref[...]), ch, sh, HEAD_DIM // ROT_FRACTION // 2)
    _store_head_major(q_ref, (q * QK_SCALE_LOG2E).astype(BF16))
    qi = _rope(_dot(xb, wqi_ref[...]), ci, si, IDX_DIM // ROT_FRACTION // 2)
    _store_head_major(qi_ref, qi.astype(BF16))
    kv = _dot(xb, wkv_ref[...])
    k = _rope(kv[:, :KV_DIM], ch, sh, HEAD_DIM // ROT_FRACTION // 2)
    kbf_ref[...] = k.astype(BF16)
    rows = k.shape[0]
    for n in range(N_KV_HEADS):
        k_ref[pl.ds(n, rows, stride=N_KV_HEADS), :] = k[:, n * HEAD_DIM:(n + 1) * HEAD_DIM]
        v_ref[pl.ds(n, rows, stride=N_KV_HEADS), :] = kv[:, KV_DIM + n * HEAD_DIM:KV_DIM + (n + 1) * HEAD_DIM]
    kw = _rope(_dot(xb, wkw_ref[...]), ci, si, IDX_DIM // ROT_FRACTION // 2)
    kiwi_ref[...] = kw
    kibf_ref[...] = kw.astype(BF16)


def _qkv_project(x, ln_g, ln_b, w, tabs, rows, seq):
    n = x.shape[0]
    tm = min(rows, n)
    tab_blocks = max(seq, tm) // tm
    row = lambda width: pl.BlockSpec((tm, width), lambda i: (i, 0))
    tab = pl.BlockSpec((tm, LANES), lambda i: (i % tab_blocks, 0))
    cache_rows = pl.BlockSpec((tm * N_KV_HEADS, HEAD_DIM), lambda i: (i, 0))
    hm_shape = lambda heads: (n // Q_TILE, heads, Q_TILE, LANES)
    hm = lambda heads: pl.BlockSpec((tm // Q_TILE, heads, Q_TILE, LANES), lambda i: (i, 0, 0, 0))
    out_shapes = (
        jax.ShapeDtypeStruct(hm_shape(N_HEADS), BF16),
        jax.ShapeDtypeStruct(hm_shape(IDX_HEADS), BF16),
        jax.ShapeDtypeStruct((n * N_KV_HEADS, HEAD_DIM), F32),
        jax.ShapeDtypeStruct((n * N_KV_HEADS, HEAD_DIM), F32),
        jax.ShapeDtypeStruct((n, LANES), F32),
        jax.ShapeDtypeStruct((n, KV_DIM), BF16),
        jax.ShapeDtypeStruct((n, LANES), BF16),
    )
    return pl.pallas_call(
        _qkv_kernel,
        grid=(n // tm,),
        in_specs=[row(D_MODEL), _resident((1, D_MODEL)), _resident((1, D_MODEL)),
                  _resident(w["q"].shape), _resident(w["qi"].shape), _resident(w["kv"].shape),
                  _resident(w["kw"].shape), tab, tab, tab, tab],
        out_specs=(hm(N_HEADS), hm(IDX_HEADS), cache_rows, cache_rows, row(LANES),
                   row(KV_DIM), row(LANES)),
        out_shape=out_shapes,
        compiler_params=_params("arbitrary"),
        name="qkv_proj",
    )(x, ln_g, ln_b, w["q"], w["qi"], w["kv"], w["kw"], *tabs)


def _sortable_key(score):
    bits = lax.bitcast_convert_type(score + 0.0, I32)
    return bits ^ ((bits >> 31) & INT_MAX)


def _dsa_kernel(*refs, past, seq, n_keys, topk, n_key_tiles, k_interleaved, n_cache_tiles):
    q_ref, qi_ref, kiwi_ref, k_ref, v_ref, ki_ref = refs[:6]
    kc_ref, vc_ref = refs[6:8] if n_cache_tiles else (None, None)
    o_ref, vt_ref, key_ref, acc_ref, m_ref, j_ref = refs[-6:]
    i = pl.program_id(1)
    tq, kt = Q_TILE, KEY_TILE
    uq = HEADS_PER_UNIT * tq

    def kv_tile(ref, j, n, interleaved):
        if interleaved:
            return ref[0, pl.ds(j * (kt * N_KV_HEADS) + n, kt, stride=N_KV_HEADS), :]
        return ref[0, pl.ds(pl.multiple_of(j * kt, kt), kt), n * HEAD_DIM:(n + 1) * HEAD_DIM]

    @pl.when(i == 0)
    def _():
        ones = jnp.ones((DENOM_ROWS, kt), BF16)
        for c in range(n_key_tiles):
            for n in range(N_KV_HEADS):
                v_tile = (kv_tile(vc_ref, c, n, True) if c < n_cache_tiles
                          else kv_tile(v_ref, c - n_cache_tiles, n, True))
                vt_ref[c, n, :HEAD_DIM, :] = v_tile.T.astype(BF16)
                vt_ref[c, n, HEAD_DIM:, :] = ones

    lane = lax.broadcasted_iota(I32, (1, tq), 1)
    t = i * tq + lane
    q_real = t < seq
    lim = jnp.minimum(((past + t) // CHUNK + 1) * CHUNK, n_keys)
    lim_max = jnp.minimum(((past + i * tq + tq - 1) // CHUNK + 1) * CHUNK, n_keys)
    n_tiles = (lim_max + kt - 1) // kt
    row = lax.broadcasted_iota(I32, (kt, tq), 0)

    wi = kiwi_ref[0].T[IDX_DIM:IDX_DIM + IDX_HEADS, :] * (IDX_DIM ** -0.5)

    def score_tile(j, n_nonneg):
        kib = ki_ref[0, pl.ds(pl.multiple_of(j * kt, kt), kt), :]
        rel_of = lambda u: _dot_nt(kib, qi_ref[0, 0, u * uq:(u + 1) * uq, :])
        n_units = IDX_HEADS // HEADS_PER_UNIT
        acc = jnp.zeros((kt, tq), F32)
        rel_next = rel_of(0)
        for u in range(n_units):
            rel, rel_next = rel_next, (rel_of(u + 1) if u + 1 < n_units else None)
            for g in range(HEADS_PER_UNIT):
                h = u * HEADS_PER_UNIT + g
                acc = acc + jnp.maximum(rel[:, g * tq:(g + 1) * tq], 0.0) * wi[h:h + 1, :]
        score = jnp.where(row < lim - j * kt, acc, -jnp.inf)
        key = _sortable_key(score)
        key_ref[j] = key
        for r in range(kt // COUNT_ROWS):
            n_nonneg = jnp.where(key[r * COUNT_ROWS:(r + 1) * COUNT_ROWS] >= 0, n_nonneg + 1, n_nonneg)
        return n_nonneg

    n_nonneg = lax.fori_loop(0, n_tiles, score_tile, jnp.zeros((COUNT_ROWS, tq), I32))

    def count(pred):
        def body(j, acc):
            blk = key_ref[j]
            for r in range(kt // COUNT_ROWS):
                rows = slice(r * COUNT_ROWS, (r + 1) * COUNT_ROWS)
                acc = jnp.where(pred(blk[rows], j * kt + row[rows]), acc + 1, acc)
            return acc
        acc = lax.fori_loop(0, n_tiles, body, jnp.zeros((COUNT_ROWS, tq), I32))
        return jnp.sum(acc, axis=0, keepdims=True)

    def count_ge(cand):
        return count(lambda keys, kidx: keys >= cand)

    def bisect(b, state):
        thr, cnt = state
        cand = thr | jnp.left_shift(jnp.int32(1), 30 - b)
        c = count_ge(cand)
        take = c >= topk
        return jnp.where(take, cand, thr), jnp.where(take, c, cnt)

    c0 = jnp.sum(n_nonneg, axis=0, keepdims=True)
    nonneg = c0 >= topk
    thr, cnt = lax.fori_loop(0, 31, bisect, (jnp.where(nonneg, 0, INT_MIN).astype(I32),
                                             jnp.where(nonneg, c0, n_tiles * kt).astype(I32)))
    tied = jnp.max(jnp.where(q_real, cnt, topk)) > topk

    j_ref[...] = jnp.full((1, tq), INT_MAX, I32)

    @pl.when(tied)
    def _():
        n_ties_wanted = topk - count(lambda keys, kidx: keys > thr)
        n_bits = int(n_key_tiles * kt).bit_length()

        def bisect_idx(it, cut):
            cand = cut | jnp.left_shift(jnp.int32(1), n_bits - 1 - it)
            c = count(lambda keys, kidx: (keys == thr) & (kidx < cand))
            return jnp.where(c <= n_ties_wanted, cand, cut)

        j_ref[...] = lax.fori_loop(0, n_bits, bisect_idx, jnp.zeros((1, tq), I32))

    j_cut = j_ref[...]

    m_ref[...] = jnp.full(m_ref.shape, MASKED_LOGIT, F32)
    acc_ref[...] = jnp.zeros(acc_ref.shape, F32)

    def attend_tile(j, carry, cached):
        n_units = N_HEADS // HEADS_PER_UNIT
        kv_of = lambda u: u * HEADS_PER_UNIT // KV_GROUP
        keys = key_ref[j]
        sel = ((keys > thr) | ((keys == thr) & (row < j_cut - j * kt))) & (row < lim - j * kt)

        def logits(u):
            kb = (kv_tile(kc_ref, j, kv_of(u), True) if cached
                  else kv_tile(k_ref, j - n_cache_tiles, kv_of(u), k_interleaved))
            return _dot_nt(kb.astype(BF16), q_ref[0, 0, u * uq:(u + 1) * uq, :])

        s_next = logits(0)
        for u in range(n_units):
            s, s_next = s_next, (logits(u + 1) if u + 1 < n_units else None)
            s = jnp.concatenate([jnp.where(sel, s[:, g * tq:(g + 1) * tq], MASKED_LOGIT)
                                 for g in range(HEADS_PER_UNIT)], axis=1)
            m_old = m_ref[u]
            m_new = jnp.maximum(m_old, jnp.max(s, axis=0, keepdims=True))
            p = jnp.exp2(s - m_new).astype(BF16)
            acc_ref[u] = jnp.exp2(m_old - m_new) * acc_ref[u] + _dot(vt_ref[j, kv_of(u)], p)
            m_ref[u] = m_new
        return carry

    if n_cache_tiles:
        lax.fori_loop(0, jnp.minimum(n_tiles, n_cache_tiles), functools.partial(attend_tile, cached=True), 0)
    lax.fori_loop(n_cache_tiles, n_tiles, functools.partial(attend_tile, cached=False), 0)

    for h in range(N_HEADS):
        u, cols = h // HEADS_PER_UNIT, slice((h % HEADS_PER_UNIT) * tq, (h % HEADS_PER_UNIT + 1) * tq)
        o = acc_ref[u, :HEAD_DIM, cols] / acc_ref[u, HEAD_DIM:HEAD_DIM + 1, cols]
        o_ref[0, :, h * HEAD_DIM:(h + 1) * HEAD_DIM] = o.T.astype(BF16)


def _dsa_attention(q, qi, kiwi, k, v, ki_bf, cache=(), *, past, seq, n_keys, k_interleaved):
    b, tq_pad, _ = kiwi.shape
    l_pad = ki_bf.shape[1]
    n_key_tiles = l_pad // KEY_TILE
    n_cache_tiles = cache[0].shape[1] // (KEY_TILE * N_KV_HEADS) if cache else 0
    topk = min(TOPK_MAX, n_keys // 4)
    qblk = lambda width: pl.BlockSpec((1, Q_TILE, width), lambda bi, i: (bi, i, 0))
    hmblk = lambda heads: pl.BlockSpec((1, 1, heads * Q_TILE, LANES), lambda bi, i: (bi, i, 0, 0))
    kblk = lambda a: pl.BlockSpec((1,) + a.shape[1:], lambda bi, i: (bi, 0, 0), pipeline_mode=pl.Buffered(1))
    kern = functools.partial(_dsa_kernel, past=past, seq=seq, n_keys=n_keys, topk=topk, n_key_tiles=n_key_tiles,
                             k_interleaved=k_interleaved, n_cache_tiles=n_cache_tiles)
    return pl.pallas_call(
        kern,
        grid=(b, tq_pad // Q_TILE),
        in_specs=[hmblk(N_HEADS), hmblk(IDX_HEADS), qblk(LANES), kblk(k), kblk(v), kblk(ki_bf)]
                 + [kblk(c) for c in cache],
        out_specs=qblk(ATTN_DIM),
        out_shape=jax.ShapeDtypeStruct((b, tq_pad, ATTN_DIM), BF16),
        scratch_shapes=[
            pltpu.VMEM((n_key_tiles, N_KV_HEADS, HEAD_DIM + DENOM_ROWS, KEY_TILE), BF16),
            pltpu.VMEM((n_key_tiles, KEY_TILE, Q_TILE), I32),
            pltpu.VMEM((N_HEADS // HEADS_PER_UNIT, HEAD_DIM + DENOM_ROWS, HEADS_PER_UNIT * Q_TILE), F32),
            pltpu.VMEM((N_HEADS // HEADS_PER_UNIT, 1, HEADS_PER_UNIT * Q_TILE), F32),
            pltpu.VMEM((1, Q_TILE), I32),
        ],
        compiler_params=_params("arbitrary", "arbitrary"),
        name="dsa_attention",
    )(q, qi, kiwi, k, v, ki_bf, *cache)


def _mixer_kernel(x_ref, attn_ref, memk_ref, memv_ref, prev_a_ref, prev_b_ref,
                  lng_ref, lnb_ref, cw_ref, ln1g_ref, ln1b_ref,
                  wgb_ref, wgc_ref, wci_ref, wmq_ref, wgl_ref, wao_ref, wco_ref, wmo_ref, wo_ref,
                  x1_ref, state_ref, carry_ref, *, seq, rows):
    i = pl.program_id(0)
    multi_seq = seq < rows
    xn = _layer_norm(x_ref[...], lng_ref[...], lnb_ref[...])
    xb = xn.astype(BF16)

    u = _dot(xb, wgc_ref[...]) * _dot(xb, wci_ref[...])
    ridx = lax.broadcasted_iota(I32, u.shape, 0)
    if multi_seq:
        tpos = ridx & (seq - 1)
        s1 = jnp.where(tpos == 0, prev_a_ref[...], pltpu.roll(u, 1, 0))
        s2 = jnp.where(tpos < 2, prev_b_ref[...], pltpu.roll(u, 2, 0))
        state_ref[...] = u
    else:
        blocks_per_seq = seq // rows
        @pl.when(i % blocks_per_seq == 0)
        def _():
            carry_ref[...] = prev_a_ref[0]

        prev = carry_ref[...]
        p1, p2 = prev[SUBLANES - 1:SUBLANES, :], prev[SUBLANES - 2:SUBLANES - 1, :]
        s1 = jnp.where(ridx == 0, p1, pltpu.roll(u, 1, 0))
        s2 = jnp.where(ridx == 0, p2, jnp.where(ridx == 1, p1, pltpu.roll(u, 2, 0)))
        carry_ref[...] = u[rows - SUBLANES:, :]
        state_ref[0] = u[rows - SUBLANES:, :]
    cw = cw_ref[...]
    conv = cw[0:1, :] * s2 + cw[1:2, :] * s1 + cw[2:3, :] * u
    conv_in = (_dot(xb, wgb_ref[...]) * conv).astype(BF16)

    mq = _dot(xb, wmq_ref[...]).astype(BF16)
    n_batches = max(1, rows // seq)
    rb = rows // n_batches
    mem_rows = []
    for bi in range(n_batches):
        heads = []
        for h in range(MEM_HEADS):
            cols = slice(h * MEM_HEAD_DIM, (h + 1) * MEM_HEAD_DIM)
            s = _dot_nt(mq[bi * rb:(bi + 1) * rb, cols], memk_ref[bi, :, cols]) * (MEM_HEAD_DIM ** -0.5)
            e = jnp.exp(s - jnp.max(s, axis=-1, keepdims=True))
            p = e / jnp.sum(e, axis=-1, keepdims=True)
            heads.append(_dot(p.astype(BF16), memv_ref[bi, :, cols]))
        mem_rows.append(jnp.concatenate(heads, axis=1))
    mem = (mem_rows[0] if n_batches == 1 else jnp.concatenate(mem_rows, axis=0)).astype(BF16)

    gate = _sigmoid(_dot(xb, wgl_ref[...]))
    mixed = (gate[:, :D_MODEL] * _dot(attn_ref[...], wao_ref[...])
             + gate[:, D_MODEL:2 * D_MODEL] * _dot(conv_in, wco_ref[...])
             + gate[:, 2 * D_MODEL:] * _dot(mem, wmo_ref[...]))
    y = ALPHA * xn + _dot(mixed.astype(BF16), wo_ref[...])
    x1_ref[...] = _layer_norm(y, ln1g_ref[...], ln1b_ref[...])


def _mixer(x, attn, mem_k, mem_v, prev_a, prev_b, ln_g, ln_b, conv_w8, ln1_g, ln1_b, w, *, seq, rows):
    n = x.shape[0]
    tm = min(rows, n)
    multi_seq = seq < tm
    n_batches = max(1, tm // seq)
    row = lambda width: pl.BlockSpec((tm, width), lambda i: (i, 0))
    mem = pl.BlockSpec((n_batches, N_MEM, MEM_DIM), lambda i: ((i * tm) // (seq * n_batches), 0, 0))
    vec = _resident((1, D_MODEL))
    if multi_seq:
        prev_spec = row(D_CONV)
        state_spec = row(D_CONV)
        state_shape = jax.ShapeDtypeStruct((n, D_CONV), F32)
    else:
        prev_spec = pl.BlockSpec((1, SUBLANES, D_CONV), lambda i: ((i * tm) // seq, 0, 0))
        state_spec = prev_spec
        state_shape = jax.ShapeDtypeStruct((n // seq, SUBLANES, D_CONV), F32)
    weights = [w[k] for k in ("gb", "gc", "ci", "mq", "gl", "ao", "co", "mo", "o")]
    kern = functools.partial(_mixer_kernel, seq=seq, rows=tm)
    return pl.pallas_call(
        kern,
        grid=(n // tm,),
        in_specs=[row(D_MODEL), row(ATTN_DIM), mem, mem, prev_spec, prev_spec,
                  vec, vec, _resident((SUBLANES, D_CONV)), vec, vec] + [_resident(m.shape) for m in weights],
        out_specs=(row(D_MODEL), state_spec),
        out_shape=(jax.ShapeDtypeStruct((n, D_MODEL), F32), state_shape),
        scratch_shapes=[pltpu.VMEM((SUBLANES, D_CONV), F32)],
        compiler_params=_params("arbitrary"),
        name="mixer",
    )(x, attn, mem_k, mem_v, prev_a, prev_b, ln_g, ln_b, conv_w8, ln1_g, ln1_b, *weights)


def _route(x, wr_hi, wr_lo, bias):
    tm = x.shape[0]
    x_hi = x.astype(BF16)
    x_lo = (x - x_hi.astype(F32)).astype(BF16)
    logits = _dot_nt(wr_hi, x_hi) + (_dot_nt(wr_hi, x_lo) + _dot_nt(wr_lo, x_hi))
    s = _sigmoid(logits)
    sc = s + bias[:, 0:1]
    sub = lax.broadcasted_iota(I32, (GROUP_SIZE, tm), 0)
    neg = -jnp.inf
    groups, gscore = [], []
    for g in range(N_GROUPS):
        xg = sc[g * GROUP_SIZE:(g + 1) * GROUP_SIZE, :]
        m1 = jnp.max(xg, axis=0, keepdims=True)
        i1 = jnp.min(jnp.where(xg == m1, sub, GROUP_SIZE), axis=0, keepdims=True)
        m2 = jnp.max(jnp.where(sub == i1, neg, xg), axis=0, keepdims=True)
        groups.append(xg)
        gscore.append(m1 + m2)
    masked = []
    for g in range(N_GROUPS):
        rank = jnp.zeros((1, tm), I32)
        for o in range(N_GROUPS):
            if o == g:
                continue
            beats = (gscore[o] > gscore[g]) | ((gscore[o] == gscore[g]) & (o < g))
            rank = rank + beats.astype(I32)
        masked.append(jnp.where(rank < TOPK_GROUPS, groups[g], neg))
    v = jnp.concatenate(masked, axis=0)
    eidx = lax.broadcasted_iota(I32, (N_EXPERTS, tm), 0)
    chosen = jnp.zeros((N_EXPERTS, tm), jnp.bool_)
    for _ in range(TOP_K):
        m = jnp.max(v, axis=0, keepdims=True)
        first = jnp.min(jnp.where(v == m, eidx, N_EXPERTS), axis=0, keepdims=True)
        hit = eidx == first
        chosen = chosen | hit
        v = jnp.where(hit, neg, v)
    wsel = jnp.where(chosen, s, 0.0)
    return wsel / jnp.sum(wsel, axis=0, keepdims=True) * ROUTED_SCALE


def _expert_ffn(xb, w_gu, w_down, gate_col=None):
    gu = _dot(xb, w_gu)
    g, u = gu[:, :D_EXPERT], gu[:, D_EXPERT:]
    h = g * _sigmoid(g) * u
    if gate_col is not None:
        h = h * gate_col
    return _dot(h.astype(BF16), w_down)


def _moe_kernel(x_ref, wrh_ref, wrl_ref, br_ref, wgu_ref, wd_ref, wsgu_ref, wsd_ref, g_ref, b_ref,
                o_ref, xb_ref, gate_ref, acc_ref, *, experts_per_step):
    j = pl.program_id(1)

    @pl.when(j == 0)
    def _():
        x = x_ref[...]
        gate_t = _route(x, wrh_ref[...], wrl_ref[...], br_ref[...])
        pad = jnp.zeros((LANES - N_EXPERTS, x.shape[0]), F32)
        gate_ref[...] = jnp.concatenate([gate_t, pad], axis=0).T
        xb = x.astype(BF16)
        xb_ref[...] = xb
        acc_ref[...] = _expert_ffn(xb, wsgu_ref[...], wsd_ref[...])

    xb = xb_ref[...]
    gate = gate_ref[...]
    lane = lax.broadcasted_iota(I32, gate.shape, 1)
    acc = acc_ref[...]
    for e in range(experts_per_step):
        col = jnp.sum(jnp.where(lane == j * experts_per_step + e, gate, 0.0), axis=1, keepdims=True)
        acc = acc + _expert_ffn(xb, wgu_ref[e], wd_ref[e], col)
    acc_ref[...] = acc

    @pl.when(j == pl.num_programs(1) - 1)
    def _():
        o_ref[...] = _layer_norm(ALPHA * x_ref[...] + acc_ref[...], g_ref[...], b_ref[...])


def _moe(x, w, ln_g, ln_b, rows):
    n = x.shape[0]
    tm = min(rows, n)
    eps = MOE_EXPERTS_PER_STEP
    row = pl.BlockSpec((tm, D_MODEL), lambda i, j: (i, 0))
    kern = functools.partial(_moe_kernel, experts_per_step=eps)
    return pl.pallas_call(
        kern,
        grid=(n // tm, N_EXPERTS // eps),
        in_specs=[row, _resident(w["r_hi"].shape), _resident(w["r_lo"].shape), _resident(w["r_b"].shape),
                  pl.BlockSpec((eps, D_MODEL, 2 * D_EXPERT), lambda i, j: (j, 0, 0)),
                  pl.BlockSpec((eps, D_EXPERT, D_MODEL), lambda i, j: (j, 0, 0)),
                  _resident(w["s_gu"].shape), _resident(w["s_d"].shape),
                  _resident((1, D_MODEL)), _resident((1, D_MODEL))],
        out_specs=row,
        out_shape=jax.ShapeDtypeStruct((n, D_MODEL), F32),
        scratch_shapes=[pltpu.VMEM((tm, D_MODEL), BF16), pltpu.VMEM((tm, LANES), F32),
                        pltpu.VMEM((tm, D_MODEL), F32)],
        compiler_params=_params("arbitrary", "arbitrary"),
        name="moe",
    )(x, w["r_hi"], w["r_lo"], w["r_b"], w["e_gu"], w["e_d"], w["s_gu"], w["s_d"], ln_g, ln_b)


def _pad_rows(a, rows):
    return a if a.shape[1] == rows else jnp.pad(a, ((0, 0), (0, rows - a.shape[1]), (0, 0)))


def _head_major_per_batch(x, b, seq, tq_pad):
    heads = x.shape[1]
    if seq % Q_TILE == 0:
        return x.reshape(b, seq // Q_TILE, heads * Q_TILE, LANES)
    per_tile = Q_TILE // seq
    x = x.reshape(-1, heads, per_tile, seq, LANES).transpose(0, 2, 1, 3, 4).reshape(b, heads, seq, LANES)
    x = jnp.pad(x, ((0, 0), (0, 0), (0, tq_pad - seq), (0, 0)))
    return x.reshape(b, 1, heads * tq_pad, LANES)


def _layer(x, mem_k_bf, mem_v_bf, past_k, past_v, past_ki, conv_prev, w, ln, layer):
    b, seq, _ = x.shape
    past = past_k.shape[1]
    n = b * seq
    n_keys = past + seq
    x2 = x.reshape(n, D_MODEL)

    reps = max(1, min(PROJ_ROWS, n) // seq)
    pos_t = np.tile(past + np.arange(seq), reps)
    tabs = (_rope_tables(pos_t, HEAD_DIM // ROT_FRACTION // 2)
            + _rope_tables(pos_t, IDX_DIM // ROT_FRACTION // 2,
                           (IDX_DIM, IDX_DIM + IDX_HEADS), IDX_HEADS ** -0.5))
    q, qi, k, v, kiwi, k_bf, ki_bf = _qkv_project(x2, ln["in_g"], ln["in_b"], w, tabs, PROJ_ROWS, seq)

    l_pad = -(-n_keys // KEY_TILE) * KEY_TILE
    tq_pad = -(-seq // Q_TILE) * Q_TILE
    rows = lambda a, length: a.reshape(b, length * N_KV_HEADS, HEAD_DIM)
    cache = ()
    k_interleaved = past > 0
    if past > 0 and past % KEY_TILE == 0:
        cache = (rows(past_k, past), rows(past_v, past))
        new_rows = (l_pad - past) * N_KV_HEADS
        k3, v3 = _pad_rows(rows(k, seq), new_rows), _pad_rows(rows(v, seq), new_rows)
    else:
        with_cache = lambda new, old: new if past == 0 else jnp.concatenate([rows(old, past), new], axis=1)
        v3 = _pad_rows(with_cache(rows(v, seq), past_v), l_pad * N_KV_HEADS)
        if k_interleaved:
            k3 = _pad_rows(with_cache(rows(k, seq), past_k), l_pad * N_KV_HEADS)
        else:
            k3 = _pad_rows(k_bf.reshape(b, seq, KV_DIM), l_pad)
    ki_past = jnp.pad(past_ki, ((0, 0), (0, 0), (0, LANES - IDX_DIM))).astype(BF16)
    ki3 = jnp.concatenate([ki_past, ki_bf.reshape(b, seq, LANES)], axis=1)
    attn = _dsa_attention(
        _head_major_per_batch(q, b, seq, tq_pad), _head_major_per_batch(qi, b, seq, tq_pad),
        _pad_rows(kiwi.reshape(b, seq, LANES), tq_pad), k3, v3, _pad_rows(ki3, l_pad), cache,
        past=past, seq=seq, n_keys=n_keys, k_interleaved=k_interleaved)
    attn = attn[:, :seq].reshape(n, ATTN_DIM)

    tm = min(PROJ_ROWS, n)
    if seq < tm:
        tpos = jnp.arange(seq)[None, :, None]
        prev_a = jnp.where(tpos == 0, conv_prev[:, 1:2], 0.0).reshape(n, D_CONV)
        prev_b = jnp.where(tpos == 0, conv_prev[:, 0:1], jnp.where(tpos == 1, conv_prev[:, 1:2], 0.0)).reshape(n, D_CONV)
    else:
        prev_a = jnp.pad(conv_prev, ((0, 0), (SUBLANES - (CONV_WIDTH - 1), 0), (0, 0)))
        prev_b = prev_a
    x1, state = _mixer(x2, attn, mem_k_bf, mem_v_bf, prev_a, prev_b, ln["in_g"], ln["in_b"], w["conv"],
                       ln["g1"], ln["b1"], w, seq=seq, rows=PROJ_ROWS)
    if seq < tm:
        new_conv = state.reshape(b, seq, D_CONV)[:, seq - (CONV_WIDTH - 1):]
    else:
        new_conv = state[:, SUBLANES - (CONV_WIDTH - 1):]

    y = _moe(x1, w, ln["g2"], ln["b2"], MOE_ROWS)
    return (y.reshape(b, seq, D_MODEL), k.reshape(b, seq, N_KV_HEADS, HEAD_DIM),
            v.reshape(b, seq, N_KV_HEADS, HEAD_DIM), kiwi[:, :IDX_DIM].reshape(b, seq, IDX_DIM), new_conv)


def _prep_weights(l, w_in, conv_w, w_attn_out, w_conv_out, w_mem_out, w_o, w_router, b_router,
                  w_exp_gu, w_exp_down, w_sh_gu, w_sh_down):
    sizes = [ATTN_DIM, KV_DIM, KV_DIM, IDX_HEADS * IDX_DIM, IDX_DIM, IDX_HEADS,
             D_CONV, D_CONV, D_CONV, MEM_DIM, N_BRANCH * D_MODEL]
    offs = np.concatenate([[0], np.cumsum(sizes)]).tolist()
    wi = w_in[l]
    part = lambda j: wi[:, offs[j]:offs[j + 1]]
    w_qi = part(3).reshape(D_MODEL, IDX_HEADS, IDX_DIM)
    w_qi = jnp.pad(w_qi, ((0, 0), (0, 0), (0, LANES - IDX_DIM))).reshape(D_MODEL, IDX_HEADS * LANES)
    w_kw = jnp.pad(jnp.concatenate([part(4), part(5)], axis=1), ((0, 0), (0, LANES - IDX_DIM - IDX_HEADS)))
    wr = w_router[l].T
    wr_hi = wr.astype(BF16)
    return {
        "q": part(0).astype(BF16), "kv": jnp.concatenate([part(1), part(2)], axis=1).astype(BF16),
        "qi": w_qi.astype(BF16), "kw": w_kw.astype(BF16),
        "gb": part(6).astype(BF16), "gc": part(7).astype(BF16), "ci": part(8).astype(BF16),
        "mq": part(9).astype(BF16), "gl": part(10).astype(BF16),
        "ao": w_attn_out[l].astype(BF16), "co": w_conv_out[l].astype(BF16), "mo": w_mem_out[l].astype(BF16),
        "o": w_o[l].astype(BF16),
        "conv": jnp.pad(conv_w[l], ((0, SUBLANES - CONV_WIDTH), (0, 0))),
        "r_hi": wr_hi, "r_lo": (wr - wr_hi.astype(F32)).astype(BF16),
        "r_b": jnp.broadcast_to(b_router[l][:, None], (N_EXPERTS, LANES)),
        "e_gu": w_exp_gu[l].astype(BF16), "e_d": w_exp_down[l].astype(BF16),
        "s_gu": w_sh_gu[l].astype(BF16), "s_d": w_sh_down[l].astype(BF16),
    }


def kernel(x_prompt, x_sample, mem_prompt, cache_attn_k, cache_attn_v, cache_idx_k, cache_mem_k, cache_mem_v,
           state_conv, ln_in_g, ln_in_b, w_in, conv_w, w_mem_kv, w_attn_out, w_conv_out, w_mem_out, w_o,
           ln1_g, ln1_b, w_router, b_router, w_exp_gu, w_exp_down, w_sh_gu, w_sh_down, ln2_g, ln2_b):
    assert DEPTH == 1 and w_in.shape[0] == 1
    l = 0
    bp, n_mem = mem_prompt.shape[0], mem_prompt.shape[1]
    w = _prep_weights(l, w_in, conv_w, w_attn_out, w_conv_out, w_mem_out, w_o, w_router, b_router,
                      w_exp_gu, w_exp_down, w_sh_gu, w_sh_down)
    ln = {"in_g": ln_in_g[None, :], "in_b": ln_in_b[None, :], "g1": ln1_g[l][None, :], "b1": ln1_b[l][None, :],
          "g2": ln2_g[l][None, :], "b2": ln2_b[l][None, :]}

    mem_kv = _project(mem_prompt.reshape(bp * n_mem, D_MODEL), w_mem_kv[l].astype(BF16))
    mem_k = mem_kv[:, :MEM_DIM].reshape(bp, n_mem, MEM_DIM)
    mem_v = mem_kv[:, MEM_DIM:].reshape(bp, n_mem, MEM_DIM)

    zeros = lambda *s: jnp.zeros(s, F32)
    yp, kp, vp, kip, cp = _layer(
        x_prompt, mem_k.astype(BF16), mem_v.astype(BF16),
        zeros(bp, 0, N_KV_HEADS, HEAD_DIM), zeros(bp, 0, N_KV_HEADS, HEAD_DIM), zeros(bp, 0, IDX_DIM),
        zeros(bp, CONV_WIDTH - 1, D_CONV), w, ln, l)
    bs = x_sample.shape[0]
    ys, ks, vs, kis, cs = _layer(
        x_sample, cache_mem_k[l].reshape(bs, n_mem, MEM_DIM).astype(BF16),
        cache_mem_v[l].reshape(bs, n_mem, MEM_DIM).astype(BF16),
        cache_attn_k[l], cache_attn_v[l], cache_idx_k[l], state_conv[l], w, ln, l)

    mem_shape = (1, bp, n_mem, MEM_HEADS, MEM_HEAD_DIM)
    return (yp, ys, kp[None], vp[None], kip[None], mem_k.reshape(mem_shape), mem_v.reshape(mem_shape), cp[None],
            ks[None], vs[None], kis[None], cs[None])
```

```python
import functools

import jax
import jax.numpy as jnp
import numpy as np
from jax import lax
from jax.experimental import pallas as pl
from jax.experimental.pallas import tpu as pltpu

D_MODEL = 1024
DEPTH = 1
CHUNK = 64
HEAD_DIM = 128
N_HEADS = D_MODEL // HEAD_DIM
N_KV_HEADS = max(1, N_HEADS // 4)
KV_GROUP = N_HEADS // N_KV_HEADS
ATTN_DIM = N_HEADS * HEAD_DIM
KV_DIM = N_KV_HEADS * HEAD_DIM
ROT_FRACTION = 4
ROPE_THETA = 500000.0
IDX_HEADS = 8
IDX_DIM = 64
TOPK_MAX = 256
D_CONV = D_MODEL
CONV_WIDTH = 3
N_MEM = 256
MEM_HEADS = 4
MEM_HEAD_DIM = D_MODEL // MEM_HEADS
MEM_DIM = MEM_HEADS * MEM_HEAD_DIM
N_BRANCH = 3
N_EXPERTS = 64
TOP_K = 8
N_GROUPS = 8
GROUP_SIZE = N_EXPERTS // N_GROUPS
TOPK_GROUPS = 4
D_EXPERT = D_MODEL // 4
ROUTED_SCALE = 2.5
LN_EPS = 1e-5
ALPHA = (2.0 * DEPTH) ** 0.25

LANES = 128
SUBLANES = 8
VMEM_LIMIT_BYTES = 62 * 1024 * 1024

Q_TILE = 128
KEY_TILE = 1024
HEADS_PER_UNIT = 2
COUNT_ROWS = 64
DENOM_ROWS = 16
PROJ_ROWS = 512
MOE_ROWS = 1024
MOE_EXPERTS_PER_STEP = 8

F32 = jnp.float32
BF16 = jnp.bfloat16
I32 = jnp.int32
INT_MIN = -(2 ** 31)
INT_MAX = 2 ** 31 - 1
MASKED_LOGIT = -1e30
QK_SCALE_LOG2E = HEAD_DIM ** -0.5 * float(np.log2(np.e))

_NT = (((1,), (1,)), ((), ()))


def _dot(a, b):
    return jnp.dot(a, b, preferred_element_type=F32)


def _dot_nt(a, b):
    return lax.dot_general(a, b, _NT, preferred_element_type=F32)


def _layer_norm(x, g, b):
    mu = jnp.mean(x, axis=-1, keepdims=True)
    xc = x - mu
    var = jnp.mean(xc * xc, axis=-1, keepdims=True)
    return xc * lax.rsqrt(var + LN_EPS) * g + b


def _sigmoid(x):
    return 1.0 / (1.0 + jnp.exp(-x))


def _resident(shape):
    nd = len(shape)
    return pl.BlockSpec(shape, lambda *_: (0,) * nd, pipeline_mode=pl.Buffered(1))


def _params(*sem):
    return pltpu.CompilerParams(dimension_semantics=sem, vmem_limit_bytes=VMEM_LIMIT_BYTES)


def _rope_tables(pos, half, extra_scale_lanes=None, extra_scale=1.0):
    inv_freq = ROPE_THETA ** (-np.arange(half, dtype=np.float64) / half)
    ang = pos.astype(np.float64)[:, None] * inv_freq[None, :]
    cos, sin = np.cos(ang), np.sin(ang)
    n = pos.shape[0]
    c = np.concatenate([cos, cos, np.ones((n, LANES - 2 * half))], axis=1)
    s = np.concatenate([-sin, sin, np.zeros((n, LANES - 2 * half))], axis=1)
    if extra_scale_lanes is not None:
        lo, hi = extra_scale_lanes
        c[:, lo:hi] = extra_scale
    return jnp.asarray(c, F32), jnp.asarray(s, F32)


def _rope(x, c, s, half):
    rows, width = x.shape
    lane = lax.broadcasted_iota(I32, (rows, LANES), 1)
    first = lane < half
    outs = []
    for j in range(width // LANES):
        xj = x[:, j * LANES:(j + 1) * LANES]
        swap = jnp.where(first, pltpu.roll(xj, LANES - half, 1), pltpu.roll(xj, half, 1))
        outs.append(xj * c + swap * s)
    return outs[0] if len(outs) == 1 else jnp.concatenate(outs, axis=1)


def _proj_kernel(x_ref, w_ref, o_ref):
    o_ref[...] = _dot(x_ref[...].astype(BF16), w_ref[...])


def _project(x, w_bf, col_tile=512):
    n, d = x.shape
    m = w_bf.shape[1]
    return pl.pallas_call(
        _proj_kernel,
        grid=(m // col_tile,),
        in_specs=[pl.BlockSpec((n, d), lambda j: (0, 0)),
                  pl.BlockSpec((d, col_tile), lambda j: (0, j))],
        out_specs=pl.BlockSpec((n, col_tile), lambda j: (0, j)),
        out_shape=jax.ShapeDtypeStruct((n, m), F32),
        compiler_params=_params("arbitrary"),
        name="mem_kv_proj",
    )(x, w_bf)


def _store_head_major(ref, x):
    for r in range(ref.shape[0]):
        for h in range(ref.shape[1]):
            ref[r, h] = x[r * Q_TILE:(r + 1) * Q_TILE, h * LANES:(h + 1) * LANES]


def _qkv_kernel(x_ref, g_ref, b_ref, wq_ref, wqi_ref, wkv_ref, wkw_ref, ch_ref, sh_ref, ci_ref, si_ref,
                q_ref, qi_ref, k_ref, v_ref, kiwi_ref, kbf_ref, kibf_ref):
    xb = _layer_norm(x_ref[...], g_ref[...], b_ref[...]).astype(BF16)
    ch, sh, ci, si = ch_ref[...], sh_ref[...], ci_ref[...], si_ref[...]
    q = _rope(_dot(xb, wq_ref[...]), ch, sh, HEAD_DIM // ROT_FRACTION // 2)
    _store_head_major(q_ref, (q * QK_SCALE_LOG2E).astype(BF16))
    qi = _rope(_dot(xb, wqi_ref[...]), ci, si, IDX_DIM // ROT_FRACTION // 2)
    _store_head_major(qi_ref, qi.astype(BF16))
    kv = _dot(xb, wkv_ref[...])
    k = _rope(kv[:, :KV_DIM], ch, sh, HEAD_DIM // ROT_FRACTION // 2)
    kbf_ref[...] = k.astype(BF16)
    rows = k.shape[0]
    for n in range(N_KV_HEADS):
        k_ref[pl.ds(n, rows, stride=N_KV_HEADS), :] = k[:, n * HEAD_DIM:(n + 1) * HEAD_DIM]
        v_ref[pl.ds(n, rows, stride=N_KV_HEADS), :] = kv[:, KV_DIM + n * HEAD_DIM:KV_DIM + (n + 1) * HEAD_DIM]
    kw = _rope(_dot(xb, wkw_ref[...]), ci, si, IDX_DIM // ROT_FRACTION // 2)
    kiwi_ref[...] = kw
    kibf_ref[...] = kw.astype(BF16)


def _qkv_project(x, ln_g, ln_b, w, tabs, rows, seq):
    n = x.shape[0]
    tm = min(rows, n)
    tab_blocks = max(seq, tm) // tm
    row = lambda width: pl.BlockSpec((tm, width), lambda i: (i, 0))
    tab = pl.BlockSpec((tm, LANES), lambda i: (i % tab_blocks, 0))
    cache_rows = pl.BlockSpec((tm * N_KV_HEADS, HEAD_DIM), lambda i: (i, 0))
    hm_shape = lambda heads: (n // Q_TILE, heads, Q_TILE, LANES)
    hm = lambda heads: pl.BlockSpec((tm // Q_TILE, heads, Q_TILE, LANES), lambda i: (i, 0, 0, 0))
    out_shapes = (
        jax.ShapeDtypeStruct(hm_shape(N_HEADS), BF16),
        jax.ShapeDtypeStruct(hm_shape(IDX_HEADS), BF16),
        jax.ShapeDtypeStruct((n * N_KV_HEADS, HEAD_DIM), F32),
        jax.ShapeDtypeStruct((n * N_KV_HEADS, HEAD_DIM), F32),
        jax.ShapeDtypeStruct((n, LANES), F32),
        jax.ShapeDtypeStruct((n, KV_DIM), BF16),
        jax.ShapeDtypeStruct((n, LANES), BF16),
    )
    return pl.pallas_call(
        _qkv_kernel,
        grid=(n // tm,),
        in_specs=[row(D_MODEL), _resident((1, D_MODEL)), _resident((1, D_MODEL)),
                  _resident(w["q"].shape), _resident(w["qi"].shape), _resident(w["kv"].shape),
                  _resident(w["kw"].shape), tab, tab, tab, tab],
        out_specs=(hm(N_HEADS), hm(IDX_HEADS), cache_rows, cache_rows, row(LANES),
                   row(KV_DIM), row(LANES)),
        out_shape=out_shapes,
        compiler_params=_params("arbitrary"),
        name="qkv_proj",
    )(x, ln_g, ln_b, w["q"], w["qi"], w["kv"], w["kw"], *tabs)


def _sortable_key(score):
    bits = lax.bitcast_convert_type(score + 0.0, I32)
    return bits ^ ((bits >> 31) & INT_MAX)


def _dsa_kernel(*refs, past, seq, n_keys, topk, n_key_tiles, k_interleaved, n_cache_tiles):
    q_ref, qi_ref, kiwi_ref, k_ref, v_ref, ki_ref = refs[:6]
    kc_ref, vc_ref = refs[6:8] if n_cache_tiles else (None, None)
    o_ref, vt_ref, key_ref, acc_ref, m_ref, j_ref = refs[-6:]
    i = pl.program_id(1)
    tq, kt = Q_TILE, KEY_TILE
    uq = HEADS_PER_UNIT * tq

    def kv_tile(ref, j, n, interleaved):
        if interleaved:
            return ref[0, pl.ds(j * (kt * N_KV_HEADS) + n, kt, stride=N_KV_HEADS), :]
        return ref[0, pl.ds(pl.multiple_of(j * kt, kt), kt), n * HEAD_DIM:(n + 1) * HEAD_DIM]

    @pl.when(i == 0)
    def _():
        ones = jnp.ones((DENOM_ROWS, kt), BF16)
        for c in range(n_key_tiles):
            for n in range(N_KV_HEADS):
                v_tile = (kv_tile(vc_ref, c, n, True) if c < n_cache_tiles
                          else kv_tile(v_ref, c - n_cache_tiles, n, True))
                vt_ref[c, n, :HEAD_DIM, :] = v_tile.T.astype(BF16)
                vt_ref[c, n, HEAD_DIM:, :] = ones

    lane = lax.broadcasted_iota(I32, (1, tq), 1)
    t = i * tq + lane
    q_real = t < seq
    lim = jnp.minimum(((past + t) // CHUNK + 1) * CHUNK, n_keys)
    lim_max = jnp.minimum(((past + i * tq + tq - 1) // CHUNK + 1) * CHUNK, n_keys)
    n_tiles = (lim_max + kt - 1) // kt
    row = lax.broadcasted_iota(I32, (kt, tq), 0)

    wi = kiwi_ref[0].T[IDX_DIM:IDX_DIM + IDX_HEADS, :] * (IDX_DIM ** -0.5)

    def score_tile(j, n_nonneg):
        kib = ki_ref[0, pl.ds(pl.multiple_of(j * kt, kt), kt), :]
        rel_of = lambda u: _dot_nt(kib, qi_ref[0, 0, u * uq:(u + 1) * uq, :])
        n_units = IDX_HEADS // HEADS_PER_UNIT
        acc = jnp.zeros((kt, tq), F32)
        rel_next = rel_of(0)
        for u in range(n_units):
            rel, rel_next = rel_next, (rel_of(u + 1) if u + 1 < n_units else None)
            for g in range(HEADS_PER_UNIT):
                h = u * HEADS_PER_UNIT + g
                acc = acc + jnp.maximum(rel[:, g * tq:(g + 1) * tq], 0.0) * wi[h:h + 1, :]
        score = jnp.where(row < lim - j * kt, acc, -jnp.inf)
        key = _sortable_key(score)
        key_ref[j] = key
        for r in range(kt // COUNT_ROWS):
            n_nonneg = jnp.where(key[r * COUNT_ROWS:(r + 1) * COUNT_ROWS] >= 0, n_nonneg + 1, n_nonneg)
        return n_nonneg

    n_nonneg = lax.fori_loop(0, n_tiles, score_tile, jnp.zeros((COUNT_ROWS, tq), I32))

    def count(pred):
        def body(j, acc):
            blk = key_ref[j]
            for r in range(kt // COUNT_ROWS):
                rows = slice(r * COUNT_ROWS, (r + 1) * COUNT_ROWS)
                acc = jnp.where(pred(blk[rows], j * kt + row[rows]), acc + 1, acc)
            return acc
        acc = lax.fori_loop(0, n_tiles, body, jnp.zeros((COUNT_ROWS, tq), I32))
        return jnp.sum(acc, axis=0, keepdims=True)

    def count_ge(cand):
        return count(lambda keys, kidx: keys >= cand)

    def bisect(b, state):
        thr, cnt = state
        cand = thr | jnp.left_shift(jnp.int32(1), 30 - b)
        c = count_ge(cand)
        take = c >= topk
        return jnp.where(take, cand, thr), jnp.where(take, c, cnt)

    c0 = jnp.sum(n_nonneg, axis=0, keepdims=True)
    nonneg = c0 >= topk
    thr, cnt = lax.fori_loop(0, 31, bisect, (jnp.where(nonneg, 0, INT_MIN).astype(I32),
                                             jnp.where(nonneg, c0, n_tiles * kt).astype(I32)))
    tied = jnp.max(jnp.where(q_real, cnt, topk)) > topk

    j_ref[...] = jnp.full((1, tq), INT_MAX, I32)

    @pl.when(tied)
    def _():
        n_ties_wanted = topk - count(lambda keys, kidx: keys > thr)
        n_bits = int(n_key_tiles * kt).bit_length()

        def bisect_idx(it, cut):
            cand = cut | jnp.left_shift(jnp.int32(1), n_bits - 1 - it)
            c = count(lambda keys, kidx: (keys == thr) & (kidx < cand))
            return jnp.where(c <= n_ties_wanted, cand, cut)

        j_ref[...] = lax.fori_loop(0, n_bits, bisect_idx, jnp.zeros((1, tq), I32))

    j_cut = j_ref[...]

    m_ref[...] = jnp.full(m_ref.shape, MASKED_LOGIT, F32)
    acc_ref[...] = jnp.zeros(acc_ref.shape, F32)

    def attend_tile(j, carry, cached):
        n_units = N_HEADS // HEADS_PER_UNIT
        kv_of = lambda u: u * HEADS_PER_UNIT // KV_GROUP
        keys = key_ref[j]
        sel = ((keys > thr) | ((keys == thr) & (row < j_cut - j * kt))) & (row < lim - j * kt)

        def logits(u):
            kb = (kv_tile(kc_ref, j, kv_of(u), True) if cached
                  else kv_tile(k_ref, j - n_cache_tiles, kv_of(u), k_interleaved))
            return _dot_nt(kb.astype(BF16), q_ref[0, 0, u * uq:(u + 1) * uq, :])

        s_next = logits(0)
        for u in range(n_units):
            s, s_next = s_next, (logits(u + 1) if u + 1 < n_units else None)
            s = jnp.concatenate([jnp.where(sel, s[:, g * tq:(g + 1) * tq], MASKED_LOGIT)
                                 for g in range(HEADS_PER_UNIT)], axis=1)
            m_old = m_ref[u]
            m_new = jnp.maximum(m_old, jnp.max(s, axis=0, keepdims=True))
            p = jnp.exp2(s - m_new).astype(BF16)
            acc_ref[u] = jnp.exp2(m_old - m_new) * acc_ref[u] + _dot(vt_ref[j, kv_of(u)], p)
            m_ref[u] = m_new
        return carry

    if n_cache_tiles:
        lax.fori_loop(0, jnp.minimum(n_tiles, n_cache_tiles), functools.partial(attend_tile, cached=True), 0)
    lax.fori_loop(n_cache_tiles, n_tiles, functools.partial(attend_tile, cached=False), 0)

    for h in range(N_HEADS):
        u, cols = h // HEADS_PER_UNIT, slice((h % HEADS_PER_UNIT) * tq, (h % HEADS_PER_UNIT + 1) * tq)
        o = acc_ref[u, :HEAD_DIM, cols] / acc_ref[u, HEAD_DIM:HEAD_DIM + 1, cols]
        o_ref[0, :, h * HEAD_DIM:(h + 1) * HEAD_DIM] = o.T.astype(BF16)


def _dsa_attention(q, qi, kiwi, k, v, ki_bf, cache=(), *, past, seq, n_keys, k_interleaved):
    b, tq_pad, _ = kiwi.shape
    l_pad = ki_bf.shape[1]
    n_key_tiles = l_pad // KEY_TILE
    n_cache_tiles = cache[0].shape[1] // (KEY_TILE * N_KV_HEADS) if cache else 0
    topk = min(TOPK_MAX, n_keys // 4)
    qblk = lambda width: pl.BlockSpec((1, Q_TILE, width), lambda bi, i: (bi, i, 0))
    hmblk = lambda heads: pl.BlockSpec((1, 1, heads * Q_TILE, LANES), lambda bi, i: (bi, i, 0, 0))
    kblk = lambda a: pl.BlockSpec((1,) + a.shape[1:], lambda bi, i: (bi, 0, 0), pipeline_mode=pl.Buffered(1))
    kern = functools.partial(_dsa_kernel, past=past, seq=seq, n_keys=n_keys, topk=topk, n_key_tiles=n_key_tiles,
                             k_interleaved=k_interleaved, n_cache_tiles=n_cache_tiles)
    return pl.pallas_call(
        kern,
        grid=(b, tq_pad // Q_TILE),
        in_specs=[hmblk(N_HEADS), hmblk(IDX_HEADS), qblk(LANES), kblk(k), kblk(v), kblk(ki_bf)]
                 + [kblk(c) for c in cache],
        out_specs=qblk(ATTN_DIM),
        out_shape=jax.ShapeDtypeStruct((b, tq_pad, ATTN_DIM), BF16),
        scratch_shapes=[
            pltpu.VMEM((n_key_tiles, N_KV_HEADS, HEAD_DIM + DENOM_ROWS, KEY_TILE), BF16),
            pltpu.VMEM((n_key_tiles, KEY_TILE, Q_TILE), I32),
            pltpu.VMEM((N_HEADS // HEADS_PER_UNIT, HEAD_DIM + DENOM_ROWS, HEADS_PER_UNIT * Q_TILE), F32),
            pltpu.VMEM((N_HEADS // HEADS_PER_UNIT, 1, HEADS_PER_UNIT * Q_TILE), F32),
            pltpu.VMEM((1, Q_TILE), I32),
        ],
        compiler_params=_params("arbitrary", "arbitrary"),
        name="dsa_attention",
    )(q, qi, kiwi, k, v, ki_bf, *cache)


def _mixer_kernel(x_ref, attn_ref, memk_ref, memv_ref, prev_a_ref, prev_b_ref,
                  lng_ref, lnb_ref, cw_ref, ln1g_ref, ln1b_ref,
                  wgb_ref, wgc_ref, wci_ref, wmq_ref, wgl_ref, wao_ref, wco_ref, wmo_ref, wo_ref,
                  x1_ref, state_ref, carry_ref, *, seq, rows):
    i = pl.program_id(0)
    multi_seq = seq < rows
    xn = _layer_norm(x_ref[...], lng_ref[...], lnb_ref[...])
    xb = xn.astype(BF16)

    u = _dot(xb, wgc_ref[...]) * _dot(xb, wci_ref[...])
    ridx = lax.broadcasted_iota(I32, u.shape, 0)
    if multi_seq:
        tpos = ridx & (seq - 1)
        s1 = jnp.where(tpos == 0, prev_a_ref[...], pltpu.roll(u, 1, 0))
        s2 = jnp.where(tpos < 2, prev_b_ref[...], pltpu.roll(u, 2, 0))
        state_ref[...] = u
    else:
        blocks_per_seq = seq // rows
        @pl.when(i % blocks_per_seq == 0)
        def _():
            carry_ref[...] = prev_a_ref[0]

        prev = carry_ref[...]
        p1, p2 = prev[SUBLANES - 1:SUBLANES, :], prev[SUBLANES - 2:SUBLANES - 1, :]
        s1 = jnp.where(ridx == 0, p1, pltpu.roll(u, 1, 0))
        s2 = jnp.where(ridx == 0, p2, jnp.where(ridx == 1, p1, pltpu.roll(u, 2, 0)))
        carry_ref[...] = u[rows - SUBLANES:, :]
        state_ref[0] = u[rows - SUBLANES:, :]
    cw = cw_ref[...]
    conv = cw[0:1, :] * s2 + cw[1:2, :] * s1 + cw[2:3, :] * u
    conv_in = (_dot(xb, wgb_ref[...]) * conv).astype(BF16)

    mq = _dot(xb, wmq_ref[...]).astype(BF16)
    n_batches = max(1, rows // seq)
    rb = rows // n_batches
    mem_rows = []
    for bi in range(n_batches):
        heads = []
        for h in range(MEM_HEADS):
            cols = slice(h * MEM_HEAD_DIM, (h + 1) * MEM_HEAD_DIM)
            s = _dot_nt(mq[bi * rb:(bi + 1) * rb, cols], memk_ref[bi, :, cols]) * (MEM_HEAD_DIM ** -0.5)
            e = jnp.exp(s - jnp.max(s, axis=-1, keepdims=True))
            p = e / jnp.sum(e, axis=-1, keepdims=True)
            heads.append(_dot(p.astype(BF16), memv_ref[bi, :, cols]))
        mem_rows.append(jnp.concatenate(heads, axis=1))
    mem = (mem_rows[0] if n_batches == 1 else jnp.concatenate(mem_rows, axis=0)).astype(BF16)

    gate = _sigmoid(_dot(xb, wgl_ref[...]))
    mixed = (gate[:, :D_MODEL] * _dot(attn_ref[...], wao_ref[...])
             + gate[:, D_MODEL:2 * D_MODEL] * _dot(conv_in, wco_ref[...])
             + gate[:, 2 * D_MODEL:] * _dot(mem, wmo_ref[...]))
    y = ALPHA * xn + _dot(mixed.astype(BF16), wo_ref[...])
    x1_ref[...] = _layer_norm(y, ln1g_ref[...], ln1b_ref[...])


def _mixer(x, attn, mem_k, mem_v, prev_a, prev_b, ln_g, ln_b, conv_w8, ln1_g, ln1_b, w, *, seq, rows):
    n = x.shape[0]
    tm = min(rows, n)
    multi_seq = seq < tm
    n_batches = max(1, tm // seq)
    row = lambda width: pl.BlockSpec((tm, width), lambda i: (i, 0))
    mem = pl.BlockSpec((n_batches, N_MEM, MEM_DIM), lambda i: ((i * tm) // (seq * n_batches), 0, 0))
    vec = _resident((1, D_MODEL))
    if multi_seq:
        prev_spec = row(D_CONV)
        state_spec = row(D_CONV)
        state_shape = jax.ShapeDtypeStruct((n, D_CONV), F32)
    else:
        prev_spec = pl.BlockSpec((1, SUBLANES, D_CONV), lambda i: ((i * tm) // seq, 0, 0))
        state_spec = prev_spec
        state_shape = jax.ShapeDtypeStruct((n // seq, SUBLANES, D_CONV), F32)
    weights = [w[k] for k in ("gb", "gc", "ci", "mq", "gl", "ao", "co", "mo", "o")]
    kern = functools.partial(_mixer_kernel, seq=seq, rows=tm)
    return pl.pallas_call(
        kern,
        grid=(n // tm,),
        in_specs=[row(D_MODEL), row(ATTN_DIM), mem, mem, prev_spec, prev_spec,
                  vec, vec, _resident((SUBLANES, D_CONV)), vec, vec] + [_resident(m.shape) for m in weights],
        out_specs=(row(D_MODEL), state_spec),
        out_shape=(jax.ShapeDtypeStruct((n, D_MODEL), F32), state_shape),
        scratch_shapes=[pltpu.VMEM((SUBLANES, D_CONV), F32)],
        compiler_params=_params("arbitrary"),
        name="mixer",
    )(x, attn, mem_k, mem_v, prev_a, prev_b, ln_g, ln_b, conv_w8, ln1_g, ln1_b, *weights)


def _route(x, wr_hi, wr_lo, bias):
    tm = x.shape[0]
    x_hi = x.astype(BF16)
    x_lo = (x - x_hi.astype(F32)).astype(BF16)
    logits = _dot_nt(wr_hi, x_hi) + (_dot_nt(wr_hi, x_lo) + _dot_nt(wr_lo, x_hi))
    s = _sigmoid(logits)
    sc = s + bias[:, 0:1]
    sub = lax.broadcasted_iota(I32, (GROUP_SIZE, tm), 0)
    neg = -jnp.inf
    groups, gscore = [], []
    for g in range(N_GROUPS):
        xg = sc[g * GROUP_SIZE:(g + 1) * GROUP_SIZE, :]
        m1 = jnp.max(xg, axis=0, keepdims=True)
        i1 = jnp.min(jnp.where(xg == m1, sub, GROUP_SIZE), axis=0, keepdims=True)
        m2 = jnp.max(jnp.where(sub == i1, neg, xg), axis=0, keepdims=True)
        groups.append(xg)
        gscore.append(m1 + m2)
    masked = []
    for g in range(N_GROUPS):
        rank = jnp.zeros((1, tm), I32)
        for o in range(N_GROUPS):
            if o == g:
                continue
            beats = (gscore[o] > gscore[g]) | ((gscore[o] == gscore[g]) & (o < g))
            rank = rank + beats.astype(I32)
        masked.append(jnp.where(rank < TOPK_GROUPS, groups[g], neg))
    v = jnp.concatenate(masked, axis=0)
    eidx = lax.broadcasted_iota(I32, (N_EXPERTS, tm), 0)
    chosen = jnp.zeros((N_EXPERTS, tm), jnp.bool_)
    for _ in range(TOP_K):
        m = jnp.max(v, axis=0, keepdims=True)
        first = jnp.min(jnp.where(v == m, eidx, N_EXPERTS), axis=0, keepdims=True)
        hit = eidx == first
        chosen = chosen | hit
        v = jnp.where(hit, neg, v)
    wsel = jnp.where(chosen, s, 0.0)
    return wsel / jnp.sum(wsel, axis=0, keepdims=True) * ROUTED_SCALE


def _expert_ffn(xb, w_gu, w_down, gate_col=None):
    gu = _dot(xb, w_gu)
    g, u = gu[:, :D_EXPERT], gu[:, D_EXPERT:]
    h = g * _sigmoid(g) * u
    if gate_col is not None:
        h = h * gate_col
    return _dot(h.astype(BF16), w_down)


def _moe_kernel(x_ref, wrh_ref, wrl_ref, br_ref, wgu_ref, wd_ref, wsgu_ref, wsd_ref, g_ref, b_ref,
                o_ref, xb_ref, gate_ref, acc_ref, *, experts_per_step):
    j = pl.program_id(1)

    @pl.when(j == 0)
    def _():
        x = x_ref[...]
        gate_t = _route(x, wrh_ref[...], wrl_ref[...], br_ref[...])
        pad = jnp.zeros((LANES - N_EXPERTS, x.shape[0]), F32)
        gate_ref[...] = jnp.concatenate([gate_t, pad], axis=0).T
        xb = x.astype(BF16)
        xb_ref[...] = xb
        acc_ref[...] = _expert_ffn(xb, wsgu_ref[...], wsd_ref[...])

    xb = xb_ref[...]
    gate = gate_ref[...]
    lane = lax.broadcasted_iota(I32, gate.shape, 1)
    acc = acc_ref[...]
    for e in range(experts_per_step):
        col = jnp.sum(jnp.where(lane == j * experts_per_step + e, gate, 0.0), axis=1, keepdims=True)
        acc = acc + _expert_ffn(xb, wgu_ref[e], wd_ref[e], col)
    acc_ref[...] = acc

    @pl.when(j == pl.num_programs(1) - 1)
    def _():
        o_ref[...] = _layer_norm(ALPHA * x_ref[...] + acc_ref[...], g_ref[...], b_ref[...])


def _moe(x, w, ln_g, ln_b, rows):
    n = x.shape[0]
    tm = min(rows, n)
    eps = MOE_EXPERTS_PER_STEP
    row = pl.BlockSpec((tm, D_MODEL), lambda i, j: (i, 0))
    kern = functools.partial(_moe_kernel, experts_per_step=eps)
    return pl.pallas_call(
        kern,
        grid=(n // tm, N_EXPERTS // eps),
        in_specs=[row, _resident(w["r_hi"].shape), _resident(w["r_lo"].shape), _resident(w["r_b"].shape),
                  pl.BlockSpec((eps, D_MODEL, 2 * D_EXPERT), lambda i, j: (j, 0, 0)),
                  pl.BlockSpec((eps, D_EXPERT, D_MODEL), lambda i, j: (j, 0, 0)),
                  _resident(w["s_gu"].shape), _resident(w["s_d"].shape),
                  _resident((1, D_MODEL)), _resident((1, D_MODEL))],
        out_specs=row,
        out_shape=jax.ShapeDtypeStruct((n, D_MODEL), F32),
        scratch_shapes=[pltpu.VMEM((tm, D_MODEL), BF16), pltpu.VMEM((tm, LANES), F32),
                        pltpu.VMEM((tm, D_MODEL), F32)],
        compiler_params=_params("arbitrary", "arbitrary"),
        name="moe",
    )(x, w["r_hi"], w["r_lo"], w["r_b"], w["e_gu"], w["e_d"], w["s_gu"], w["s_d"], ln_g, ln_b)


def _pad_rows(a, rows):
    return a if a.shape[1] == rows else jnp.pad(a, ((0, 0), (0, rows - a.shape[1]), (0, 0)))


def _head_major_per_batch(x, b, seq, tq_pad):
    heads = x.shape[1]
    if seq % Q_TILE == 0:
        return x.reshape(b, seq // Q_TILE, heads * Q_TILE, LANES)
    per_tile = Q_TILE // seq
    x = x.reshape(-1, heads, per_tile, seq, LANES).transpose(0, 2, 1, 3, 4).reshape(b, heads, seq, LANES)
    x = jnp.pad(x, ((0, 0), (0, 0), (0, tq_pad - seq), (0, 0)))
    return x.reshape(b, 1, heads * tq_pad, LANES)


def _layer(x, mem_k_bf, mem_v_bf, past_k, past_v, past_ki, conv_prev, w, ln, layer):
    b, seq, _ = x.shape
    past = past_k.shape[1]
    n = b * seq
    n_keys = past + seq
    x2 = x.reshape(n, D_MODEL)

    reps = max(1, min(PROJ_ROWS, n) // seq)
    pos_t = np.tile(past + np.arange(seq), reps)
    tabs = (_rope_tables(pos_t, HEAD_DIM // ROT_FRACTION // 2)
            + _rope_tables(pos_t, IDX_DIM // ROT_FRACTION // 2,
                           (IDX_DIM, IDX_DIM + IDX_HEADS), IDX_HEADS ** -0.5))
    q, qi, k, v, kiwi, k_bf, ki_bf = _qkv_project(x2, ln["in_g"], ln["in_b"], w, tabs, PROJ_ROWS, seq)

    l_pad = -(-n_keys // KEY_TILE) * KEY_TILE
    tq_pad = -(-seq // Q_TILE) * Q_TILE
    rows = lambda a, length: a.reshape(b, length * N_KV_HEADS, HEAD_DIM)
    cache = ()
    k_interleaved = past > 0
    if past > 0 and past % KEY_TILE == 0:
        cache = (rows(past_k, past), rows(past_v, past))
        new_rows = (l_pad - past) * N_KV_HEADS
        k3, v3 = _pad_rows(rows(k, seq), new_rows), _pad_rows(rows(v, seq), new_rows)
    else:
        with_cache = lambda new, old: new if past == 0 else jnp.concatenate([rows(old, past), new], axis=1)
        v3 = _pad_rows(with_cache(rows(v, seq), past_v), l_pad * N_KV_HEADS)
        if k_interleaved:
            k3 = _pad_rows(with_cache(rows(k, seq), past_k), l_pad * N_KV_HEADS)
        else:
            k3 = _pad_rows(k_bf.reshape(b, seq, KV_DIM), l_pad)
    ki_past = jnp.pad(past_ki, ((0, 0), (0, 0), (0, LANES - IDX_DIM))).astype(BF16)
    ki3 = jnp.concatenate([ki_past, ki_bf.reshape(b, seq, LANES)], axis=1)
    attn = _dsa_attention(
        _head_major_per_batch(q, b, seq, tq_pad), _head_major_per_batch(qi, b, seq, tq_pad),
        _pad_rows(kiwi.reshape(b, seq, LANES), tq_pad), k3, v3, _pad_rows(ki3, l_pad), cache,
        past=past, seq=seq, n_keys=n_keys, k_interleaved=k_interleaved)
    attn = attn[:, :seq].reshape(n, ATTN_DIM)

    tm = min(PROJ_ROWS, n)
    if seq < tm:
        tpos = jnp.arange(seq)[None, :, None]
        prev_a = jnp.where(tpos == 0, conv_prev[:, 1:2], 0.0).reshape(n, D_CONV)
        prev_b = jnp.where(tpos == 0, conv_prev[:, 0:1], jnp.where(tpos == 1, conv_prev[:, 1:2], 0.0)).reshape(n, D_CONV)
    else:
        prev_a = jnp.pad(conv_prev, ((0, 0), (SUBLANES - (CONV_WIDTH - 1), 0), (0, 0)))
        prev_b = prev_a
    x1, state = _mixer(x2, attn, mem_k_bf, mem_v_bf, prev_a, prev_b, ln["in_g"], ln["in_b"], w["conv"],
                       ln["g1"], ln["b1"], w, seq=seq, rows=PROJ_ROWS)
    if seq < tm:
        new_conv = state.reshape(b, seq, D_CONV)[:, seq - (CONV_WIDTH - 1):]
    else:
        new_conv = state[:, SUBLANES - (CONV_WIDTH - 1):]

    y = _moe(x1, w, ln["g2"], ln["b2"], MOE_ROWS)
    return (y.reshape(b, seq, D_MODEL), k.reshape(b, seq, N_KV_HEADS, HEAD_DIM),
            v.reshape(b, seq, N_KV_HEADS, HEAD_DIM), kiwi[:, :IDX_DIM].reshape(b, seq, IDX_DIM), new_conv)


def _prep_weights(l, w_in, conv_w, w_attn_out, w_conv_out, w_mem_out, w_o, w_router, b_router,
                  w_exp_gu, w_exp_down, w_sh_gu, w_sh_down):
    sizes = [ATTN_DIM, KV_DIM, KV_DIM, IDX_HEADS * IDX_DIM, IDX_DIM, IDX_HEADS,
             D_CONV, D_CONV, D_CONV, MEM_DIM, N_BRANCH * D_MODEL]
    offs = np.concatenate([[0], np.cumsum(sizes)]).tolist()
    wi = w_in[l]
    part = lambda j: wi[:, offs[j]:offs[j + 1]]
    w_qi = part(3).reshape(D_MODEL, IDX_HEADS, IDX_DIM)
    w_qi = jnp.pad(w_qi, ((0, 0), (0, 0), (0, LANES - IDX_DIM))).reshape(D_MODEL, IDX_HEADS * LANES)
    w_kw = jnp.pad(jnp.concatenate([part(4), part(5)], axis=1), ((0, 0), (0, LANES - IDX_DIM - IDX_HEADS)))
    wr = w_router[l].T
    wr_hi = wr.astype(BF16)
    return {
        "q": part(0).astype(BF16), "kv": jnp.concatenate([part(1), part(2)], axis=1).astype(BF16),
        "qi": w_qi.astype(BF16), "kw": w_kw.astype(BF16),
        "gb": part(6).astype(BF16), "gc": part(7).astype(BF16), "ci": part(8).astype(BF16),
        "mq": part(9).astype(BF16), "gl": part(10).astype(BF16),
        "ao": w_attn_out[l].astype(BF16), "co": w_conv_out[l].astype(BF16), "mo": w_mem_out[l].astype(BF16),
        "o": w_o[l].astype(BF16),
        "conv": jnp.pad(conv_w[l], ((0, SUBLANES - CONV_WIDTH), (0, 0))),
        "r_hi": wr_hi, "r_lo": (wr - wr_hi.astype(F32)).astype(BF16),
        "r_b": jnp.broadcast_to(b_router[l][:, None], (N_EXPERTS, LANES)),
        "e_gu": w_exp_gu[l].astype(BF16), "e_d": w_exp_down[l].astype(BF16),
        "s_gu": w_sh_gu[l].astype(BF16), "s_d": w_sh_down[l].astype(BF16),
    }


def kernel(x_prompt, x_sample, mem_prompt, cache_attn_k, cache_attn_v, cache_idx_k, cache_mem_k, cache_mem_v,
           state_conv, ln_in_g, ln_in_b, w_in, conv_w, w_mem_kv, w_attn_out, w_conv_out, w_mem_out, w_o,
           ln1_g, ln1_b, w_router, b_router, w_exp_gu, w_exp_down, w_sh_gu, w_sh_down, ln2_g, ln2_b):
    assert DEPTH == 1 and w_in.shape[0] == 1
    l = 0
    bp, n_mem = mem_prompt.shape[0], mem_prompt.shape[1]
    w = _prep_weights(l, w_in, conv_w, w_attn_out, w_conv_out, w_mem_out, w_o, w_router, b_router,
                      w_exp_gu, w_exp_down, w_sh_gu, w_sh_down)
    ln = {"in_g": ln_in_g[None, :], "in_b": ln_in_b[None, :], "g1": ln1_g[l][None, :], "b1": ln1_b[l][None, :],
          "g2": ln2_g[l][None, :], "b2": ln2_b[l][None, :]}

    mem_kv = _project(mem_prompt.reshape(bp * n_mem, D_MODEL), w_mem_kv[l].astype(BF16))
    mem_k = mem_kv[:, :MEM_DIM].reshape(bp, n_mem, MEM_DIM)
    mem_v = mem_kv[:, MEM_DIM:].reshape(bp, n_mem, MEM_DIM)

    zeros = lambda *s: jnp.zeros(s, F32)
    yp, kp, vp, kip, cp = _layer(
        x_prompt, mem_k.astype(BF16), mem_v.astype(BF16),
        zeros(bp, 0, N_KV_HEADS, HEAD_DIM), zeros(bp, 0, N_KV_HEADS, HEAD_DIM), zeros(bp, 0, IDX_DIM),
        zeros(bp, CONV_WIDTH - 1, D_CONV), w, ln, l)
    bs = x_sample.shape[0]
    ys, ks, vs, kis, cs = _layer(
        x_sample, cache_mem_k[l].reshape(bs, n_mem, MEM_DIM).astype(BF16),
        cache_mem_v[l].reshape(bs, n_mem, MEM_DIM).astype(BF16),
        cache_attn_k[l], cache_attn_v[l], cache_idx_k[l], state_conv[l], w, ln, l)

    mem_shape = (1, bp, n_mem, MEM_HEADS, MEM_HEAD_DIM)
    return (yp, ys, kp[None], vp[None], kip[None], mem_k.reshape(mem_shape), mem_v.reshape(mem_shape), cp[None],
            ks[None], vs[None], kis[None], cs[None])
```
